```python
import math
import jax, jax.numpy as jnp
from jax import lax
import numpy as np

D_MODEL = 2048
BATCH = 2
SEQ = 4096
DEPTH = 4
DEC_BATCH = 8
DEC_SEQ = 4
PAST_LEN = 16384
PAGE_SIZE = 128

N_A_LAYERS = DEPTH // 2
N_B_LAYERS = DEPTH - N_A_LAYERS
PLE_DIM = 256
ROPE_THETA = 500000.0
NORM_EPS = 1e-6
SUBLN_EPS = 1e-5

A_GROUPS = ((128, 1), (512, 4), (2048, 16))
N_GROUPS_A = len(A_GROUPS)
A_HEADS = 8
A_HEAD_DIM = 128
A_BLOCK = 128

B_HEADS = 8
B_QK_DIM = 128
B_V_DIM = 2 * B_QK_DIM
B_QBLOCK = 128

PEER_HEADS = 8
PEER_KEY_DIM = 256
N_KEYS = 128
N_EXPERTS = N_KEYS * N_KEYS
PEER_TOPK = 16
PEER_BLOCK = 128

kernel_name = 'yoco_longnet_diffattn_peer_step'


def rmsnorm(x, g, eps=NORM_EPS):
    xf = x.astype(jnp.float32)
    y = xf * lax.rsqrt(jnp.mean(xf * xf, axis=-1, keepdims=True) + eps)
    return (y * g.astype(jnp.float32)).astype(x.dtype)


def rope_partial(x, pos):
    dh = x.shape[-1]
    rot = dh // 4
    half = rot // 2
    inv_freq = ROPE_THETA ** (-jnp.arange(half, dtype=jnp.float32) / half)
    ang = pos.astype(jnp.float32)[:, None] * inv_freq[None, :]
    shape = (1, pos.shape[0]) + (1,) * (x.ndim - 3) + (half,)
    cos = jnp.cos(ang).reshape(shape)
    sin = jnp.sin(ang).reshape(shape)
    xf = x.astype(jnp.float32)
    x1 = xf[..., :half]
    x2 = xf[..., half:rot]
    out = jnp.concatenate([x1 * cos - x2 * sin, x2 * cos + x1 * sin, xf[..., rot:]], axis=-1)
    return out.astype(x.dtype)


def a_project(h, w_qkv, pos):
    B, S, _ = h.shape
    qkv = (h @ w_qkv).reshape(B, S, N_GROUPS_A, 3, A_HEADS, A_HEAD_DIM)
    q = rope_partial(qkv[:, :, :, 0], pos)
    k = rope_partial(qkv[:, :, :, 1], pos)
    v = qkv[:, :, :, 2]
    return q, k, v


def dilated_group_prompt(q, k, v, dilation, span):
    B, S, H, Dh = q.shape
    n_sub = S // dilation
    nb = -(-n_sub // A_BLOCK)
    pad = nb * A_BLOCK - n_sub

    def to_sub(t):
        t = t.reshape(B, n_sub, dilation, H, Dh).transpose(0, 2, 1, 3, 4)
        return jnp.pad(t, ((0, 0), (0, 0), (0, pad), (0, 0), (0, 0)))

    def band(t):
        tb = jnp.pad(to_sub(t), ((0, 0), (0, 0), (A_BLOCK, 0), (0, 0), (0, 0)))
        tb = tb.reshape(B, dilation, nb + 1, A_BLOCK, H, Dh)
        return jnp.concatenate([tb[:, :, :-1], tb[:, :, 1:]], axis=3)

    qs = to_sub(q).reshape(B, dilation, nb, A_BLOCK, H, Dh)
    kk = band(k)
    vv = band(v)
    s = jnp.einsum('brnqhd,brnkhd->brnhqk', qs, kk, preferred_element_type=jnp.float32) / math.sqrt(Dh)
    qi = jnp.arange(A_BLOCK)[:, None]
    kj = jnp.arange(2 * A_BLOCK)[None, :]
    rel = qi + A_BLOCK - kj
    key_idx = jnp.arange(nb)[:, None, None] * A_BLOCK + kj[None] - A_BLOCK
    valid = (rel >= 0) & (rel <= span) & (key_idx >= 0)
    s = jnp.where(valid[None, None, :, None], s, -jnp.inf)
    m = jnp.max(s, axis=-1, keepdims=True)
    e = jnp.exp(s - m)
    den = jnp.sum(e, axis=-1)
    o = jnp.einsum('brnhqk,brnkhd->brnqhd', e, vv.astype(jnp.float32))
    o = o / den.transpose(0, 1, 2, 4, 3)[..., None]
    lse = (m[..., 0] + jnp.log(den)).transpose(0, 1, 2, 4, 3)
    o = o.reshape(B, dilation, nb * A_BLOCK, H, Dh)[:, :, :n_sub]
    o = o.transpose(0, 2, 1, 3, 4).reshape(B, S, H, Dh)
    lse = lse.reshape(B, dilation, nb * A_BLOCK, H)[:, :, :n_sub]
    lse = lse.transpose(0, 2, 1, 3).reshape(B, S, H)
    return o, lse


def dilated_group_sample(q, k_buf, v_buf, k_new, v_new, dilation, span):
    WB = k_buf.shape[1]
    T = q.shape[1]
    kc = jnp.concatenate([k_buf, k_new], axis=1)
    vc = jnp.concatenate([v_buf, v_new], axis=1)
    idx = WB + jnp.arange(T)[:, None] - dilation * jnp.arange(span + 1)[None, :]
    valid = idx >= 0
    idx = jnp.clip(idx, 0, None)
    kg = kc[:, idx]
    vg = vc[:, idx]
    s = jnp.einsum('bqhd,bqkhd->bhqk', q, kg, preferred_element_type=jnp.float32) / math.sqrt(q.shape[-1])
    s = jnp.where(valid[None, None], s, -jnp.inf)
    m = jnp.max(s, axis=-1, keepdims=True)
    e = jnp.exp(s - m)
    den = jnp.sum(e, axis=-1)
    o = jnp.einsum('bhqk,bqkhd->bqhd', e, vg.astype(jnp.float32)) / den.transpose(0, 2, 1)[..., None]
    lse = (m[..., 0] + jnp.log(den)).transpose(0, 2, 1)
    return o, lse


def combine_groups(h, outs, lses, w_o):
    o = jnp.stack(outs, axis=2)
    w = jax.nn.softmax(jnp.stack(lses, axis=2), axis=2)
    o = jnp.sum(w[..., None] * o, axis=2)
    B, S = o.shape[:2]
    return o.reshape(B, S, A_HEADS * A_HEAD_DIM).astype(h.dtype) @ w_o


def mixer_a_prompt(h, w_qkv, w_o, pos):
    q, k, v = a_project(h, w_qkv, pos)
    S = h.shape[1]
    outs, lses, rows = [], [], []
    for g, (win, dil) in enumerate(A_GROUPS):
        o, l = dilated_group_prompt(q[:, :, g], k[:, :, g], v[:, :, g], dil, win // dil)
        outs.append(o)
        lses.append(l)
        wb = min(win, S)
        rows.append(jnp.stack([k[:, S - wb:, g], v[:, S - wb:, g]], axis=2))
    return combine_groups(h, outs, lses, w_o), rows


def mixer_a_sample(h, bufs, w_qkv, w_o, pos):
    q, k, v = a_project(h, w_qkv, pos)
    outs, lses, rows = [], [], []
    for g, (win, dil) in enumerate(A_GROUPS):
        buf = bufs[g]
        o, l = dilated_group_sample(q[:, :, g], buf[:, :, 0], buf[:, :, 1], k[:, :, g], v[:, :, g], dil, win // dil)
        outs.append(o)
        lses.append(l)
        rows.append(jnp.stack([k[:, :, g], v[:, :, g]], axis=2))
    return combine_groups(h, outs, lses, w_o), rows


def shared_kv(x_res, norm_kv, w_kv, pos):
    B, S, _ = x_res.shape
    kv = rmsnorm(x_res, norm_kv) @ w_kv
    k = rope_partial(kv[..., :B_HEADS * 2 * B_QK_DIM].reshape(B, S, B_HEADS, 2, B_QK_DIM), pos)
    v = kv[..., B_HEADS * 2 * B_QK_DIM:].reshape(B, S, B_HEADS, B_V_DIM)
    rows = jnp.stack([k.reshape(B, S, B_HEADS, 2 * B_QK_DIM), v], axis=2)
    return k, v, rows


def lambda_value(lp, lam_init):
    lp = lp.astype(jnp.float32)
    return jnp.exp(jnp.sum(lp[0] * lp[1])) - jnp.exp(jnp.sum(lp[2] * lp[3])) + lam_init


def diff_softmax_attend(q, segments, lam):
    scale = 1.0 / math.sqrt(B_QK_DIM)
    ss = [jnp.where(mask[None, None, None],
                    jnp.einsum('bqhcd,bkhcd->bhcqk', q, k, preferred_element_type=jnp.float32) * scale,
                    -jnp.inf)
          for k, v, mask in segments]
    p = jax.nn.softmax(jnp.concatenate(ss, axis=-1), axis=-1)
    a = p[:, :, 0] - lam * p[:, :, 1]
    out = None
    off = 0
    for k, v, _ in segments:
        n = k.shape[1]
        part = jnp.einsum('bhqk,bkhd->bqhd', a[..., off:off + n], v.astype(jnp.float32))
        out = part if out is None else out + part
        off += n
    return out


def diff_out(o, g_sub, lam_init, w_o, dtype):
    B, S = o.shape[:2]
    on = o * lax.rsqrt(jnp.mean(o * o, axis=-1, keepdims=True) + SUBLN_EPS) * g_sub.astype(jnp.float32)
    on = on * (1.0 - lam_init)
    return on.reshape(B, S, B_HEADS * B_V_DIM).astype(dtype) @ w_o


def mixer_b_prompt(h, k, v, w_q, lp, g_sub, w_o, lam_init, pos):
    B, S, _ = h.shape
    q = rope_partial((h @ w_q).reshape(B, S, B_HEADS, 2, B_QK_DIM), pos)
    lam = lambda_value(lp, lam_init)
    nq = S // B_QBLOCK
    qblocks = q.reshape(B, nq, B_QBLOCK, B_HEADS, 2, B_QK_DIM).transpose(1, 0, 2, 3, 4, 5)
    kpos = jnp.arange(S)

    def one_block(args):
        qb, bi = args
        qpos = bi * B_QBLOCK + jnp.arange(B_QBLOCK)
        mask = kpos[None, :] <= qpos[:, None]
        return diff_softmax_attend(qb, [(k, v, mask)], lam)

    o = lax.map(one_block, (qblocks, jnp.arange(nq)))
    o = o.transpose(1, 0, 2, 3, 4).reshape(B, S, B_HEADS, B_V_DIM)
    return diff_out(o, g_sub, lam_init, w_o, h.dtype)


def mixer_b_sample(h, k_past, v_past, k_new, v_new, w_q, lp, g_sub, w_o, lam_init, pos):
    DB, T, _ = h.shape
    q = rope_partial((h @ w_q).reshape(DB, T, B_HEADS, 2, B_QK_DIM), pos)
    lam = lambda_value(lp, lam_init)
    mask_past = jnp.ones((T, k_past.shape[1]), dtype=bool)
    mask_new = jnp.arange(T)[None, :] <= jnp.arange(T)[:, None]
    o = diff_softmax_attend(q, [(k_past, v_past, mask_past), (k_new, v_new, mask_new)], lam)
    return diff_out(o, g_sub, lam_init, w_o, h.dtype)


def peer(h, wq, subkeys, u, v):
    B, S, D = h.shape
    n = B * S
    xt = h.reshape(n, D)
    q = (xt @ wq).reshape(n, PEER_HEADS, 2, PEER_KEY_DIM // 2)
    s = jnp.einsum('nhcd,hckd->nhck', q, subkeys, preferred_element_type=jnp.float32)
    sv, si = lax.top_k(s, PEER_TOPK)
    cand = (sv[:, :, 0, :, None] + sv[:, :, 1, None, :]).reshape(n, PEER_HEADS, PEER_TOPK * PEER_TOPK)
    cv, ci = lax.top_k(cand, PEER_TOPK)
    e_idx = (jnp.take_along_axis(si[:, :, 0], ci // PEER_TOPK, axis=-1) * N_KEYS
             + jnp.take_along_axis(si[:, :, 1], ci % PEER_TOPK, axis=-1))
    gate = jax.nn.softmax(cv, axis=-1)
    n_pad = (-n) % PEER_BLOCK
    nb = (n + n_pad) // PEER_BLOCK
    xb = jnp.pad(xt, ((0, n_pad), (0, 0))).reshape(nb, PEER_BLOCK, D)
    eb = jnp.pad(e_idx, ((0, n_pad), (0, 0), (0, 0))).reshape(nb, PEER_BLOCK, PEER_HEADS, PEER_TOPK)
    gb = jnp.pad(gate, ((0, n_pad), (0, 0), (0, 0))).reshape(nb, PEER_BLOCK, PEER_HEADS, PEER_TOPK)

    def block(args):
        xc, ec, gc = args
        ue = u[ec]
        ve = v[ec]
        act = jax.nn.gelu(jnp.einsum('nd,nhkd->nhk', xc, ue, preferred_element_type=jnp.float32), approximate=False)
        return jnp.einsum('nhk,nhkd->nd', gc * act, ve.astype(jnp.float32))

    out = lax.map(block, (xb, eb, gb)).reshape(nb * PEER_BLOCK, D)[:n]
    return out.reshape(B, S, D).astype(h.dtype)


def ple(x, p_i, w_p, g_norm, w_gate):
    gate = jax.nn.sigmoid((rmsnorm(x, g_norm) @ w_gate).astype(jnp.float32))
    return x + ((p_i @ w_p).astype(jnp.float32) * gate).astype(x.dtype)


def setup_inputs(seed: int = 0) -> dict:
    key = jax.random.key(seed)
    ks = iter(jax.random.split(key, 40))

    def nrm(shape, scale):
        return jax.random.normal(next(ks), shape, jnp.float32) * scale

    n_pages = PAST_LEN // PAGE_SIZE
    n_used = DEC_BATCH * n_pages
    n_phys = n_used + (n_used + 3) // 4
    page_table = jax.random.permutation(next(ks), n_phys)[:n_used].reshape(DEC_BATCH, n_pages).astype(jnp.int32)
    a_w = A_HEADS * A_HEAD_DIM
    b_qk = B_HEADS * 2 * B_QK_DIM
    b_v = B_HEADS * B_V_DIM
    return {
        'x_prompt': nrm((BATCH, SEQ, D_MODEL), 1.0),
        'x_sample': nrm((DEC_BATCH, DEC_SEQ, D_MODEL), 1.0),
        'cache_a_w128': nrm((N_A_LAYERS, DEC_BATCH, min(A_GROUPS[0][0], PAST_LEN), 2, A_HEADS, A_HEAD_DIM), 1.0),
        'cache_a_w512': nrm((N_A_LAYERS, DEC_BATCH, min(A_GROUPS[1][0], PAST_LEN), 2, A_HEADS, A_HEAD_DIM), 1.0),
        'cache_a_w2048': nrm((N_A_LAYERS, DEC_BATCH, min(A_GROUPS[2][0], PAST_LEN), 2, A_HEADS, A_HEAD_DIM), 1.0),
        'cache_b_kv': nrm((n_phys, PAGE_SIZE, 2, B_HEADS, B_V_DIM), 1.0),
        'page_table': page_table,
        'p_prompt': nrm((DEPTH, BATCH, SEQ, PLE_DIM), 1.0),
        'p_sample': nrm((DEPTH, DEC_BATCH, DEC_SEQ, PLE_DIM), 1.0),
        'norm_mix': 1.0 + nrm((DEPTH, D_MODEL), 0.02),
        'norm_ffn': 1.0 + nrm((DEPTH, D_MODEL), 0.02),
        'norm_ple': 1.0 + nrm((DEPTH, D_MODEL), 0.02),
        'norm_kv': 1.0 + nrm((D_MODEL,), 0.02),
        'norm_final': 1.0 + nrm((D_MODEL,), 0.02),
        'w_qkv_a': nrm((N_A_LAYERS, D_MODEL, N_GROUPS_A * 3 * a_w), D_MODEL ** -0.5),
        'w_o_a': nrm((N_A_LAYERS, a_w, D_MODEL), a_w ** -0.5),
        'w_kv_b': nrm((D_MODEL, b_qk + b_v), D_MODEL ** -0.5),
        'w_q_b': nrm((N_B_LAYERS, D_MODEL, b_qk), D_MODEL ** -0.5),
        'diff_lambda': nrm((N_B_LAYERS, 4, B_QK_DIM), 0.1),
        'norm_sub_b': 1.0 + nrm((N_B_LAYERS, B_V_DIM), 0.02),
        'w_o_b': nrm((N_B_LAYERS, b_v, D_MODEL), b_v ** -0.5),
        'peer_wq': nrm((DEPTH, D_MODEL, PEER_HEADS * PEER_KEY_DIM), D_MODEL ** -0.5),
        'peer_subkeys': nrm((DEPTH, PEER_HEADS, 2, N_KEYS, PEER_KEY_DIM // 2), (PEER_KEY_DIM // 2) ** -0.5),
        'peer_u': nrm((DEPTH, N_EXPERTS, D_MODEL), D_MODEL ** -0.5),
        'peer_v': nrm((DEPTH, N_EXPERTS, D_MODEL), (PEER_HEADS * PEER_TOPK) ** -0.5),
        'w_ple': nrm((DEPTH, PLE_DIM, D_MODEL), PLE_DIM ** -0.5),
        'w_ple_gate': nrm((DEPTH, D_MODEL, D_MODEL), D_MODEL ** -0.5),
    }


def reference(x_prompt, x_sample, cache_a_w128, cache_a_w512, cache_a_w2048, cache_b_kv, page_table,
              p_prompt, p_sample, norm_mix, norm_ffn, norm_ple, norm_kv, norm_final,
              w_qkv_a, w_o_a, w_kv_b, w_q_b, diff_lambda, norm_sub_b, w_o_b,
              peer_wq, peer_subkeys, peer_u, peer_v, w_ple, w_ple_gate):
    pos_p = jnp.arange(x_prompt.shape[1], dtype=jnp.int32)
    pos_s = PAST_LEN + jnp.arange(x_sample.shape[1], dtype=jnp.int32)
    a_caches = (cache_a_w128, cache_a_w512, cache_a_w2048)

    DB, n_pages = page_table.shape
    past = cache_b_kv[page_table].reshape(DB, n_pages * PAGE_SIZE, 2, B_HEADS, B_V_DIM)
    k_past = past[:, :, 0].reshape(DB, n_pages * PAGE_SIZE, B_HEADS, 2, B_QK_DIM)
    v_past = past[:, :, 1]

    xp, xs = x_prompt, x_sample
    a_rows_p = [[] for _ in A_GROUPS]
    a_rows_s = [[] for _ in A_GROUPS]
    for i in range(DEPTH):
        hp = rmsnorm(xp, norm_mix[i])
        hs = rmsnorm(xs, norm_mix[i])
        if i < N_A_LAYERS:
            yp, rows_p = mixer_a_prompt(hp, w_qkv_a[i], w_o_a[i], pos_p)
            ys, rows_s = mixer_a_sample(hs, [c[i] for c in a_caches], w_qkv_a[i], w_o_a[i], pos_s)
            for g in range(N_GROUPS_A):
                a_rows_p[g].append(rows_p[g])
                a_rows_s[g].append(rows_s[g])
        else:
            j = i - N_A_LAYERS
            if j == 0:
                kp, vp, new_b_kv_prompt = shared_kv(xp, norm_kv, w_kv_b, pos_p)
                kn, vn, new_b_kv_sample = shared_kv(xs, norm_kv, w_kv_b, pos_s)
            lam_init = 0.8 - 0.6 * math.exp(-0.3 * i)
            yp = mixer_b_prompt(hp, kp, vp, w_q_b[j], diff_lambda[j], norm_sub_b[j], w_o_b[j], lam_init, pos_p)
            ys = mixer_b_sample(hs, k_past, v_past, kn, vn, w_q_b[j], diff_lambda[j], norm_sub_b[j], w_o_b[j],
                                lam_init, pos_s)
        xp = xp + yp
        xs = xs + ys
        xp = xp + peer(rmsnorm(xp, norm_ffn[i]), peer_wq[i], peer_subkeys[i], peer_u[i], peer_v[i])
        xs = xs + peer(rmsnorm(xs, norm_ffn[i]), peer_wq[i], peer_subkeys[i], peer_u[i], peer_v[i])
        xp = ple(xp, p_prompt[i], w_ple[i], norm_ple[i], w_ple_gate[i])
        xs = ple(xs, p_sample[i], w_ple[i], norm_ple[i], w_ple_gate[i])

    y_prompt = rmsnorm(xp, norm_final)
    y_sample = rmsnorm(xs, norm_final)
    new_a_w128_prompt = jnp.stack(a_rows_p[0], axis=0)
    new_a_w512_prompt = jnp.stack(a_rows_p[1], axis=0)
    new_a_w2048_prompt = jnp.stack(a_rows_p[2], axis=0)
    new_a_w128_sample = jnp.stack(a_rows_s[0], axis=0)
    new_a_w512_sample = jnp.stack(a_rows_s[1], axis=0)
    new_a_w2048_sample = jnp.stack(a_rows_s[2], axis=0)
    return (y_prompt, y_sample, new_a_w128_prompt, new_a_w512_prompt, new_a_w2048_prompt,
            new_a_w128_sample, new_a_w512_sample, new_a_w2048_sample, new_b_kv_prompt, new_b_kv_sample)
```

```python
import functools
import math

import jax
import jax.numpy as jnp
from jax import lax
from jax.experimental import pallas as pl
from jax.experimental.pallas import tpu as pltpu

BF16 = jnp.bfloat16
F32 = jnp.float32

ROPE_THETA = 500000.0
NORM_EPS = 1e-6
SUBLN_EPS = 1e-5
A_GROUPS = ((128, 1), (512, 4), (2048, 16))
A_HEADS = 8
A_HEAD_DIM = 128
A_BLOCK = 128
B_HEADS = 8
B_QK_DIM = 128
B_V_DIM = 256
PEER_HEADS = 8
N_KEYS = 128
PEER_TOPK = 16
PAGE_SIZE = 128

LANES = 128
SUBLANES = 8
VMEM_LIMIT = 56 * 1024 * 1024
TOKEN_PAD = 512
NOT_SELECTED = 99.0


def _params(sem):
    return pltpu.CompilerParams(dimension_semantics=sem, vmem_limit_bytes=VMEM_LIMIT)


def _rms_kernel(*refs, eps, has_add, out_x, out_h, out_ht, out_y):
    it = iter(refs)
    x_ref = next(it)
    add_ref = next(it) if has_add else None
    g_ref = next(it)
    x = x_ref[...]
    if has_add:
        x = x + add_ref[...].T
    if out_x:
        next(it)[...] = x
    y = x * lax.rsqrt(jnp.mean(x * x, axis=-1, keepdims=True) + eps) * g_ref[...]
    if out_h:
        next(it)[...] = y.astype(BF16)
    if out_ht:
        next(it)[...] = y.T.astype(BF16)
    if out_y:
        next(it)[...] = y


def rms_norm(x, g, *, add_t=None, out_x=False, out_h=False, out_ht=False, out_y=False,
             eps=NORM_EPS, tm=256):
    n, d = x.shape
    grid = (n // tm,)
    row = pl.BlockSpec((tm, d), lambda i: (i, 0))
    col = pl.BlockSpec((d, tm), lambda i: (0, i))
    in_specs = [row]
    args = [x]
    if add_t is not None:
        in_specs.append(col)
        args.append(add_t)
    in_specs.append(pl.BlockSpec((1, d), lambda i: (0, 0)))
    args.append(g.reshape(1, d))
    out_shape, out_specs = [], []
    if out_x:
        out_shape.append(jax.ShapeDtypeStruct((n, d), F32)); out_specs.append(row)
    if out_h:
        out_shape.append(jax.ShapeDtypeStruct((n, d), BF16)); out_specs.append(row)
    if out_ht:
        out_shape.append(jax.ShapeDtypeStruct((d, n), BF16)); out_specs.append(col)
    if out_y:
        out_shape.append(jax.ShapeDtypeStruct((n, d), F32)); out_specs.append(row)
    kern = functools.partial(_rms_kernel, eps=eps, has_add=add_t is not None, out_x=out_x,
                             out_h=out_h, out_ht=out_ht, out_y=out_y)
    return pl.pallas_call(kern, grid=grid, in_specs=in_specs, out_specs=out_specs,
                          out_shape=out_shape, compiler_params=_params(("parallel",)),
                          name="rms_norm")(*args)


def _rope_tile(y, tab):
    c = tab[:, 0:LANES]
    s1 = tab[:, LANES:2 * LANES]
    s2 = tab[:, 2 * LANES:3 * LANES]
    outs = []
    for g in range(y.shape[1] // LANES):
        yg = y[:, g * LANES:(g + 1) * LANES]
        outs.append(yg * c + pltpu.roll(yg, LANES - 16, 1) * s1 + pltpu.roll(yg, 16, 1) * s2)
    return jnp.concatenate(outs, axis=1) if len(outs) > 1 else outs[0]


def _mm_kernel(*refs, mode, rope_fn, tn):
    if mode == "rope":
        x_ref, w_ref, tab_ref, o_ref, wb_ref = refs
    elif mode == "res":
        x_ref, w_ref, r_ref, o_ref, wb_ref = refs
    else:
        x_ref, w_ref, o_ref, wb_ref = refs
    j = pl.program_id(0)

    @pl.when(pl.program_id(1) == 0)
    def _():
        wb_ref[...] = w_ref[...].astype(BF16)

    y = jnp.dot(x_ref[...], wb_ref[...], preferred_element_type=F32)
    if mode == "rope":
        roped = rope_fn(j * tn)

        @pl.when(roped)
        def _():
            o_ref[...] = _rope_tile(y, tab_ref[...])

        @pl.when(jnp.logical_not(roped))
        def _():
            o_ref[...] = y
    elif mode == "res":
        o_ref[...] = r_ref[...] + y
    else:
        o_ref[...] = y


def matmul(x, w, *, mode="plain", tab=None, res=None, rope_fn=None, tm=512, tn=512):
    n, k = x.shape
    m = w.shape[1]
    tm = min(tm, n)
    tn = min(tn, m)
    while n % tm:
        tm //= 2
    grid = (m // tn, n // tm)
    in_specs = [pl.BlockSpec((tm, k), lambda j, i: (i, 0)),
                pl.BlockSpec((k, tn), lambda j, i: (0, j))]
    args = [x, w]
    if mode == "rope":
        in_specs.append(pl.BlockSpec((tm, 3 * LANES), lambda j, i: (i, 0)))
        args.append(tab)
    elif mode == "res":
        in_specs.append(pl.BlockSpec((tm, tn), lambda j, i: (i, j)))
        args.append(res)
    kern = functools.partial(_mm_kernel, mode=mode, rope_fn=rope_fn, tn=tn)
    return pl.pallas_call(
        kern, grid=grid, in_specs=in_specs,
        out_specs=pl.BlockSpec((tm, tn), lambda j, i: (i, j)),
        out_shape=jax.ShapeDtypeStruct((n, m), F32),
        scratch_shapes=[pltpu.VMEM((k, tn), BF16)],
        compiler_params=_params(("arbitrary", "arbitrary")),
        name="matmul_" + mode)(*args)


def _ple_kernel(x_ref, p_ref, hn_ref, wp_ref, wg_ref, o_ref, wgb_ref):
    @pl.when(pl.program_id(1) == 0)
    def _():
        wgb_ref[...] = wg_ref[...].astype(BF16)

    gate = jax.nn.sigmoid(jnp.dot(hn_ref[...], wgb_ref[...], preferred_element_type=F32))
    up = jnp.dot(p_ref[...].astype(BF16), wp_ref[...].astype(BF16), preferred_element_type=F32)
    o_ref[...] = x_ref[...] + up * gate


def ple(x, p, hn, w_p, w_gate, *, tm=512, tn=512):
    n, d = x.shape
    kp = p.shape[1]
    tn = min(tn, d)
    while n % tm:
        tm //= 2
    grid = (d // tn, n // tm)
    return pl.pallas_call(
        _ple_kernel, grid=grid,
        in_specs=[pl.BlockSpec((tm, tn), lambda j, i: (i, j)),
                  pl.BlockSpec((tm, kp), lambda j, i: (i, 0)),
                  pl.BlockSpec((tm, d), lambda j, i: (i, 0)),
                  pl.BlockSpec((kp, tn), lambda j, i: (0, j)),
                  pl.BlockSpec((d, tn), lambda j, i: (0, j))],
        out_specs=pl.BlockSpec((tm, tn), lambda j, i: (i, j)),
        out_shape=jax.ShapeDtypeStruct((n, d), F32),
        scratch_shapes=[pltpu.VMEM((d, tn), BF16)],
        compiler_params=_params(("arbitrary", "arbitrary")),
        name="ple")(x, p, hn, w_p, w_gate)


def _attn_a_kernel(q_ref, kc_ref, kp_ref, vc_ref, vp_ref, o_ref, l_ref):
    n = pl.program_id(2)
    blk = q_ref.shape[0]
    qi = lax.broadcasted_iota(jnp.int32, (blk, blk), 0)
    kj = lax.broadcasted_iota(jnp.int32, (blk, blk), 1)
    mask_prev = jnp.logical_and(kj >= qi, n > 0)
    mask_cur = kj <= qi
    scale = 1.0 / math.sqrt(A_HEAD_DIM)
    nt = (((1,), (1,)), ((), ()))
    for h in range(A_HEADS):
        sl = slice(h * A_HEAD_DIM, (h + 1) * A_HEAD_DIM)
        q = q_ref[:, sl].astype(BF16)
        sp = lax.dot_general(q, kp_ref[:, sl].astype(BF16), nt, preferred_element_type=F32) * scale
        sc = lax.dot_general(q, kc_ref[:, sl].astype(BF16), nt, preferred_element_type=F32) * scale
        sp = jnp.where(mask_prev, sp, -jnp.inf)
        sc = jnp.where(mask_cur, sc, -jnp.inf)
        m = jnp.maximum(jnp.max(sp, axis=-1, keepdims=True), jnp.max(sc, axis=-1, keepdims=True))
        ep = jnp.exp(sp - m)
        ec = jnp.exp(sc - m)
        den = jnp.sum(ep, axis=-1, keepdims=True) + jnp.sum(ec, axis=-1, keepdims=True)
        o = (jnp.dot(ep.astype(BF16), vp_ref[:, sl].astype(BF16), preferred_element_type=F32)
             + jnp.dot(ec.astype(BF16), vc_ref[:, sl].astype(BF16), preferred_element_type=F32))
        o_ref[:, sl] = o / den
        l_ref[:, sl] = jnp.broadcast_to(m + jnp.log(den), (blk, A_HEAD_DIM))


def attn_a_prompt_group(qkv, g, dil, batch, seq):
    n_tot, width = qkv.shape
    hw = A_HEADS * A_HEAD_DIM
    nb = seq // dil // A_BLOCK
    cols = width // hw
    view = qkv.reshape(n_tot // dil, dil * width)

    def spec(which, prev):
        def imap(b, r, n):
            row = b * nb + (jnp.maximum(n - 1, 0) if prev else n)
            return (row, r * cols + g * 3 + which)
        return pl.BlockSpec((A_BLOCK, hw), imap)

    out_spec = pl.BlockSpec((A_BLOCK, hw), lambda b, r, n: (b * nb + n, r))
    out_sds = jax.ShapeDtypeStruct((n_tot // dil, dil * hw), F32)
    o, l = pl.pallas_call(
        _attn_a_kernel, grid=(batch, dil, nb),
        in_specs=[spec(0, False), spec(1, False), spec(1, True), spec(2, False), spec(2, True)],
        out_specs=[out_spec, out_spec], out_shape=[out_sds, out_sds],
        compiler_params=_params(("parallel", "parallel", "arbitrary")),
        name="attn_a_prompt")(view, view, view, view, view)
    return o.reshape(n_tot, hw), l.reshape(n_tot, hw)


def _combine_kernel(o0, o1, o2, l0, l1, l2, out_ref):
    a, b, c = l0[...], l1[...], l2[...]
    m = jnp.maximum(jnp.maximum(a, b), c)
    ea, eb, ec = jnp.exp(a - m), jnp.exp(b - m), jnp.exp(c - m)
    out = (ea * o0[...] + eb * o1[...] + ec * o2[...]) / (ea + eb + ec)
    out_ref[...] = out.astype(BF16)


def combine_groups(os, ls, n_rows, tm=256):
    hw = os[0].shape[1]
    spec = pl.BlockSpec((tm, hw), lambda i: (i, 0))
    return pl.pallas_call(
        _combine_kernel, grid=(n_rows // tm,), in_specs=[spec] * 6, out_specs=spec,
        out_shape=jax.ShapeDtypeStruct((n_rows, hw), BF16),
        compiler_params=_params(("parallel",)), name="combine_groups")(*os, *ls)


def _attn_a_sample_kernel(qkv_ref, c0_ref, c1_ref, c2_ref, o_ref):
    t_new = qkv_ref.shape[1]
    hw = A_HEADS * A_HEAD_DIM
    scale = 1.0 / math.sqrt(A_HEAD_DIM)
    nt = (((1,), (1,)), ((), ()))
    caches = (c0_ref, c1_ref, c2_ref)
    ti = lax.broadcasted_iota(jnp.int32, (t_new, A_BLOCK), 0)
    ni = lax.broadcasted_iota(jnp.int32, (t_new, A_BLOCK), 1)
    tq = lax.broadcasted_iota(jnp.int32, (t_new, t_new), 0)
    tj = lax.broadcasted_iota(jnp.int32, (t_new, t_new), 1)
    for h in range(A_HEADS):
        outs, lses = [], []
        for g, (win, dil) in enumerate(A_GROUPS):
            base = g * 3 * hw + h * A_HEAD_DIM
            q = qkv_ref[0, :, base:base + A_HEAD_DIM].astype(BF16)
            kn = qkv_ref[0, :, base + hw:base + hw + A_HEAD_DIM].astype(BF16)
            vn = qkv_ref[0, :, base + 2 * hw:base + 2 * hw + A_HEAD_DIM].astype(BF16)
            n_sub = min(dil, t_new)
            s_new = lax.dot_general(q, kn, nt, preferred_element_type=F32) * scale
            ok_new = jnp.logical_and(tj <= tq, jnp.bitwise_and(tq - tj, dil - 1) == 0)
            s_new = jnp.where(ok_new, s_new, -jnp.inf)
            ss, vs = [], []
            for r in range(n_sub):
                kb = caches[g][0, :, r * 2 * hw + h * A_HEAD_DIM:r * 2 * hw + (h + 1) * A_HEAD_DIM]
                vb = caches[g][0, :, r * 2 * hw + hw + h * A_HEAD_DIM:r * 2 * hw + hw + (h + 1) * A_HEAD_DIM]
                s = lax.dot_general(q, kb.astype(BF16), nt, preferred_element_type=F32) * scale
                ok = jnp.logical_and(jnp.bitwise_and(ti, dil - 1) == r, ni * dil + r >= ti)
                ss.append(jnp.where(ok, s, -jnp.inf))
                vs.append(vb.astype(BF16))
            m = jnp.max(s_new, axis=-1, keepdims=True)
            for s in ss:
                m = jnp.maximum(m, jnp.max(s, axis=-1, keepdims=True))
            e_new = jnp.exp(s_new - m)
            den = jnp.sum(e_new, axis=-1, keepdims=True)
            o = jnp.dot(e_new.astype(BF16), vn, preferred_element_type=F32)
            for s, vb in zip(ss, vs):
                e = jnp.exp(s - m)
                den = den + jnp.sum(e, axis=-1, keepdims=True)
                o = o + jnp.dot(e.astype(BF16), vb, preferred_element_type=F32)
            outs.append(o / den)
            lses.append(m + jnp.log(den))
        lm = jnp.maximum(jnp.maximum(lses[0], lses[1]), lses[2])
        ws = [jnp.exp(l - lm) for l in lses]
        tot = ws[0] + ws[1] + ws[2]
        comb = (ws[0] * outs[0] + ws[1] * outs[1] + ws[2] * outs[2]) / tot
        o_ref[0, :, h * A_HEAD_DIM:(h + 1) * A_HEAD_DIM] = comb


def attn_a_sample(qkv_s, caches, layer):
    db, t_new, width = qkv_s.shape
    hw = A_HEADS * A_HEAD_DIM
    in_specs = [pl.BlockSpec((1, t_new, width), lambda b: (b, 0, 0))]
    args = [qkv_s]
    for (win, dil), c in zip(A_GROUPS, caches):
        nl = c.shape[0]
        view = c.reshape(nl * db, win // dil, dil * 2 * hw)
        n_sub = min(dil, t_new)
        in_specs.append(pl.BlockSpec((1, win // dil, n_sub * 2 * hw),
                                     lambda b, layer=layer: (layer * db + b, 0, 0)))
        args.append(view)
    return pl.pallas_call(
        _attn_a_sample_kernel, grid=(db,), in_specs=in_specs,
        out_specs=pl.BlockSpec((1, t_new, hw), lambda b: (b, 0, 0)),
        out_shape=jax.ShapeDtypeStruct((db, t_new, hw), F32),
        compiler_params=_params(("parallel",)), name="attn_a_sample")(*args)


def _lambda(lp_ref, lam_init):
    lp = lp_ref[...]
    a = jnp.sum(lp[0:1, :] * lp[1:2, :], axis=-1, keepdims=True)
    b = jnp.sum(lp[2:3, :] * lp[3:4, :], axis=-1, keepdims=True)
    return jnp.exp(a) - jnp.exp(b) + lam_init


def _sub_ln(o, g, lam_init):
    on = o * lax.rsqrt(jnp.mean(o * o, axis=-1, keepdims=True) + SUBLN_EPS) * g
    return on * (1.0 - lam_init)


def _attn_b_kernel(q_ref, k_ref, v_ref, lp_ref, g_ref, o_ref, m_ref, l_ref, acc_ref, *, lam_init, tq, tk):
    qi = pl.program_id(2)
    ki = pl.program_id(3)
    nk = pl.num_programs(3)
    scale = 1.0 / math.sqrt(B_QK_DIM)
    nt = (((1,), (1,)), ((), ()))

    @pl.when(ki == 0)
    def _():
        m_ref[...] = jnp.full(m_ref.shape, -jnp.inf, F32)
        l_ref[...] = jnp.zeros(l_ref.shape, F32)
        acc_ref[...] = jnp.zeros(acc_ref.shape, F32)

    @pl.when(ki * tk <= qi * tq + tq - 1)
    def _():
        rows = qi * tq + lax.broadcasted_iota(jnp.int32, (tq, tk), 0)
        cols = ki * tk + lax.broadcasted_iota(jnp.int32, (tq, tk), 1)
        mask = cols <= rows
        v = v_ref[...].astype(BF16)
        for c in range(2):
            sl = slice(c * B_QK_DIM, (c + 1) * B_QK_DIM)
            s = lax.dot_general(q_ref[:, sl].astype(BF16), k_ref[:, sl].astype(BF16), nt,
                                preferred_element_type=F32) * scale
            s = jnp.where(mask, s, -jnp.inf)
            m_old = m_ref[c]
            m_new = jnp.maximum(m_old, jnp.max(s, axis=-1, keepdims=True))
            alpha = jnp.exp(m_old - m_new)
            p = jnp.exp(s - m_new)
            l_ref[c] = alpha * l_ref[c] + jnp.sum(p, axis=-1, keepdims=True)
            acc_ref[c] = alpha * acc_ref[c] + jnp.dot(p.astype(BF16), v, preferred_element_type=F32)
            m_ref[c] = m_new

    @pl.when(ki == nk - 1)
    def _():
        lam = _lambda(lp_ref, lam_init)
        o = acc_ref[0] / l_ref[0] - lam * (acc_ref[1] / l_ref[1])
        o_ref[...] = _sub_ln(o, g_ref[...], lam_init).astype(BF16)


def attn_b_prompt(q, kv, lp, g_sub, lam_init, batch, seq, *, tq=512, tk=512):
    tq = min(tq, seq)
    tk = min(tk, seq)
    nq, nk = seq // tq, seq // tk
    hd = 2 * B_QK_DIM

    def kmap(b, h, i, j):
        return (b * nk + jnp.minimum(j, (i * tq + tq - 1) // tk), h)

    def vmap_(b, h, i, j):
        return (b * nk + jnp.minimum(j, (i * tq + tq - 1) // tk), B_HEADS + h)

    kern = functools.partial(_attn_b_kernel, lam_init=lam_init, tq=tq, tk=tk)
    return pl.pallas_call(
        kern, grid=(batch, B_HEADS, nq, nk),
        in_specs=[pl.BlockSpec((tq, hd), lambda b, h, i, j: (b * nq + i, h)),
                  pl.BlockSpec((tk, hd), kmap),
                  pl.BlockSpec((tk, B_V_DIM), vmap_),
                  pl.BlockSpec((4, B_QK_DIM), lambda b, h, i, j: (0, 0)),
                  pl.BlockSpec((1, B_V_DIM), lambda b, h, i, j: (0, 0))],
        out_specs=pl.BlockSpec((tq, B_V_DIM), lambda b, h, i, j: (b * nq + i, h)),
        out_shape=jax.ShapeDtypeStruct((batch * seq, B_HEADS * B_V_DIM), BF16),
        scratch_shapes=[pltpu.VMEM((2, tq, 1), F32), pltpu.VMEM((2, tq, 1), F32),
                        pltpu.VMEM((2, tq, B_V_DIM), F32)],
        compiler_params=_params(("parallel", "parallel", "parallel", "arbitrary")),
        name="attn_b_prompt")(q, kv, kv, lp, g_sub.reshape(1, B_V_DIM))


def _attn_b_sample_kernel(pt_ref, qbd_ref, page_ref, kvn_ref, lp_ref, g_ref, o_ref,
                          m_ref, l_ref, acc_ref, *, lam_init, t_new):
    p = pl.program_id(1)
    n_pages = pl.num_programs(1)
    scale = 1.0 / math.sqrt(B_QK_DIM)
    kw = B_HEADS * 2 * B_QK_DIM
    nt = (((1,), (1,)), ((), ()))
    rows_per_head = 2 * t_new

    @pl.when(p == 0)
    def _():
        m_ref[...] = jnp.full(m_ref.shape, -jnp.inf, F32)
        l_ref[...] = jnp.zeros(l_ref.shape, F32)
        acc_ref[...] = jnp.zeros(acc_ref.shape, F32)

    qbd = qbd_ref[0]

    def update(s, vals):
        m_old = m_ref[...]
        m_new = jnp.maximum(m_old, jnp.max(s, axis=-1, keepdims=True))
        alpha = jnp.exp(m_old - m_new)
        e = jnp.exp(s - m_new)
        l_ref[...] = alpha * l_ref[...] + jnp.sum(e, axis=-1, keepdims=True)
        m_ref[...] = m_new
        eb = e.astype(BF16)
        for h in range(B_HEADS):
            rs = slice(h * rows_per_head, (h + 1) * rows_per_head)
            pv = jnp.dot(eb[rs, :], vals[:, h * B_V_DIM:(h + 1) * B_V_DIM].astype(BF16),
                         preferred_element_type=F32)
            acc_ref[rs, :] = alpha[rs, :] * acc_ref[rs, :] + pv

    keys = page_ref[0, :, 0:kw].astype(BF16)
    s = lax.dot_general(qbd, keys, nt, preferred_element_type=F32) * scale
    update(s, page_ref[0, :, kw:2 * kw])

    @pl.when(p == n_pages - 1)
    def _():
        kn = kvn_ref[0, :, 0:kw].astype(BF16)
        sn = lax.dot_general(qbd, kn, nt, preferred_element_type=F32) * scale
        rq = lax.broadcasted_iota(jnp.int32, sn.shape, 0) % t_new
        cj = lax.broadcasted_iota(jnp.int32, sn.shape, 1)
        update(jnp.where(cj <= rq, sn, -jnp.inf), kvn_ref[0, :, kw:2 * kw])
        lam = _lambda(lp_ref, lam_init)
        on = acc_ref[...] / l_ref[...]
        for h in range(B_HEADS):
            r0 = h * rows_per_head
            o = on[r0:r0 + t_new, :] - lam * on[r0 + t_new:r0 + 2 * t_new, :]
            o_ref[0, :, h * B_V_DIM:(h + 1) * B_V_DIM] = _sub_ln(o, g_ref[...], lam_init)


def attn_b_sample(q_s, kv_s, cache_b_kv, page_table, lp, g_sub, lam_init):
    db, t_new, qw = q_s.shape
    n_pages = page_table.shape[1]
    n_phys = cache_b_kv.shape[0]
    kvw = 2 * B_HEADS * B_V_DIM
    pages = cache_b_kv.reshape(n_phys, PAGE_SIZE, kvw)
    n_hc = B_HEADS * 2
    q5 = q_s.reshape(db, t_new, n_hc, B_QK_DIM).transpose(0, 2, 1, 3)
    eye = jnp.eye(n_hc, dtype=F32)
    qbd = (q5[:, :, :, None, :] * eye[None, :, None, :, None]).reshape(db, n_hc * t_new, n_hc * B_QK_DIM)
    qbd = qbd.astype(BF16)
    rows = n_hc * t_new
    kern = functools.partial(_attn_b_sample_kernel, lam_init=lam_init, t_new=t_new)
    grid_spec = pltpu.PrefetchScalarGridSpec(
        num_scalar_prefetch=1, grid=(db, n_pages),
        in_specs=[pl.BlockSpec((1, rows, qw), lambda b, p, pt: (b, 0, 0)),
                  pl.BlockSpec((1, PAGE_SIZE, kvw), lambda b, p, pt: (pt[b, p], 0, 0)),
                  pl.BlockSpec((1, t_new, kvw), lambda b, p, pt: (b, 0, 0)),
                  pl.BlockSpec((4, B_QK_DIM), lambda b, p, pt: (0, 0)),
                  pl.BlockSpec((1, B_V_DIM), lambda b, p, pt: (0, 0))],
        out_specs=pl.BlockSpec((1, t_new, qw), lambda b, p, pt: (b, 0, 0)),
        scratch_shapes=[pltpu.VMEM((rows, 1), F32), pltpu.VMEM((rows, 1), F32),
                        pltpu.VMEM((rows, B_V_DIM), F32)])
    return pl.pallas_call(
        kern, grid_spec=grid_spec,
        out_shape=jax.ShapeDtypeStruct((db, t_new, qw), F32),
        compiler_params=_params(("parallel", "arbitrary")),
        name="attn_b_sample")(page_table, qbd, pages, kv_s, lp, g_sub.reshape(1, B_V_DIM))


def _top_ranks(s):
    iota = lax.broadcasted_iota(jnp.int32, s.shape, 0).astype(F32)
    rank = jnp.full(s.shape, NOT_SELECTED, F32)
    vals = []
    work = s
    for a in range(PEER_TOPK):
        m = jnp.max(work, axis=0, keepdims=True)
        idx = jnp.min(jnp.where(work == m, iota, float(N_KEYS)), axis=0, keepdims=True)
        hit = iota == idx
        rank = jnp.where(hit, float(a), rank)
        work = jnp.where(hit, -jnp.inf, work)
        vals.append(m)
    return vals, rank


def _candidate_rows():
    pieces = [(0, 0, 16, 16)]
    for a in range(1, 8):
        pieces.append((a, 0, 8, PEER_TOPK // (a + 1)))
    return pieces


def _route_kernel(ht_ref, wq_ref, sk_ref, r1_ref, e1_ref, bq_out_ref, c0_out_ref, qt_ref, bq_ref, c0_ref):
    n_chunks = ht_ref.shape[1] // LANES
    qt_ref[...] = jnp.dot(wq_ref[...], ht_ref[...], preferred_element_type=F32)
    neg = -jnp.inf
    t = LANES

    def body(it, carry):
        h = it // n_chunks
        cs = pl.ds(pl.multiple_of((it % n_chunks) * LANES, LANES), LANES)
        vals, ranks, scores = [], [], []
        for c in range(2):
            hc = h * 2 + c
            qhc = qt_ref[pl.ds(pl.multiple_of(hc * N_KEYS, N_KEYS), N_KEYS), cs].astype(BF16)
            s = jnp.dot(sk_ref[hc].astype(BF16), qhc, preferred_element_type=F32)
            v, r = _top_ranks(s)
            vals.append(v); ranks.append(r); scores.append(s)
        v0, v1 = vals
        v1_16 = jnp.concatenate(v1, axis=0)
        v0_hi = jnp.concatenate(v0[8:16], axis=0)
        cands, flats = [], []
        for a, _, rows, nvalid in _candidate_rows():
            b_iota = lax.broadcasted_iota(jnp.int32, (rows, t), 0)
            cs_ab = v0[a] + v1_16[0:rows, :]
            cands.append(jnp.where(b_iota < nvalid, cs_ab, neg))
            flats.append((b_iota + a * PEER_TOPK).astype(F32))
        cands.append(v0_hi + v1[0])
        flats.append(((lax.broadcasted_iota(jnp.int32, (8, t), 0) + 8) * PEER_TOPK).astype(F32))
        cand = jnp.concatenate(cands, axis=0)
        flat = jnp.concatenate(flats, axis=0)
        big = float(PEER_TOPK * PEER_TOPK)
        work = cand
        sel = jnp.zeros(cand.shape, F32)
        for _ in range(PEER_TOPK):
            m = jnp.max(work, axis=0, keepdims=True)
            idx = jnp.min(jnp.where(work == m, flat, big), axis=0, keepdims=True)
            hit = flat == idx
            sel = jnp.where(hit, 1.0, sel)
            work = jnp.where(hit, neg, work)
        top = v0[0] + v1[0]
        z = jnp.sum(jnp.where(sel > 0.0, jnp.exp(cand - top), 0.0), axis=0, keepdims=True)
        counts = [jnp.sum(sel[0:16, :], axis=0, keepdims=True)]
        for k in range(1, 8):
            counts.append(jnp.sum(sel[8 + 8 * k:16 + 8 * k, :], axis=0, keepdims=True))
        hi = sel[72:80, :]
        bq = jnp.zeros(ranks[0].shape, F32)
        for a in range(PEER_TOPK):
            cnt = counts[a] if a < 8 else hi[a - 8:a - 7, :]
            bq = jnp.where(ranks[0] == float(a), cnt, bq)
        rs = pl.ds(pl.multiple_of(h * N_KEYS, N_KEYS), N_KEYS)
        bq_ref[rs, cs] = bq
        c0_ref[rs, cs] = jnp.exp(scores[0] - v0[0]) / z
        r1_ref[rs, cs] = ranks[1]
        e1_ref[rs, cs] = jnp.exp(scores[1] - v1[0])
        return carry

    lax.fori_loop(0, PEER_HEADS * n_chunks, body, 0)
    for h in range(PEER_HEADS):
        bq_out_ref[:, h, :] = bq_ref[h * N_KEYS:(h + 1) * N_KEYS, :]
        c0_out_ref[:, h, :] = c0_ref[h * N_KEYS:(h + 1) * N_KEYS, :]


def peer_route(ht, wq_t, subkeys, *, tm=256):
    d, n = ht.shape
    rows = PEER_HEADS * N_KEYS
    slab = pl.BlockSpec((rows, tm), lambda i: (0, i))
    sds = jax.ShapeDtypeStruct((rows, n), F32)
    slab3 = pl.BlockSpec((N_KEYS, PEER_HEADS, tm), lambda i: (0, 0, i))
    sds3 = jax.ShapeDtypeStruct((N_KEYS, PEER_HEADS, n), F32)
    return pl.pallas_call(
        _route_kernel, grid=(n // tm,),
        in_specs=[pl.BlockSpec((d, tm), lambda i: (0, i)),
                  pl.BlockSpec(wq_t.shape, lambda i: (0, 0)),
                  pl.BlockSpec(subkeys.shape, lambda i: (0, 0, 0))],
        out_specs=[slab, slab, slab3, slab3], out_shape=[sds, sds, sds3, sds3],
        scratch_shapes=[pltpu.VMEM((wq_t.shape[0], tm), F32), pltpu.VMEM((rows, tm), F32),
                        pltpu.VMEM((rows, tm), F32)],
        compiler_params=_params(("parallel",)), name="peer_route")(ht, wq_t, subkeys)


def _gelu(a):
    return 0.5 * a * (1.0 + lax.erf(a * math.sqrt(0.5)))


def _expert_kernel(ht_ref, u_ref, vt_ref, r1_ref, e1_ref, bq_ref, c0_ref, o_ref, a_ref, g_ref, *, te, tm):
    e = pl.program_id(1)
    rows_per_step = te // N_KEYS
    n_chunks = tm // LANES

    @pl.when(e == 0)
    def _():
        o_ref[...] = jnp.zeros(o_ref.shape, F32)

    a_ref[...] = jnp.dot(u_ref[...], ht_ref[...], preferred_element_type=F32)

    def body(it, carry):
        r = it // n_chunks
        c = it % n_chunks
        i = e * rows_per_step + r
        cs = pl.ds(pl.multiple_of(c * LANES, LANES), LANES)
        rs = pl.ds(pl.multiple_of(r * N_KEYS, N_KEYS), N_KEYS)
        w = jnp.zeros((N_KEYS, LANES), F32)
        bq_all = bq_ref[i, :, cs]
        c0_all = c0_ref[i, :, cs]
        for h in range(PEER_HEADS):
            bq = bq_all[h:h + 1, :]
            c0 = c0_all[h:h + 1, :]
            hs = pl.ds(h * N_KEYS, N_KEYS)
            w = w + jnp.where(r1_ref[hs, cs] < bq, e1_ref[hs, cs] * c0, 0.0)
        g_ref[rs, cs] = (_gelu(a_ref[rs, cs]) * w).astype(BF16)
        return carry

    lax.fori_loop(0, rows_per_step * n_chunks, body, 0)
    o_ref[...] += jnp.dot(vt_ref[...], g_ref[...], preferred_element_type=F32)


def peer_experts(ht, u_bf, vt_bf, slabs, *, tm=512, te=1024):
    d, n = ht.shape
    n_exp = u_bf.shape[0]
    while n % tm:
        tm //= 2
    te = min(te, n_exp)
    rows = PEER_HEADS * N_KEYS
    slab = pl.BlockSpec((rows, tm), lambda i, e: (0, i))
    slab3 = pl.BlockSpec((N_KEYS, PEER_HEADS, tm), lambda i, e: (0, 0, i))
    kern = functools.partial(_expert_kernel, te=te, tm=tm)
    return pl.pallas_call(
        kern, grid=(n // tm, n_exp // te),
        in_specs=[pl.BlockSpec((d, tm), lambda i, e: (0, i)),
                  pl.BlockSpec((te, d), lambda i, e: (e, 0)),
                  pl.BlockSpec((d, te), lambda i, e: (0, e)),
                  slab, slab, slab3, slab3],
        out_specs=pl.BlockSpec((d, tm), lambda i, e: (0, i)),
        out_shape=jax.ShapeDtypeStruct((d, n), F32),
        scratch_shapes=[pltpu.VMEM((te, tm), F32), pltpu.VMEM((te, tm), BF16)],
        compiler_params=_params(("parallel", "arbitrary")),
        name="peer_experts")(ht, u_bf, vt_bf, *slabs)


def _rope_table(pos):
    rot = A_HEAD_DIM // 4
    half = rot // 2
    inv_freq = ROPE_THETA ** (-jnp.arange(half, dtype=F32) / half)
    ang = pos.astype(F32)[:, None] * inv_freq[None, :]
    cos, sin = jnp.cos(ang), jnp.sin(ang)
    n = pos.shape[0]
    ones = jnp.ones((n, A_HEAD_DIM - rot), F32)
    zeros = jnp.zeros((n, A_HEAD_DIM - rot), F32)
    zh = jnp.zeros((n, half), F32)
    return jnp.concatenate([cos, cos, ones, -sin, zh, zeros, zh, sin, zeros], axis=1)


def kernel(x_prompt, x_sample, cache_a_w128, cache_a_w512, cache_a_w2048, cache_b_kv, page_table,
           p_prompt, p_sample, norm_mix, norm_ffn, norm_ple, norm_kv, norm_final,
           w_qkv_a, w_o_a, w_kv_b, w_q_b, diff_lambda, norm_sub_b, w_o_b,
           peer_wq, peer_subkeys, peer_u, peer_v, w_ple, w_ple_gate):
    batch, seq, d = x_prompt.shape
    db, t_new, _ = x_sample.shape
    depth = norm_mix.shape[0]
    n_a = w_qkv_a.shape[0]
    past_len = page_table.shape[1] * PAGE_SIZE
    n_p, n_s = batch * seq, db * t_new
    n_tot = -(-(n_p + n_s) // TOKEN_PAD) * TOKEN_PAD
    pad = n_tot - n_p - n_s
    a_caches = (cache_a_w128, cache_a_w512, cache_a_w2048)
    hw = A_HEADS * A_HEAD_DIM

    def tokens(prompt_part, sample_part):
        w = prompt_part.shape[-1]
        return jnp.concatenate([prompt_part.reshape(n_p, w), sample_part.reshape(n_s, w),
                                jnp.zeros((pad, w), prompt_part.dtype)], axis=0)

    x = tokens(x_prompt, x_sample)
    pos = jnp.concatenate([jnp.tile(jnp.arange(seq, dtype=jnp.int32), batch),
                           jnp.tile(past_len + jnp.arange(t_new, dtype=jnp.int32), db),
                           jnp.zeros((pad,), jnp.int32)])
    tab = _rope_table(pos)

    a_rows_p = [[] for _ in A_GROUPS]
    a_rows_s = [[] for _ in A_GROUPS]
    new_b_kv_prompt = new_b_kv_sample = kv = None
    for i in range(depth):
        h = rms_norm(x, norm_mix[i], out_h=True)[0]
        if i < n_a:
            qkv = matmul(h, w_qkv_a[i], mode="rope", tab=tab,
                         rope_fn=lambda col: (col // hw) % 3 != 2)
            os, ls = [], []
            for g, (win, dil) in enumerate(A_GROUPS):
                o_g, l_g = attn_a_prompt_group(qkv, g, dil, batch, seq)
                os.append(o_g); ls.append(l_g)
            o_p = combine_groups(os, ls, n_p)
            qkv_s = qkv[n_p:n_p + n_s].reshape(db, t_new, qkv.shape[1])
            o_s = attn_a_sample(qkv_s, a_caches, i).reshape(n_s, hw).astype(BF16)
            o_all = jnp.concatenate([o_p, o_s, jnp.zeros((pad, hw), BF16)], axis=0)
            x = matmul(o_all, w_o_a[i], mode="res", res=x)
            q5p = qkv[:n_p].reshape(batch, seq, len(A_GROUPS), 3, A_HEADS, A_HEAD_DIM)
            q5s = qkv_s.reshape(db, t_new, len(A_GROUPS), 3, A_HEADS, A_HEAD_DIM)
            for g, (win, dil) in enumerate(A_GROUPS):
                wb = min(win, seq)
                a_rows_p[g].append(q5p[:, seq - wb:, g, 1:3])
                a_rows_s[g].append(q5s[:, :, g, 1:3])
        else:
            j = i - n_a
            if j == 0:
                hkv = rms_norm(x, norm_kv, out_h=True)[0]
                kw = B_HEADS * 2 * B_QK_DIM
                kv = matmul(hkv, w_kv_b, mode="rope", tab=tab, rope_fn=lambda col: col < kw)
                new_b_kv_prompt = kv[:n_p].reshape(batch, seq, 2, B_HEADS, B_V_DIM)
                new_b_kv_sample = kv[n_p:n_p + n_s].reshape(db, t_new, 2, B_HEADS, B_V_DIM)
            lam_init = 0.8 - 0.6 * math.exp(-0.3 * i)
            q = matmul(h, w_q_b[j], mode="rope", tab=tab, rope_fn=lambda col: col >= 0)
            o_p = attn_b_prompt(q, kv, diff_lambda[j], norm_sub_b[j], lam_init, batch, seq)
            q_s = q[n_p:n_p + n_s].reshape(db, t_new, q.shape[1])
            kv_s = kv[n_p:n_p + n_s].reshape(db, t_new, kv.shape[1])
            o_s = attn_b_sample(q_s, kv_s, cache_b_kv, page_table, diff_lambda[j], norm_sub_b[j], lam_init)
            o_all = jnp.concatenate([o_p, o_s.reshape(n_s, -1).astype(BF16),
                                     jnp.zeros((pad, o_p.shape[1]), BF16)], axis=0)
            x = matmul(o_all, w_o_b[j], mode="res", res=x)
        ht = rms_norm(x, norm_ffn[i], out_ht=True)[0]
        wq_t = peer_wq[i].T.astype(BF16)
        sk = peer_subkeys[i].reshape(PEER_HEADS * 2, N_KEYS, -1)
        slabs = peer_route(ht, wq_t, sk)
        peer_t = peer_experts(ht, peer_u[i].astype(BF16), peer_v[i].T.astype(BF16), slabs)
        x, hn = rms_norm(x, norm_ple[i], add_t=peer_t, out_x=True, out_h=True)
        x = ple(x, tokens(p_prompt[i], p_sample[i]), hn, w_ple[i], w_ple_gate[i])

    y = rms_norm(x, norm_final, out_y=True)[0]
    y_prompt = y[:n_p].reshape(batch, seq, d)
    y_sample = y[n_p:n_p + n_s].reshape(db, t_new, d)
    outs_p = [jnp.stack(r, axis=0) for r in a_rows_p]
    outs_s = [jnp.stack(r, axis=0) for r in a_rows_s]
    return (y_prompt, y_sample, *outs_p, *outs_s, new_b_kv_prompt, new_b_kv_sample)
```

```python
import functools
import math

import jax
import jax.numpy as jnp
from jax import lax
from jax.experimental import pallas as pl
from jax.experimental.pallas import tpu as pltpu

BF16 = jnp.bfloat16
F32 = jnp.float32

ROPE_THETA = 500000.0
NORM_EPS = 1e-6
SUBLN_EPS = 1e-5
A_GROUPS = ((128, 1), (512, 4), (2048, 16))
A_HEADS = 8
A_HEAD_DIM = 128
A_BLOCK = 128
B_HEADS = 8
B_QK_DIM = 128
B_V_DIM = 256
PEER_HEADS = 8
N_KEYS = 128
PEER_TOPK = 16
PAGE_SIZE = 128

LANES = 128
SUBLANES = 8
VMEM_LIMIT = 56 * 1024 * 1024
TOKEN_PAD = 512
NOT_SELECTED = 99.0


def _params(sem):
    return pltpu.CompilerParams(dimension_semantics=sem, vmem_limit_bytes=VMEM_LIMIT)


def _rms_kernel(*refs, eps, has_add, out_x, out_h, out_ht, out_y):
    it = iter(refs)
    x_ref = next(it)
    add_ref = next(it) if has_add else None
    g_ref = next(it)
    x = x_ref[...]
    if has_add:
        x = x + add_ref[...].T
    if out_x:
        next(it)[...] = x
    y = x * lax.rsqrt(jnp.mean(x * x, axis=-1, keepdims=True) + eps) * g_ref[...]
    if out_h:
        next(it)[...] = y.astype(BF16)
    if out_ht:
        next(it)[...] = y.T.astype(BF16)
    if out_y:
        next(it)[...] = y


def rms_norm(x, g, *, add_t=None, out_x=False, out_h=False, out_ht=False, out_y=False,
             eps=NORM_EPS, tm=256):
    n, d = x.shape
    grid = (n // tm,)
    row = pl.BlockSpec((tm, d), lambda i: (i, 0))
    col = pl.BlockSpec((d, tm), lambda i: (0, i))
    in_specs = [row]
    args = [x]
    if add_t is not None:
        in_specs.append(col)
        args.append(add_t)
    in_specs.append(pl.BlockSpec((1, d), lambda i: (0, 0)))
    args.append(g.reshape(1, d))
    out_shape, out_specs = [], []
    if out_x:
        out_shape.append(jax.ShapeDtypeStruct((n, d), F32)); out_specs.append(row)
    if out_h:
        out_shape.append(jax.ShapeDtypeStruct((n, d), BF16)); out_specs.append(row)
    if out_ht:
        out_shape.append(jax.ShapeDtypeStruct((d, n), BF16)); out_specs.append(col)
    if out_y:
        out_shape.append(jax.ShapeDtypeStruct((n, d), F32)); out_specs.append(row)
    kern = functools.partial(_rms_kernel, eps=eps, has_add=add_t is not None, out_x=out_x,
                             out_h=out_h, out_ht=out_ht, out_y=out_y)
    return pl.pallas_call(kern, grid=grid, in_specs=in_specs, out_specs=out_specs,
                          out_shape=out_shape, compiler_params=_params(("parallel",)),
                          name="rms_norm")(*args)


def _rope_tile(y, tab):
    c = tab[:, 0:LANES]
    s1 = tab[:, LANES:2 * LANES]
    s2 = tab[:, 2 * LANES:3 * LANES]
    outs = []
    for g in range(y.shape[1] // LANES):
        yg = y[:, g * LANES:(g + 1) * LANES]
        outs.append(yg * c + pltpu.roll(yg, LANES - 16, 1) * s1 + pltpu.roll(yg, 16, 1) * s2)
    return jnp.concatenate(outs, axis=1) if len(outs) > 1 else outs[0]


def _mm_kernel(*refs, mode, rope_fn, tn):
    if mode == "rope":
        x_ref, w_ref, tab_ref, o_ref, wb_ref = refs
    elif mode == "res":
        x_ref, w_ref, r_ref, o_ref, wb_ref = refs
    else:
        x_ref, w_ref, o_ref, wb_ref = refs
    j = pl.program_id(0)

    @pl.when(pl.program_id(1) == 0)
    def _():
        wb_ref[...] = w_ref[...].astype(BF16)

    y = jnp.dot(x_ref[...], wb_ref[...], preferred_element_type=F32)
    if mode == "rope":
        roped = rope_fn(j * tn)

        @pl.when(roped)
        def _():
            o_ref[...] = _rope_tile(y, tab_ref[...])

        @pl.when(jnp.logical_not(roped))
        def _():
            o_ref[...] = y
    elif mode == "res":
        o_ref[...] = r_ref[...] + y
    else:
        o_ref[...] = y


def matmul(x, w, *, mode="plain", tab=None, res=None, rope_fn=None, tm=512, tn=512):
    n, k = x.shape
    m = w.shape[1]
    tm = min(tm, n)
    tn = min(tn, m)
    while n % tm:
        tm //= 2
    grid = (m // tn, n // tm)
    in_specs = [pl.BlockSpec((tm, k), lambda j, i: (i, 0)),
                pl.BlockSpec((k, tn), lambda j, i: (0, j))]
    args = [x, w]
    if mode == "rope":
        in_specs.append(pl.BlockSpec((tm, 3 * LANES), lambda j, i: (i, 0)))
        args.append(tab)
    elif mode == "res":
        in_specs.append(pl.BlockSpec((tm, tn), lambda j, i: (i, j)))
        args.append(res)
    kern = functools.partial(_mm_kernel, mode=mode, rope_fn=rope_fn, tn=tn)
    return pl.pallas_call(
        kern, grid=grid, in_specs=in_specs,
        out_specs=pl.BlockSpec((tm, tn), lambda j, i: (i, j)),
        out_shape=jax.ShapeDtypeStruct((n, m), F32),
        scratch_shapes=[pltpu.VMEM((k, tn), BF16)],
        compiler_params=_params(("arbitrary", "arbitrary")),
        name="matmul_" + mode)(*args)


def _ple_kernel(x_ref, p_ref, hn_ref, wp_ref, wg_ref, o_ref, wgb_ref):
    @pl.when(pl.program_id(1) == 0)
    def _():
        wgb_ref[...] = wg_ref[...].astype(BF16)

    gate = jax.nn.sigmoid(jnp.dot(hn_ref[...], wgb_ref[...], preferred_element_type=F32))
    up = jnp.dot(p_ref[...].astype(BF16), wp_ref[...].astype(BF16), preferred_element_type=F32)
    o_ref[...] = x_ref[...] + up * gate


def ple(x, p, hn, w_p, w_gate, *, tm=512, tn=512):
    n, d = x.shape
    kp = p.shape[1]
    tn = min(tn, d)
    while n % tm:
        tm //= 2
    grid = (d // tn, n // tm)
    return pl.pallas_call(
        _ple_kernel, grid=grid,
        in_specs=[pl.BlockSpec((tm, tn), lambda j, i: (i, j)),
                  pl.BlockSpec((tm, kp), lambda j, i: (i, 0)),
                  pl.BlockSpec((tm, d), lambda j, i: (i, 0)),
                  pl.BlockSpec((kp, tn), lambda j, i: (0, j)),
                  pl.BlockSpec((d, tn), lambda j, i: (0, j))],
        out_specs=pl.BlockSpec((tm, tn), lambda j, i: (i, j)),
        out_shape=jax.ShapeDtypeStruct((n, d), F32),
        scratch_shapes=[pltpu.VMEM((d, tn), BF16)],
        compiler_params=_params(("arbitrary", "arbitrary")),
        name="ple")(x, p, hn, w_p, w_gate)


def _attn_a_kernel(*refs):
    n_in = 5 * len(A_GROUPS)
    o_ref, og_ref, lg_ref = refs[n_in:n_in + 3]
    blk = pl.program_id(1)
    t_blk = o_ref.shape[0]
    qi = lax.broadcasted_iota(jnp.int32, (A_BLOCK, A_BLOCK), 0)
    kj = lax.broadcasted_iota(jnp.int32, (A_BLOCK, A_BLOCK), 1)
    far = kj >= qi
    near = kj <= qi
    scale = 1.0 / math.sqrt(A_HEAD_DIM)
    nt = (((1,), (1,)), ((), ()))
    for g, (win, dil) in enumerate(A_GROUPS):
        q_ref, kc_ref, vc_ref, kp_ref, vp_ref = refs[5 * g:5 * g + 5]
        span = A_BLOCK * dil

        def body(it, carry, g=g, dil=dil, span=span, q_ref=q_ref, kc_ref=kc_ref, vc_ref=vc_ref,
                 kp_ref=kp_ref, vp_ref=vp_ref):
            n = it // dil
            r = it % dil
            start = n * span + r
            rows = pl.ds(start, A_BLOCK, stride=dil)
            before = pl.ds(jnp.maximum(start - span, 0), A_BLOCK, stride=dil)
            outside = pl.ds(r, A_BLOCK, stride=dil)
            first = n == 0
            q = q_ref[rows, :].astype(BF16)
            kc = kc_ref[rows, :].astype(BF16)
            vc = vc_ref[rows, :].astype(BF16)
            kp = jnp.where(first, kp_ref[outside, :], kc_ref[before, :]).astype(BF16)
            vp = jnp.where(first, vp_ref[outside, :], vc_ref[before, :]).astype(BF16)
            has_prev = jnp.logical_or(n > 0, blk > 0)
            sp = lax.dot_general(q, kp, nt, preferred_element_type=F32) * scale
            sc = lax.dot_general(q, kc, nt, preferred_element_type=F32) * scale
            sp = jnp.where(jnp.logical_and(far, has_prev), sp, -jnp.inf)
            sc = jnp.where(near, sc, -jnp.inf)
            m = jnp.maximum(jnp.max(sp, axis=-1, keepdims=True), jnp.max(sc, axis=-1, keepdims=True))
            ep = jnp.exp(sp - m)
            ec = jnp.exp(sc - m)
            den = jnp.sum(ep, axis=-1, keepdims=True) + jnp.sum(ec, axis=-1, keepdims=True)
            o = (jnp.dot(ep.astype(BF16), vp, preferred_element_type=F32)
                 + jnp.dot(ec.astype(BF16), vc, preferred_element_type=F32))
            og_ref[g, rows, :] = o / den
            lg_ref[g, rows, :] = jnp.broadcast_to(m + jnp.log(den), (A_BLOCK, A_HEAD_DIM))
            return carry

        lax.fori_loop(0, t_blk // A_BLOCK, body, 0)
    l0, l1, l2 = lg_ref[0], lg_ref[1], lg_ref[2]
    m = jnp.maximum(jnp.maximum(l0, l1), l2)
    e0, e1, e2 = jnp.exp(l0 - m), jnp.exp(l1 - m), jnp.exp(l2 - m)
    out = (e0 * og_ref[0] + e1 * og_ref[1] + e2 * og_ref[2]) / (e0 + e1 + e2)
    o_ref[...] = out.astype(BF16)


def attn_a_prompt(qkv, batch, seq):
    n_tot, width = qkv.shape
    hw = A_HEADS * A_HEAD_DIM
    t_blk = A_BLOCK * max(d for _, d in A_GROUPS)
    assert seq % t_blk == 0
    nblk = seq // t_blk
    in_specs, args = [], []
    for g, (win, dil) in enumerate(A_GROUPS):
        span = A_BLOCK * dil
        per_blk = t_blk // span
        for which in range(3):
            col = (g * 3 + which) * A_HEADS
            in_specs.append(pl.BlockSpec((t_blk, A_HEAD_DIM),
                                         lambda b, k, h, col=col: (b * nblk + k, col + h)))
            args.append(qkv)
        for which in (1, 2):
            col = (g * 3 + which) * A_HEADS
            in_specs.append(pl.BlockSpec(
                (span, A_HEAD_DIM),
                lambda b, k, h, col=col, per_blk=per_blk: (jnp.maximum((b * nblk + k) * per_blk - 1, 0), col + h)))
            args.append(qkv)
    scratch = pltpu.VMEM((len(A_GROUPS), t_blk, A_HEAD_DIM), F32)
    return pl.pallas_call(
        _attn_a_kernel, grid=(batch, nblk, A_HEADS), in_specs=in_specs,
        out_specs=pl.BlockSpec((t_blk, A_HEAD_DIM), lambda b, k, h: (b * nblk + k, h)),
        out_shape=jax.ShapeDtypeStruct((batch * seq, hw), BF16),
        scratch_shapes=[scratch, scratch],
        compiler_params=_params(("parallel", "parallel", "parallel")),
        name="attn_a_prompt")(*args)


def _attn_a_sample_kernel(qkv_ref, c0_ref, c1_ref, c2_ref, o_ref):
    t_new = qkv_ref.shape[1]
    hw = A_HEADS * A_HEAD_DIM
    scale = 1.0 / math.sqrt(A_HEAD_DIM)
    nt = (((1,), (1,)), ((), ()))
    caches = (c0_ref, c1_ref, c2_ref)
    ti = lax.broadcasted_iota(jnp.int32, (t_new, A_BLOCK), 0)
    ni = lax.broadcasted_iota(jnp.int32, (t_new, A_BLOCK), 1)
    tq = lax.broadcasted_iota(jnp.int32, (t_new, t_new), 0)
    tj = lax.broadcasted_iota(jnp.int32, (t_new, t_new), 1)
    for h in range(A_HEADS):
        outs, lses = [], []
        for g, (win, dil) in enumerate(A_GROUPS):
            base = g * 3 * hw + h * A_HEAD_DIM
            q = qkv_ref[0, :, base:base + A_HEAD_DIM].astype(BF16)
            kn = qkv_ref[0, :, base + hw:base + hw + A_HEAD_DIM].astype(BF16)
            vn = qkv_ref[0, :, base + 2 * hw:base + 2 * hw + A_HEAD_DIM].astype(BF16)
            n_sub = min(dil, t_new)
            s_new = lax.dot_general(q, kn, nt, preferred_element_type=F32) * scale
            ok_new = jnp.logical_and(tj <= tq, jnp.bitwise_and(tq - tj, dil - 1) == 0)
            s_new = jnp.where(ok_new, s_new, -jnp.inf)
            ss, vs = [], []
            for r in range(n_sub):
                kb = caches[g][0, :, r * 2 * hw + h * A_HEAD_DIM:r * 2 * hw + (h + 1) * A_HEAD_DIM]
                vb = caches[g][0, :, r * 2 * hw + hw + h * A_HEAD_DIM:r * 2 * hw + hw + (h + 1) * A_HEAD_DIM]
                s = lax.dot_general(q, kb.astype(BF16), nt, preferred_element_type=F32) * scale
                ok = jnp.logical_and(jnp.bitwise_and(ti, dil - 1) == r, ni * dil + r >= ti)
                ss.append(jnp.where(ok, s, -jnp.inf))
                vs.append(vb.astype(BF16))
            m = jnp.max(s_new, axis=-1, keepdims=True)
            for s in ss:
                m = jnp.maximum(m, jnp.max(s, axis=-1, keepdims=True))
            e_new = jnp.exp(s_new - m)
            den = jnp.sum(e_new, axis=-1, keepdims=True)
            o = jnp.dot(e_new.astype(BF16), vn, preferred_element_type=F32)
            for s, vb in zip(ss, vs):
                e = jnp.exp(s - m)
                den = den + jnp.sum(e, axis=-1, keepdims=True)
                o = o + jnp.dot(e.astype(BF16), vb, preferred_element_type=F32)
            outs.append(o / den)
            lses.append(m + jnp.log(den))
        lm = jnp.maximum(jnp.maximum(lses[0], lses[1]), lses[2])
        ws = [jnp.exp(l - lm) for l in lses]
        tot = ws[0] + ws[1] + ws[2]
        comb = (ws[0] * outs[0] + ws[1] * outs[1] + ws[2] * outs[2]) / tot
        o_ref[0, :, h * A_HEAD_DIM:(h + 1) * A_HEAD_DIM] = comb


def attn_a_sample(qkv_s, caches, layer):
    db, t_new, width = qkv_s.shape
    hw = A_HEADS * A_HEAD_DIM
    in_specs = [pl.BlockSpec((1, t_new, width), lambda b: (b, 0, 0))]
    args = [qkv_s]
    for (win, dil), c in zip(A_GROUPS, caches):
        nl = c.shape[0]
        view = c.reshape(nl * db, win // dil, dil * 2 * hw)
        n_sub = min(dil, t_new)
        in_specs.append(pl.BlockSpec((1, win // dil, n_sub * 2 * hw),
                                     lambda b, layer=layer: (layer * db + b, 0, 0)))
        args.append(view)
    return pl.pallas_call(
        _attn_a_sample_kernel, grid=(db,), in_specs=in_specs,
        out_specs=pl.BlockSpec((1, t_new, hw), lambda b: (b, 0, 0)),
        out_shape=jax.ShapeDtypeStruct((db, t_new, hw), F32),
        compiler_params=_params(("parallel",)), name="attn_a_sample")(*args)


def _lambda(lp_ref, lam_init):
    lp = lp_ref[...]
    a = jnp.sum(lp[0:1, :] * lp[1:2, :], axis=-1, keepdims=True)
    b = jnp.sum(lp[2:3, :] * lp[3:4, :], axis=-1, keepdims=True)
    return jnp.exp(a) - jnp.exp(b) + lam_init


def _sub_ln(o, g, lam_init):
    on = o * lax.rsqrt(jnp.mean(o * o, axis=-1, keepdims=True) + SUBLN_EPS) * g
    return on * (1.0 - lam_init)


def _attn_b_kernel(q_ref, k_ref, v_ref, lp_ref, g_ref, o_ref, m_ref, l_ref, acc_ref, *, lam_init, tq, tk):
    qi = pl.program_id(2)
    ki = pl.program_id(3)
    nk = pl.num_programs(3)
    scale = 1.0 / math.sqrt(B_QK_DIM)
    nt = (((1,), (1,)), ((), ()))

    @pl.when(ki == 0)
    def _():
        m_ref[...] = jnp.full(m_ref.shape, -jnp.inf, F32)
        l_ref[...] = jnp.zeros(l_ref.shape, F32)
        acc_ref[...] = jnp.zeros(acc_ref.shape, F32)

    @pl.when(ki * tk <= qi * tq + tq - 1)
    def _():
        rows = qi * tq + lax.broadcasted_iota(jnp.int32, (tq, tk), 0)
        cols = ki * tk + lax.broadcasted_iota(jnp.int32, (tq, tk), 1)
        mask = cols <= rows
        v = v_ref[...].astype(BF16)
        for c in range(2):
            sl = slice(c * B_QK_DIM, (c + 1) * B_QK_DIM)
            s = lax.dot_general(q_ref[:, sl].astype(BF16), k_ref[:, sl].astype(BF16), nt,
                                preferred_element_type=F32) * scale
            s = jnp.where(mask, s, -jnp.inf)
            m_old = m_ref[c]
            m_new = jnp.maximum(m_old, jnp.max(s, axis=-1, keepdims=True))
            alpha = jnp.exp(m_old - m_new)
            p = jnp.exp(s - m_new)
            l_ref[c] = alpha * l_ref[c] + jnp.sum(p, axis=-1, keepdims=True)
            acc_ref[c] = alpha * acc_ref[c] + jnp.dot(p.astype(BF16), v, preferred_element_type=F32)
            m_ref[c] = m_new

    @pl.when(ki == nk - 1)
    def _():
        lam = _lambda(lp_ref, lam_init)
        o = acc_ref[0] / l_ref[0] - lam * (acc_ref[1] / l_ref[1])
        o_ref[...] = _sub_ln(o, g_ref[...], lam_init).astype(BF16)


def attn_b_prompt(q, kv, lp, g_sub, lam_init, batch, seq, *, tq=512, tk=512):
    tq = min(tq, seq)
    tk = min(tk, seq)
    nq, nk = seq // tq, seq // tk
    hd = 2 * B_QK_DIM

    def kmap(b, h, i, j):
        return (b * nk + jnp.minimum(j, (i * tq + tq - 1) // tk), h)

    def vmap_(b, h, i, j):
        return (b * nk + jnp.minimum(j, (i * tq + tq - 1) // tk), B_HEADS + h)

    kern = functools.partial(_attn_b_kernel, lam_init=lam_init, tq=tq, tk=tk)
    return pl.pallas_call(
        kern, grid=(batch, B_HEADS, nq, nk),
        in_specs=[pl.BlockSpec((tq, hd), lambda b, h, i, j: (b * nq + i, h)),
                  pl.BlockSpec((tk, hd), kmap),
                  pl.BlockSpec((tk, B_V_DIM), vmap_),
                  pl.BlockSpec((4, B_QK_DIM), lambda b, h, i, j: (0, 0)),
                  pl.BlockSpec((1, B_V_DIM), lambda b, h, i, j: (0, 0))],
        out_specs=pl.BlockSpec((tq, B_V_DIM), lambda b, h, i, j: (b * nq + i, h)),
        out_shape=jax.ShapeDtypeStruct((batch * seq, B_HEADS * B_V_DIM), BF16),
        scratch_shapes=[pltpu.VMEM((2, tq, 1), F32), pltpu.VMEM((2, tq, 1), F32),
                        pltpu.VMEM((2, tq, B_V_DIM), F32)],
        compiler_params=_params(("parallel", "parallel", "parallel", "arbitrary")),
        name="attn_b_prompt")(q, kv, kv, lp, g_sub.reshape(1, B_V_DIM))


def _attn_b_sample_kernel(*refs, lam_init, t_new, n_par):
    pt_ref, q_ref = refs[0], refs[1]
    page_refs = refs[2:2 + 2 * n_par]
    kvn_ref, lp_ref, g_ref, o_ref, m_ref, l_ref, acc_ref, xs_ref = refs[2 + 2 * n_par:]
    p = pl.program_id(1)
    n_steps = pl.num_programs(1)
    scale = 1.0 / math.sqrt(B_QK_DIM)
    kw = B_HEADS * 2 * B_QK_DIM
    nt = (((1,), (1,)), ((), ()))
    rph = 2 * t_new

    @pl.when(p == 0)
    def _():
        m_ref[...] = jnp.full(m_ref.shape, -jnp.inf, F32)
        l_ref[...] = jnp.zeros(l_ref.shape, F32)
        acc_ref[...] = jnp.zeros(acc_ref.shape, F32)

    def update(state, s, weigh):
        m_old, l_old, acc_old = state
        m_new = jnp.maximum(m_old, jnp.max(s, axis=-1, keepdims=True))
        alpha = jnp.exp(m_old - m_new)
        e = jnp.exp(s - m_new)
        return m_new, alpha * l_old + jnp.sum(e, axis=-1, keepdims=True), alpha * acc_old + weigh(e)

    rows_per_page = 2 * PAGE_SIZE
    for h in range(B_HEADS):
        for half in range(2):
            for k in range(n_par):
                xs_ref[h, half, k * rows_per_page:(k + 1) * rows_per_page, :] = (
                    page_refs[2 * k + half][0, pl.ds(h, rows_per_page, stride=B_HEADS), :].astype(BF16))

    parts = []
    for h in range(B_HEADS):
        qh = q_ref[0, h * rph:(h + 1) * rph, :]
        parts.append(lax.dot_general(qh[:, 0:B_QK_DIM], xs_ref[h, 0], nt, preferred_element_type=F32)
                     + lax.dot_general(qh[:, B_QK_DIM:], xs_ref[h, 1], nt, preferred_element_type=F32))
    s = jnp.concatenate(parts, axis=0) * scale
    is_key = jnp.bitwise_and(lax.broadcasted_iota(jnp.int32, s.shape, 1), 1) == 0
    s = jnp.where(is_key, s, -jnp.inf)

    def weigh(e):
        ev = pltpu.roll(e, 1, 1)
        outs = []
        for h in range(B_HEADS):
            evh = ev[h * rph:(h + 1) * rph, :].astype(BF16)
            outs.append(jnp.concatenate([jnp.dot(evh, xs_ref[h, 0], preferred_element_type=F32),
                                         jnp.dot(evh, xs_ref[h, 1], preferred_element_type=F32)], axis=1))
        return jnp.concatenate(outs, axis=0)

    m_new, l_new, acc_new = update((m_ref[...], l_ref[...], acc_ref[...]), s, weigh)
    m_ref[...] = m_new
    l_ref[...] = l_new
    acc_ref[...] = acc_new

    @pl.when(p == n_steps - 1)
    def _():
        lam = _lambda(lp_ref, lam_init)
        for h in range(B_HEADS):
            rs = slice(h * rph, (h + 1) * rph)
            qh = q_ref[0, rs, :]
            kn = kvn_ref[0, :, h * B_V_DIM:(h + 1) * B_V_DIM].astype(BF16)
            vn = kvn_ref[0, :, kw + h * B_V_DIM:kw + (h + 1) * B_V_DIM].astype(BF16)
            sn = lax.dot_general(qh, kn, nt, preferred_element_type=F32) * scale
            rq = jnp.bitwise_and(lax.broadcasted_iota(jnp.int32, sn.shape, 0), t_new - 1)
            cj = lax.broadcasted_iota(jnp.int32, sn.shape, 1)
            _, l_fin, acc_fin = update(
                (m_ref[rs, :], l_ref[rs, :], acc_ref[rs, :]), jnp.where(cj <= rq, sn, -jnp.inf),
                lambda e, vn=vn: jnp.dot(e.astype(BF16), vn, preferred_element_type=F32))
            on = acc_fin / l_fin
            o = on[0:t_new, :] - lam * on[t_new:2 * t_new, :]
            o_ref[0, :, h * B_V_DIM:(h + 1) * B_V_DIM] = _sub_ln(o, g_ref[...], lam_init)


def attn_b_sample(q_s, kv_s, cache_b_kv, page_table, lp, g_sub, lam_init):
    db, t_new, qw = q_s.shape
    n_pages = page_table.shape[1]
    n_phys = cache_b_kv.shape[0]
    assert t_new & (t_new - 1) == 0
    n_par = 4 if n_pages % 4 == 0 else 1
    kvw = 2 * B_HEADS * B_V_DIM
    page_rows = PAGE_SIZE * 2 * B_HEADS
    pages = cache_b_kv.reshape(n_phys, page_rows, B_V_DIM)
    q5 = q_s.reshape(db, t_new, B_HEADS, 2, B_QK_DIM).transpose(0, 2, 3, 1, 4)
    eye = jnp.eye(2, dtype=F32)
    qbd = q5[:, :, :, :, None, :] * eye[None, None, :, None, :, None]
    rows = B_HEADS * 2 * t_new
    qbd = qbd.reshape(db, rows, 2 * B_QK_DIM).astype(BF16)
    kern = functools.partial(_attn_b_sample_kernel, lam_init=lam_init, t_new=t_new, n_par=n_par)
    page_specs = [pl.BlockSpec((1, page_rows, B_QK_DIM),
                               lambda b, p, pt, k=k, half=half: (pt[b, p * n_par + k], 0, half))
                  for k in range(n_par) for half in range(2)]
    grid_spec = pltpu.PrefetchScalarGridSpec(
        num_scalar_prefetch=1, grid=(db, n_pages // n_par),
        in_specs=[pl.BlockSpec((1, rows, 2 * B_QK_DIM), lambda b, p, pt: (b, 0, 0))] + page_specs + [
                  pl.BlockSpec((1, t_new, kvw), lambda b, p, pt: (b, 0, 0)),
                  pl.BlockSpec((4, B_QK_DIM), lambda b, p, pt: (0, 0)),
                  pl.BlockSpec((1, B_V_DIM), lambda b, p, pt: (0, 0))],
        out_specs=pl.BlockSpec((1, t_new, qw), lambda b, p, pt: (b, 0, 0)),
        scratch_shapes=[pltpu.VMEM((rows, 1), F32), pltpu.VMEM((rows, 1), F32),
                        pltpu.VMEM((rows, B_V_DIM), F32),
                        pltpu.VMEM((B_HEADS, 2, n_par * 2 * PAGE_SIZE, B_QK_DIM), BF16)])
    return pl.pallas_call(
        kern, grid_spec=grid_spec,
        out_shape=jax.ShapeDtypeStruct((db, t_new, qw), F32),
        compiler_params=_params(("parallel", "arbitrary")),
        name="attn_b_sample")(page_table, qbd, *([pages] * (2 * n_par)), kv_s, lp, g_sub.reshape(1, B_V_DIM))


def _top_ranks(s):
    iota = lax.broadcasted_iota(jnp.int32, s.shape, 0).astype(F32)
    rank = jnp.full(s.shape, NOT_SELECTED, F32)
    vals = []
    work = s
    for a in range(PEER_TOPK):
        m = jnp.max(work, axis=0, keepdims=True)
        idx = jnp.min(jnp.where(work == m, iota, float(N_KEYS)), axis=0, keepdims=True)
        hit = iota == idx
        rank = jnp.where(hit, float(a), rank)
        work = jnp.where(hit, -jnp.inf, work)
        vals.append(m)
    return vals, rank


def _candidate_rows():
    pieces = [(0, 0, 16, 16)]
    for a in range(1, 8):
        pieces.append((a, 0, 8, PEER_TOPK // (a + 1)))
    return pieces


def _route_kernel(ht_ref, wq_ref, sk_ref, r1_ref, e1_ref, bq_out_ref, c0_out_ref, qt_ref, bq_ref, c0_ref):
    n_chunks = ht_ref.shape[1] // LANES
    qt_ref[...] = jnp.dot(wq_ref[...], ht_ref[...], preferred_element_type=F32)
    neg = -jnp.inf
    t = LANES

    def body(it, carry):
        h = it // n_chunks
        cs = pl.ds(pl.multiple_of((it % n_chunks) * LANES, LANES), LANES)
        vals, ranks, scores = [], [], []
        for c in range(2):
            hc = h * 2 + c
            qhc = qt_ref[pl.ds(pl.multiple_of(hc * N_KEYS, N_KEYS), N_KEYS), cs].astype(BF16)
            s = jnp.dot(sk_ref[hc].astype(BF16), qhc, preferred_element_type=F32)
            v, r = _top_ranks(s)
            vals.append(v); ranks.append(r); scores.append(s)
        v0, v1 = vals
        v1_16 = jnp.concatenate(v1, axis=0)
        v0_hi = jnp.concatenate(v0[8:16], axis=0)
        cands, flats = [], []
        for a, _, rows, nvalid in _candidate_rows():
            b_iota = lax.broadcasted_iota(jnp.int32, (rows, t), 0)
            cs_ab = v0[a] + v1_16[0:rows, :]
            cands.append(jnp.where(b_iota < nvalid, cs_ab, neg))
            flats.append((b_iota + a * PEER_TOPK).astype(F32))
        cands.append(v0_hi + v1[0])
        flats.append(((lax.broadcasted_iota(jnp.int32, (8, t), 0) + 8) * PEER_TOPK).astype(F32))
        cand = jnp.concatenate(cands, axis=0)
        flat = jnp.concatenate(flats, axis=0)
        big = float(PEER_TOPK * PEER_TOPK)
        work = cand
        sel = jnp.zeros(cand.shape, F32)
        for _ in range(PEER_TOPK):
            m = jnp.max(work, axis=0, keepdims=True)
            idx = jnp.min(jnp.where(work == m, flat, big), axis=0, keepdims=True)
            hit = flat == idx
            sel = jnp.where(hit, 1.0, sel)
            work = jnp.where(hit, neg, work)
        top = v0[0] + v1[0]
        z = jnp.sum(jnp.where(sel > 0.0, jnp.exp(cand - top), 0.0), axis=0, keepdims=True)
        counts = [jnp.sum(sel[0:16, :], axis=0, keepdims=True)]
        for k in range(1, 8):
            counts.append(jnp.sum(sel[8 + 8 * k:16 + 8 * k, :], axis=0, keepdims=True))
        hi = sel[72:80, :]
        bq = jnp.zeros(ranks[0].shape, F32)
        for a in range(PEER_TOPK):
            cnt = counts[a] if a < 8 else hi[a - 8:a - 7, :]
            bq = jnp.where(ranks[0] == float(a), cnt, bq)
        rs = pl.ds(pl.multiple_of(h * N_KEYS, N_KEYS), N_KEYS)
        bq_ref[rs, cs] = bq
        c0_ref[rs, cs] = jnp.exp(scores[0] - v0[0]) / z
        r1_ref[rs, cs] = ranks[1].astype(BF16)
        e1_ref[rs, cs] = jnp.exp(scores[1] - v1[0]).astype(BF16)
        return carry

    lax.fori_loop(0, PEER_HEADS * n_chunks, body, 0)
    for h in range(PEER_HEADS):
        bq_out_ref[:, h, :] = bq_ref[h * N_KEYS:(h + 1) * N_KEYS, :]
        c0_out_ref[:, h, :] = c0_ref[h * N_KEYS:(h + 1) * N_KEYS, :]


def peer_route(ht, wq_t, subkeys, *, tm=256):
    d, n = ht.shape
    rows = PEER_HEADS * N_KEYS
    slab = pl.BlockSpec((rows, tm), lambda i: (0, i))
    sds = jax.ShapeDtypeStruct((rows, n), BF16)
    slab3 = pl.BlockSpec((N_KEYS, PEER_HEADS, tm), lambda i: (0, 0, i))
    sds3 = jax.ShapeDtypeStruct((N_KEYS, PEER_HEADS, n), F32)
    return pl.pallas_call(
        _route_kernel, grid=(n // tm,),
        in_specs=[pl.BlockSpec((d, tm), lambda i: (0, i)),
                  pl.BlockSpec(wq_t.shape, lambda i: (0, 0)),
                  pl.BlockSpec(subkeys.shape, lambda i: (0, 0, 0))],
        out_specs=[slab, slab, slab3, slab3], out_shape=[sds, sds, sds3, sds3],
        scratch_shapes=[pltpu.VMEM((wq_t.shape[0], tm), F32), pltpu.VMEM((rows, tm), F32),
                        pltpu.VMEM((rows, tm), F32)],
        compiler_params=_params(("parallel",)), name="peer_route")(ht, wq_t, subkeys)


def _gelu(a):
    return 0.5 * a * (1.0 + lax.erf(a * math.sqrt(0.5)))


def _expert_kernel(x_ref, ht_ref, u_ref, v_ref, r1_ref, e1_ref, bq_ref, c0_ref, o_ref, a_ref, *, te, tm, ge):
    e = pl.program_id(1)
    rows_per_group = ge // N_KEYS
    n_chunks = tm // LANES
    packed = 2 * SUBLANES
    tiles = N_KEYS // packed

    @pl.when(e == 0)
    def _():
        o_ref[...] = x_ref[...]

    ht = ht_ref[...]
    n_groups = te // ge

    def scores(k):
        a_ref[k] = jnp.dot(u_ref[k * ge:(k + 1) * ge, :], ht, preferred_element_type=F32)

    scores(0)
    for k in range(n_groups):
        if k + 1 < n_groups:
            scores(k + 1)
        rows = []
        for r in range(rows_per_group):
            i = e * (te // N_KEYS) + k * rows_per_group + r
            cols = []
            for c in range(n_chunks):
                cs = slice(c * LANES, (c + 1) * LANES)
                bq_all = bq_ref[i, :, cs]
                c0_all = c0_ref[i, :, cs]
                w = [jnp.zeros((packed, LANES), BF16)] * tiles
                for h in range(PEER_HEADS):
                    bq = jnp.broadcast_to(bq_all[h:h + 1, :], (packed, LANES)).astype(BF16)
                    c0 = jnp.broadcast_to(c0_all[h:h + 1, :], (packed, LANES)).astype(BF16)
                    for t in range(tiles):
                        js = slice(h * N_KEYS + t * packed, h * N_KEYS + (t + 1) * packed)
                        e1 = e1_ref[js, cs]
                        w[t] = w[t] + jnp.where(r1_ref[js, cs] < bq, e1 * c0, jnp.zeros_like(e1))
                gate = jnp.concatenate(w, axis=0).astype(F32)
                cols.append(_gelu(a_ref[k, r * N_KEYS:(r + 1) * N_KEYS, cs]) * gate)
            rows.append(jnp.concatenate(cols, axis=1))
        gt = jnp.concatenate(rows, axis=0)
        g = gt.T.astype(BF16)
        o_ref[...] += jnp.dot(g, v_ref[k * ge:(k + 1) * ge, :], preferred_element_type=F32)


def peer_experts(x, ht, u_bf, v_bf, slabs, *, tm=512, te=1024, ge=256):
    d, n = ht.shape
    n_exp = u_bf.shape[0]
    while n % tm:
        tm //= 2
    te = min(te, n_exp)
    rows = PEER_HEADS * N_KEYS
    once = pl.Buffered(1)
    slab = pl.BlockSpec((rows, tm), lambda i, e: (0, i), pipeline_mode=once)
    slab3 = pl.BlockSpec((N_KEYS, PEER_HEADS, tm), lambda i, e: (0, 0, i), pipeline_mode=once)
    kern = functools.partial(_expert_kernel, te=te, tm=tm, ge=ge)
    return pl.pallas_call(
        kern, grid=(n // tm, n_exp // te),
        in_specs=[pl.BlockSpec((tm, d), lambda i, e: (i, 0), pipeline_mode=once),
                  pl.BlockSpec((d, tm), lambda i, e: (0, i), pipeline_mode=once),
                  pl.BlockSpec((te, d), lambda i, e: (e, 0)),
                  pl.BlockSpec((te, d), lambda i, e: (e, 0)),
                  slab, slab, slab3, slab3],
        out_specs=pl.BlockSpec((tm, d), lambda i, e: (i, 0)),
        out_shape=jax.ShapeDtypeStruct((n, d), F32),
        scratch_shapes=[pltpu.VMEM((te // ge, ge, tm), F32)],
        compiler_params=_params(("parallel", "arbitrary")),
        name="peer_experts")(x, ht, u_bf, v_bf, *slabs)


def _rope_table(pos):
    rot = A_HEAD_DIM // 4
    half = rot // 2
    inv_freq = ROPE_THETA ** (-jnp.arange(half, dtype=F32) / half)
    ang = pos.astype(F32)[:, None] * inv_freq[None, :]
    cos, sin = jnp.cos(ang), jnp.sin(ang)
    n = pos.shape[0]
    ones = jnp.ones((n, A_HEAD_DIM - rot), F32)
    zeros = jnp.zeros((n, A_HEAD_DIM - rot), F32)
    zh = jnp.zeros((n, half), F32)
    return jnp.concatenate([cos, cos, ones, -sin, zh, zeros, zh, sin, zeros], axis=1)


def kernel(x_prompt, x_sample, cache_a_w128, cache_a_w512, cache_a_w2048, cache_b_kv, page_table,
           p_prompt, p_sample, norm_mix, norm_ffn, norm_ple, norm_kv, norm_final,
           w_qkv_a, w_o_a, w_kv_b, w_q_b, diff_lambda, norm_sub_b, w_o_b,
           peer_wq, peer_subkeys, peer_u, peer_v, w_ple, w_ple_gate):
    batch, seq, d = x_prompt.shape
    db, t_new, _ = x_sample.shape
    depth = norm_mix.shape[0]
    n_a = w_qkv_a.shape[0]
    past_len = page_table.shape[1] * PAGE_SIZE
    n_p, n_s = batch * seq, db * t_new
    n_tot = -(-(n_p + n_s) // TOKEN_PAD) * TOKEN_PAD
    pad = n_tot - n_p - n_s
    a_caches = (cache_a_w128, cache_a_w512, cache_a_w2048)
    hw = A_HEADS * A_HEAD_DIM

    def tokens(prompt_part, sample_part):
        w = prompt_part.shape[-1]
        return jnp.concatenate([prompt_part.reshape(n_p, w), sample_part.reshape(n_s, w),
                                jnp.zeros((pad, w), prompt_part.dtype)], axis=0)

    x = tokens(x_prompt, x_sample)
    pos = jnp.concatenate([jnp.tile(jnp.arange(seq, dtype=jnp.int32), batch),
                           jnp.tile(past_len + jnp.arange(t_new, dtype=jnp.int32), db),
                           jnp.zeros((pad,), jnp.int32)])
    tab = _rope_table(pos)
    u_bf = peer_u.astype(BF16)
    v_bf = peer_v.astype(BF16)

    a_rows_p = [[] for _ in A_GROUPS]
    a_rows_s = [[] for _ in A_GROUPS]
    new_b_kv_prompt = new_b_kv_sample = kv = None
    for i in range(depth):
        h = rms_norm(x, norm_mix[i], out_h=True)[0]
        if i < n_a:
            qkv = matmul(h, w_qkv_a[i], mode="rope", tab=tab,
                         rope_fn=lambda col: (col // hw) % 3 != 2)
            o_p = attn_a_prompt(qkv, batch, seq)
            qkv_s = qkv[n_p:n_p + n_s].reshape(db, t_new, qkv.shape[1])
            o_s = attn_a_sample(qkv_s, a_caches, i).reshape(n_s, hw).astype(BF16)
            o_all = jnp.concatenate([o_p, o_s, jnp.zeros((pad, hw), BF16)], axis=0)
            x = matmul(o_all, w_o_a[i], mode="res", res=x)
            q5p = qkv[:n_p].reshape(batch, seq, len(A_GROUPS), 3, A_HEADS, A_HEAD_DIM)
            q5s = qkv_s.reshape(db, t_new, len(A_GROUPS), 3, A_HEADS, A_HEAD_DIM)
            for g, (win, dil) in enumerate(A_GROUPS):
                wb = min(win, seq)
                a_rows_p[g].append(q5p[:, seq - wb:, g, 1:3])
                a_rows_s[g].append(q5s[:, :, g, 1:3])
        else:
            j = i - n_a
            if j == 0:
                hkv = rms_norm(x, norm_kv, out_h=True)[0]
                kw = B_HEADS * 2 * B_QK_DIM
                kv = matmul(hkv, w_kv_b, mode="rope", tab=tab, rope_fn=lambda col: col < kw)
                new_b_kv_prompt = kv[:n_p].reshape(batch, seq, 2, B_HEADS, B_V_DIM)
                new_b_kv_sample = kv[n_p:n_p + n_s].reshape(db, t_new, 2, B_HEADS, B_V_DIM)
            lam_init = 0.8 - 0.6 * math.exp(-0.3 * i)
            q = matmul(h, w_q_b[j], mode="rope", tab=tab, rope_fn=lambda col: col >= 0)
            o_p = attn_b_prompt(q, kv, diff_lambda[j], norm_sub_b[j], lam_init, batch, seq)
            q_s = q[n_p:n_p + n_s].reshape(db, t_new, q.shape[1])
            kv_s = kv[n_p:n_p + n_s].reshape(db, t_new, kv.shape[1])
            o_s = attn_b_sample(q_s, kv_s, cache_b_kv, page_table, diff_lambda[j], norm_sub_b[j], lam_init)
            o_all = jnp.concatenate([o_p, o_s.reshape(n_s, -1).astype(BF16),
                                     jnp.zeros((pad, o_p.shape[1]), BF16)], axis=0)
            x = matmul(o_all, w_o_b[j], mode="res", res=x)
        ht = rms_norm(x, norm_ffn[i], out_ht=True)[0]
        wq_t = peer_wq[i].T.astype(BF16)
        sk = peer_subkeys[i].reshape(PEER_HEADS * 2, N_KEYS, -1)
        slabs = peer_route(ht, wq_t, sk)
        x = peer_experts(x, ht, u_bf[i], v_bf[i], slabs)
        hn = rms_norm(x, norm_ple[i], out_h=True)[0]
        x = ple(x, tokens(p_prompt[i], p_sample[i]), hn, w_ple[i], w_ple_gate[i])

    y = rms_norm(x, norm_final, out_y=True)[0]
    y_prompt = y[:n_p].reshape(batch, seq, d)
    y_sample = y[n_p:n_p + n_s].reshape(db, t_new, d)
    outs_p = [jnp.stack(r, axis=0) for r in a_rows_p]
    outs_s = [jnp.stack(r, axis=0) for r in a_rows_s]
    return (y_prompt, y_sample, *outs_p, *outs_s, new_b_kv_prompt, new_b_kv_sample)
```

```python
import functools
import math

import jax
import jax.numpy as jnp
from jax import lax
from jax.experimental import pallas as pl
from jax.experimental.pallas import tpu as pltpu

BF16 = jnp.bfloat16
F32 = jnp.float32

ROPE_THETA = 500000.0
NORM_EPS = 1e-6
SUBLN_EPS = 1e-5
A_GROUPS = ((128, 1), (512, 4), (2048, 16))
A_HEADS = 8
A_HEAD_DIM = 128
A_BLOCK = 128
B_HEADS = 8
B_QK_DIM = 128
B_V_DIM = 256
PEER_HEADS = 8
N_KEYS = 128
PEER_TOPK = 16
PAGE_SIZE = 128

LANES = 128
SUBLANES = 8
VMEM_LIMIT = 56 * 1024 * 1024
TOKEN_PAD = 512
NOT_SELECTED = 99.0


def _params(sem):
    return pltpu.CompilerParams(dimension_semantics=sem, vmem_limit_bytes=VMEM_LIMIT)


def _rms_kernel(*refs, eps, has_add, out_x, out_h, out_ht, out_y):
    it = iter(refs)
    x_ref = next(it)
    add_ref = next(it) if has_add else None
    g_ref = next(it)
    x = x_ref[...]
    if has_add:
        x = x + add_ref[...].T
    if out_x:
        next(it)[...] = x
    y = x * lax.rsqrt(jnp.mean(x * x, axis=-1, keepdims=True) + eps) * g_ref[...]
    if out_h:
        next(it)[...] = y.astype(BF16)
    if out_ht:
        next(it)[...] = y.T.astype(BF16)
    if out_y:
        next(it)[...] = y


def rms_norm(x, g, *, add_t=None, out_x=False, out_h=False, out_ht=False, out_y=False,
             eps=NORM_EPS, tm=256):
    n, d = x.shape
    grid = (n // tm,)
    row = pl.BlockSpec((tm, d), lambda i: (i, 0))
    col = pl.BlockSpec((d, tm), lambda i: (0, i))
    in_specs = [row]
    args = [x]
    if add_t is not None:
        in_specs.append(col)
        args.append(add_t)
    in_specs.append(pl.BlockSpec((1, d), lambda i: (0, 0)))
    args.append(g.reshape(1, d))
    out_shape, out_specs = [], []
    if out_x:
        out_shape.append(jax.ShapeDtypeStruct((n, d), F32)); out_specs.append(row)
    if out_h:
        out_shape.append(jax.ShapeDtypeStruct((n, d), BF16)); out_specs.append(row)
    if out_ht:
        out_shape.append(jax.ShapeDtypeStruct((d, n), BF16)); out_specs.append(col)
    if out_y:
        out_shape.append(jax.ShapeDtypeStruct((n, d), F32)); out_specs.append(row)
    kern = functools.partial(_rms_kernel, eps=eps, has_add=add_t is not None, out_x=out_x,
                             out_h=out_h, out_ht=out_ht, out_y=out_y)
    return pl.pallas_call(kern, grid=grid, in_specs=in_specs, out_specs=out_specs,
                          out_shape=out_shape, compiler_params=_params(("parallel",)),
                          name="rms_norm")(*args)


def _rope_tile(y, tab):
    c = tab[:, 0:LANES]
    s1 = tab[:, LANES:2 * LANES]
    s2 = tab[:, 2 * LANES:3 * LANES]
    outs = []
    for g in range(y.shape[1] // LANES):
        yg = y[:, g * LANES:(g + 1) * LANES]
        outs.append(yg * c + pltpu.roll(yg, LANES - 16, 1) * s1 + pltpu.roll(yg, 16, 1) * s2)
    return jnp.concatenate(outs, axis=1) if len(outs) > 1 else outs[0]


def _mm_kernel(*refs, mode, rope_fn, tn, out_scale):
    if mode == "rope":
        x_ref, w_ref, tab_ref, o_ref, wb_ref = refs
    elif mode == "res":
        x_ref, w_ref, r_ref, o_ref, wb_ref = refs
    else:
        x_ref, w_ref, o_ref, wb_ref = refs
    j = pl.program_id(0)

    @pl.when(pl.program_id(1) == 0)
    def _():
        wb_ref[...] = w_ref[...].astype(BF16)

    y = jnp.dot(x_ref[...], wb_ref[...], preferred_element_type=F32)
    if mode == "rope":
        roped = rope_fn(j * tn)

        def finish(v):
            return (v if out_scale == 1.0 else v * out_scale).astype(o_ref.dtype)

        @pl.when(roped)
        def _():
            o_ref[...] = finish(_rope_tile(y, tab_ref[...]))

        @pl.when(jnp.logical_not(roped))
        def _():
            o_ref[...] = finish(y)
    elif mode == "res":
        o_ref[...] = r_ref[...] + y
    else:
        o_ref[...] = y


def matmul(x, w, *, mode="plain", tab=None, res=None, rope_fn=None, tm=512, tn=512,
           out_scale=1.0, out_dtype=F32):
    n, k = x.shape
    m = w.shape[1]
    tm = min(tm, n)
    tn = min(tn, m)
    while n % tm:
        tm //= 2
    grid = (m // tn, n // tm)
    in_specs = [pl.BlockSpec((tm, k), lambda j, i: (i, 0)),
                pl.BlockSpec((k, tn), lambda j, i: (0, j))]
    args = [x, w]
    if mode == "rope":
        in_specs.append(pl.BlockSpec((tm, 3 * LANES), lambda j, i: (i, 0)))
        args.append(tab)
    elif mode == "res":
        in_specs.append(pl.BlockSpec((tm, tn), lambda j, i: (i, j)))
        args.append(res)
    assert mode == "rope" or (out_scale == 1.0 and out_dtype == F32)
    kern = functools.partial(_mm_kernel, mode=mode, rope_fn=rope_fn, tn=tn, out_scale=out_scale)
    return pl.pallas_call(
        kern, grid=grid, in_specs=in_specs,
        out_specs=pl.BlockSpec((tm, tn), lambda j, i: (i, j)),
        out_shape=jax.ShapeDtypeStruct((n, m), out_dtype),
        scratch_shapes=[pltpu.VMEM((k, tn), BF16)],
        compiler_params=_params(("arbitrary", "arbitrary")),
        name="matmul_" + mode)(*args)


def _ple_kernel(x_ref, p_ref, hn_ref, wp_ref, wg_ref, o_ref, wgb_ref):
    @pl.when(pl.program_id(1) == 0)
    def _():
        wgb_ref[...] = wg_ref[...].astype(BF16)

    gate = jax.nn.sigmoid(jnp.dot(hn_ref[...], wgb_ref[...], preferred_element_type=F32))
    up = jnp.dot(p_ref[...].astype(BF16), wp_ref[...].astype(BF16), preferred_element_type=F32)
    o_ref[...] = x_ref[...] + up * gate


def ple(x, p, hn, w_p, w_gate, *, tm=512, tn=512):
    n, d = x.shape
    kp = p.shape[1]
    tn = min(tn, d)
    while n % tm:
        tm //= 2
    grid = (d // tn, n // tm)
    return pl.pallas_call(
        _ple_kernel, grid=grid,
        in_specs=[pl.BlockSpec((tm, tn), lambda j, i: (i, j)),
                  pl.BlockSpec((tm, kp), lambda j, i: (i, 0)),
                  pl.BlockSpec((tm, d), lambda j, i: (i, 0)),
                  pl.BlockSpec((kp, tn), lambda j, i: (0, j)),
                  pl.BlockSpec((d, tn), lambda j, i: (0, j))],
        out_specs=pl.BlockSpec((tm, tn), lambda j, i: (i, j)),
        out_shape=jax.ShapeDtypeStruct((n, d), F32),
        scratch_shapes=[pltpu.VMEM((d, tn), BF16)],
        compiler_params=_params(("arbitrary", "arbitrary")),
        name="ple")(x, p, hn, w_p, w_gate)


def _attn_a_kernel(*refs):
    n_in = 5 * len(A_GROUPS)
    o_ref, og_ref, lg_ref = refs[n_in:n_in + 3]
    blk = pl.program_id(1)
    t_blk = o_ref.shape[0]
    qi = lax.broadcasted_iota(jnp.int32, (A_BLOCK, A_BLOCK), 0)
    kj = lax.broadcasted_iota(jnp.int32, (A_BLOCK, A_BLOCK), 1)
    far = kj >= qi
    near = kj <= qi
    scale = 1.0 / math.sqrt(A_HEAD_DIM)
    nt = (((1,), (1,)), ((), ()))
    for g, (win, dil) in enumerate(A_GROUPS):
        q_ref, kc_ref, vc_ref, kp_ref, vp_ref = refs[5 * g:5 * g + 5]
        span = A_BLOCK * dil

        def body(it, carry, g=g, dil=dil, span=span, q_ref=q_ref, kc_ref=kc_ref, vc_ref=vc_ref,
                 kp_ref=kp_ref, vp_ref=vp_ref):
            n = it // dil
            r = it % dil
            start = n * span + r
            rows = pl.ds(start, A_BLOCK, stride=dil)
            before = pl.ds(jnp.maximum(start - span, 0), A_BLOCK, stride=dil)
            outside = pl.ds(r, A_BLOCK, stride=dil)
            first = n == 0
            q = q_ref[rows, :].astype(BF16)
            kc = kc_ref[rows, :].astype(BF16)
            vc = vc_ref[rows, :].astype(BF16)
            kp = jnp.where(first, kp_ref[outside, :], kc_ref[before, :]).astype(BF16)
            vp = jnp.where(first, vp_ref[outside, :], vc_ref[before, :]).astype(BF16)
            has_prev = jnp.logical_or(n > 0, blk > 0)
            sp = lax.dot_general(q, kp, nt, preferred_element_type=F32) * scale
            sc = lax.dot_general(q, kc, nt, preferred_element_type=F32) * scale
            sp = jnp.where(jnp.logical_and(far, has_prev), sp, -jnp.inf)
            sc = jnp.where(near, sc, -jnp.inf)
            m = jnp.maximum(jnp.max(sp, axis=-1, keepdims=True), jnp.max(sc, axis=-1, keepdims=True))
            ep = jnp.exp(sp - m)
            ec = jnp.exp(sc - m)
            den = jnp.sum(ep, axis=-1, keepdims=True) + jnp.sum(ec, axis=-1, keepdims=True)
            o = (jnp.dot(ep.astype(BF16), vp, preferred_element_type=F32)
                 + jnp.dot(ec.astype(BF16), vc, preferred_element_type=F32))
            og_ref[g, rows, :] = o / den
            lg_ref[g, rows, :] = jnp.broadcast_to(m + jnp.log(den), (A_BLOCK, A_HEAD_DIM))
            return carry

        lax.fori_loop(0, t_blk // A_BLOCK, body, 0, unroll=4)
    l0, l1, l2 = lg_ref[0], lg_ref[1], lg_ref[2]
    m = jnp.maximum(jnp.maximum(l0, l1), l2)
    e0, e1, e2 = jnp.exp(l0 - m), jnp.exp(l1 - m), jnp.exp(l2 - m)
    out = (e0 * og_ref[0] + e1 * og_ref[1] + e2 * og_ref[2]) / (e0 + e1 + e2)
    o_ref[...] = out.astype(BF16)


def attn_a_prompt(qkv, batch, seq):
    n_tot, width = qkv.shape
    hw = A_HEADS * A_HEAD_DIM
    t_blk = A_BLOCK * max(d for _, d in A_GROUPS)
    assert seq % t_blk == 0
    nblk = seq // t_blk
    in_specs, args = [], []
    for g, (win, dil) in enumerate(A_GROUPS):
        span = A_BLOCK * dil
        per_blk = t_blk // span
        for which in range(3):
            col = (g * 3 + which) * A_HEADS
            in_specs.append(pl.BlockSpec((t_blk, A_HEAD_DIM),
                                         lambda b, k, h, col=col: (b * nblk + k, col + h)))
            args.append(qkv)
        for which in (1, 2):
            col = (g * 3 + which) * A_HEADS
            in_specs.append(pl.BlockSpec(
                (span, A_HEAD_DIM),
                lambda b, k, h, col=col, per_blk=per_blk: (jnp.maximum((b * nblk + k) * per_blk - 1, 0), col + h)))
            args.append(qkv)
    scratch = pltpu.VMEM((len(A_GROUPS), t_blk, A_HEAD_DIM), F32)
    return pl.pallas_call(
        _attn_a_kernel, grid=(batch, nblk, A_HEADS), in_specs=in_specs,
        out_specs=pl.BlockSpec((t_blk, A_HEAD_DIM), lambda b, k, h: (b * nblk + k, h)),
        out_shape=jax.ShapeDtypeStruct((batch * seq, hw), BF16),
        scratch_shapes=[scratch, scratch],
        compiler_params=_params(("parallel", "parallel", "parallel")),
        name="attn_a_prompt")(*args)


def _attn_a_sample_kernel(qkv_ref, c0_ref, c1_ref, c2_ref, o_ref):
    t_new = qkv_ref.shape[1]
    hw = A_HEADS * A_HEAD_DIM
    scale = 1.0 / math.sqrt(A_HEAD_DIM)
    nt = (((1,), (1,)), ((), ()))
    caches = (c0_ref, c1_ref, c2_ref)
    ti = lax.broadcasted_iota(jnp.int32, (t_new, A_BLOCK), 0)
    ni = lax.broadcasted_iota(jnp.int32, (t_new, A_BLOCK), 1)
    tq = lax.broadcasted_iota(jnp.int32, (t_new, t_new), 0)
    tj = lax.broadcasted_iota(jnp.int32, (t_new, t_new), 1)
    for h in range(A_HEADS):
        outs, lses = [], []
        for g, (win, dil) in enumerate(A_GROUPS):
            base = g * 3 * hw + h * A_HEAD_DIM
            q = qkv_ref[0, :, base:base + A_HEAD_DIM].astype(BF16)
            kn = qkv_ref[0, :, base + hw:base + hw + A_HEAD_DIM].astype(BF16)
            vn = qkv_ref[0, :, base + 2 * hw:base + 2 * hw + A_HEAD_DIM].astype(BF16)
            n_sub = min(dil, t_new)
            s_new = lax.dot_general(q, kn, nt, preferred_element_type=F32) * scale
            ok_new = jnp.logical_and(tj <= tq, jnp.bitwise_and(tq - tj, dil - 1) == 0)
            s_new = jnp.where(ok_new, s_new, -jnp.inf)
            ss, vs = [], []
            for r in range(n_sub):
                kb = caches[g][0, :, r * 2 * hw + h * A_HEAD_DIM:r * 2 * hw + (h + 1) * A_HEAD_DIM]
                vb = caches[g][0, :, r * 2 * hw + hw + h * A_HEAD_DIM:r * 2 * hw + hw + (h + 1) * A_HEAD_DIM]
                s = lax.dot_general(q, kb.astype(BF16), nt, preferred_element_type=F32) * scale
                ok = jnp.logical_and(jnp.bitwise_and(ti, dil - 1) == r, ni * dil + r >= ti)
                ss.append(jnp.where(ok, s, -jnp.inf))
                vs.append(vb.astype(BF16))
            m = jnp.max(s_new, axis=-1, keepdims=True)
            for s in ss:
                m = jnp.maximum(m, jnp.max(s, axis=-1, keepdims=True))
            e_new = jnp.exp(s_new - m)
            den = jnp.sum(e_new, axis=-1, keepdims=True)
            o = jnp.dot(e_new.astype(BF16), vn, preferred_element_type=F32)
            for s, vb in zip(ss, vs):
                e = jnp.exp(s - m)
                den = den + jnp.sum(e, axis=-1, keepdims=True)
                o = o + jnp.dot(e.astype(BF16), vb, preferred_element_type=F32)
            outs.append(o / den)
            lses.append(m + jnp.log(den))
        lm = jnp.maximum(jnp.maximum(lses[0], lses[1]), lses[2])
        ws = [jnp.exp(l - lm) for l in lses]
        tot = ws[0] + ws[1] + ws[2]
        comb = (ws[0] * outs[0] + ws[1] * outs[1] + ws[2] * outs[2]) / tot
        o_ref[0, :, h * A_HEAD_DIM:(h + 1) * A_HEAD_DIM] = comb


def attn_a_sample(qkv_s, caches, layer):
    db, t_new, width = qkv_s.shape
    hw = A_HEADS * A_HEAD_DIM
    in_specs = [pl.BlockSpec((1, t_new, width), lambda b: (b, 0, 0))]
    args = [qkv_s]
    for (win, dil), c in zip(A_GROUPS, caches):
        nl = c.shape[0]
        view = c.reshape(nl * db, win // dil, dil * 2 * hw)
        n_sub = min(dil, t_new)
        in_specs.append(pl.BlockSpec((1, win // dil, n_sub * 2 * hw),
                                     lambda b, layer=layer: (layer * db + b, 0, 0)))
        args.append(view)
    return pl.pallas_call(
        _attn_a_sample_kernel, grid=(db,), in_specs=in_specs,
        out_specs=pl.BlockSpec((1, t_new, hw), lambda b: (b, 0, 0)),
        out_shape=jax.ShapeDtypeStruct((db, t_new, hw), F32),
        compiler_params=_params(("parallel",)), name="attn_a_sample")(*args)


def _lambda(lp_ref, lam_init):
    lp = lp_ref[...]
    a = jnp.sum(lp[0:1, :] * lp[1:2, :], axis=-1, keepdims=True)
    b = jnp.sum(lp[2:3, :] * lp[3:4, :], axis=-1, keepdims=True)
    return jnp.exp(a) - jnp.exp(b) + lam_init


def _sub_ln(o, g, lam_init):
    on = o * lax.rsqrt(jnp.mean(o * o, axis=-1, keepdims=True) + SUBLN_EPS) * g
    return on * (1.0 - lam_init)


def _attn_b_kernel(q_ref, k_ref, v_ref, lp_ref, g_ref, o_ref, m_ref, l_ref, acc_ref, *, lam_init, tq, tk):
    qi = pl.program_id(2)
    ki = pl.program_id(3)
    nk = pl.num_programs(3)
    nt = (((1,), (1,)), ((), ()))
    sub = min(tq, 256)

    @pl.when(ki == 0)
    def _():
        m_ref[...] = jnp.full(m_ref.shape, -jnp.inf, F32)
        l_ref[...] = jnp.zeros(l_ref.shape, F32)
        acc_ref[...] = jnp.zeros(acc_ref.shape, F32)

    def step(masked):
        v = v_ref[...].astype(BF16)
        if masked:
            rows = qi * tq + lax.broadcasted_iota(jnp.int32, (tq, tk), 0)
            cols = ki * tk + lax.broadcasted_iota(jnp.int32, (tq, tk), 1)
            mask = cols <= rows
        for c in range(2):
            sl = slice(c * B_QK_DIM, (c + 1) * B_QK_DIM)
            kc = k_ref[:, sl].astype(BF16)
            for r in range(tq // sub):
                rs = slice(r * sub, (r + 1) * sub)
                s = lax.dot_general(q_ref[rs, sl], kc, nt, preferred_element_type=F32)
                if masked:
                    s = jnp.where(mask[rs, :], s, -jnp.inf)
                m_old = m_ref[c, rs, :]
                m_new = jnp.maximum(m_old, jnp.max(s, axis=-1, keepdims=True))
                alpha = jnp.exp(m_old - m_new)
                p = jnp.exp(s - m_new)
                l_ref[c, rs, :] = alpha * l_ref[c, rs, :] + jnp.sum(p, axis=-1, keepdims=True)
                acc_ref[c, rs, :] = (alpha * acc_ref[c, rs, :]
                                     + jnp.dot(p.astype(BF16), v, preferred_element_type=F32))
                m_ref[c, rs, :] = m_new

    first_row = qi * tq
    last_col = ki * tk + tk - 1

    @pl.when(last_col <= first_row)
    def _():
        step(False)

    @pl.when(jnp.logical_and(last_col > first_row, ki * tk <= first_row + tq - 1))
    def _():
        step(True)

    @pl.when(ki == nk - 1)
    def _():
        lam = _lambda(lp_ref, lam_init)
        o = acc_ref[0] / l_ref[0] - lam * (acc_ref[1] / l_ref[1])
        o_ref[...] = _sub_ln(o, g_ref[...], lam_init).astype(BF16)


def attn_b_prompt(q, kv, lp, g_sub, lam_init, batch, seq, *, tq=512, tk=512):
    tq = min(tq, seq)
    tk = min(tk, seq)
    nq, nk = seq // tq, seq // tk
    hd = 2 * B_QK_DIM

    def kmap(b, h, i, j):
        return (b * nk + jnp.minimum(j, (i * tq + tq - 1) // tk), h)

    def vmap_(b, h, i, j):
        return (b * nk + jnp.minimum(j, (i * tq + tq - 1) // tk), B_HEADS + h)

    kern = functools.partial(_attn_b_kernel, lam_init=lam_init, tq=tq, tk=tk)
    return pl.pallas_call(
        kern, grid=(batch, B_HEADS, nq, nk),
        in_specs=[pl.BlockSpec((tq, hd), lambda b, h, i, j: (b * nq + i, h)),
                  pl.BlockSpec((tk, hd), kmap),
                  pl.BlockSpec((tk, B_V_DIM), vmap_),
                  pl.BlockSpec((4, B_QK_DIM), lambda b, h, i, j: (0, 0)),
                  pl.BlockSpec((1, B_V_DIM), lambda b, h, i, j: (0, 0))],
        out_specs=pl.BlockSpec((tq, B_V_DIM), lambda b, h, i, j: (b * nq + i, h)),
        out_shape=jax.ShapeDtypeStruct((batch * seq, B_HEADS * B_V_DIM), BF16),
        scratch_shapes=[pltpu.VMEM((2, tq, 1), F32), pltpu.VMEM((2, tq, 1), F32),
                        pltpu.VMEM((2, tq, B_V_DIM), F32)],
        compiler_params=_params(("parallel", "parallel", "parallel", "arbitrary")),
        name="attn_b_prompt")(q, kv, kv, lp, g_sub.reshape(1, B_V_DIM))


def _attn_b_sample_kernel(*refs, lam_init, t_new, n_par):
    pt_ref, q_ref = refs[0], refs[1]
    page_refs = refs[2:2 + 2 * n_par]
    kvn_ref, lp_ref, g_ref, o_ref, m_ref, l_ref, acc_ref, xs_ref = refs[2 + 2 * n_par:]
    p = pl.program_id(1)
    n_steps = pl.num_programs(1)
    kw = B_HEADS * 2 * B_QK_DIM
    nt = (((1,), (1,)), ((), ()))
    rph = 2 * t_new

    @pl.when(p == 0)
    def _():
        m_ref[...] = jnp.full(m_ref.shape, -jnp.inf, F32)
        l_ref[...] = jnp.zeros(l_ref.shape, F32)
        acc_ref[...] = jnp.zeros(acc_ref.shape, F32)

    def update(state, s, weigh):
        m_old, l_old, acc_old = state
        m_new = jnp.maximum(m_old, jnp.max(s, axis=-1, keepdims=True))
        alpha = jnp.exp(m_old - m_new)
        e = jnp.exp(s - m_new)
        return m_new, alpha * l_old + jnp.sum(e, axis=-1, keepdims=True), alpha * acc_old + weigh(e)

    rows_per_page = 2 * PAGE_SIZE
    for h in range(B_HEADS):
        for half in range(2):
            for k in range(n_par):
                xs_ref[h, half, k * rows_per_page:(k + 1) * rows_per_page, :] = (
                    page_refs[2 * k + half][0, pl.ds(h, rows_per_page, stride=B_HEADS), :].astype(BF16))

    parts = []
    for h in range(B_HEADS):
        qh = q_ref[0, h * rph:(h + 1) * rph, :]
        parts.append(lax.dot_general(qh[:, 0:B_QK_DIM], xs_ref[h, 0], nt, preferred_element_type=F32)
                     + lax.dot_general(qh[:, B_QK_DIM:], xs_ref[h, 1], nt, preferred_element_type=F32))
    s = jnp.concatenate(parts, axis=0)
    is_key = jnp.bitwise_and(lax.broadcasted_iota(jnp.int32, s.shape, 1), 1) == 0
    s = jnp.where(is_key, s, -jnp.inf)

    def weigh(e):
        ev = pltpu.roll(e, 1, 1)
        outs = []
        for h in range(B_HEADS):
            evh = ev[h * rph:(h + 1) * rph, :].astype(BF16)
            outs.append(jnp.concatenate([jnp.dot(evh, xs_ref[h, 0], preferred_element_type=F32),
                                         jnp.dot(evh, xs_ref[h, 1], preferred_element_type=F32)], axis=1))
        return jnp.concatenate(outs, axis=0)

    m_new, l_new, acc_new = update((m_ref[...], l_ref[...], acc_ref[...]), s, weigh)
    m_ref[...] = m_new
    l_ref[...] = l_new
    acc_ref[...] = acc_new

    @pl.when(p == n_steps - 1)
    def _():
        lam = _lambda(lp_ref, lam_init)
        for h in range(B_HEADS):
            rs = slice(h * rph, (h + 1) * rph)
            qh = q_ref[0, rs, :]
            kn = kvn_ref[0, :, h * B_V_DIM:(h + 1) * B_V_DIM].astype(BF16)
            vn = kvn_ref[0, :, kw + h * B_V_DIM:kw + (h + 1) * B_V_DIM].astype(BF16)
            sn = lax.dot_general(qh, kn, nt, preferred_element_type=F32)
            rq = jnp.bitwise_and(lax.broadcasted_iota(jnp.int32, sn.shape, 0), t_new - 1)
            cj = lax.broadcasted_iota(jnp.int32, sn.shape, 1)
            _, l_fin, acc_fin = update(
                (m_ref[rs, :], l_ref[rs, :], acc_ref[rs, :]), jnp.where(cj <= rq, sn, -jnp.inf),
                lambda e, vn=vn: jnp.dot(e.astype(BF16), vn, preferred_element_type=F32))
            on = acc_fin / l_fin
            o = on[0:t_new, :] - lam * on[t_new:2 * t_new, :]
            o_ref[0, :, h * B_V_DIM:(h + 1) * B_V_DIM] = _sub_ln(o, g_ref[...], lam_init)


def attn_b_sample(q_s, kv_s, cache_b_kv, page_table, lp, g_sub, lam_init):
    db, t_new, qw = q_s.shape
    n_pages = page_table.shape[1]
    n_phys = cache_b_kv.shape[0]
    assert t_new & (t_new - 1) == 0
    n_par = 4 if n_pages % 4 == 0 else 1
    kvw = 2 * B_HEADS * B_V_DIM
    page_rows = PAGE_SIZE * 2 * B_HEADS
    pages = cache_b_kv.reshape(n_phys, page_rows, B_V_DIM)
    q5 = q_s.reshape(db, t_new, B_HEADS, 2, B_QK_DIM).transpose(0, 2, 3, 1, 4)
    eye = jnp.eye(2, dtype=q_s.dtype)
    qbd = q5[:, :, :, :, None, :] * eye[None, None, :, None, :, None]
    rows = B_HEADS * 2 * t_new
    qbd = qbd.reshape(db, rows, 2 * B_QK_DIM).astype(BF16)
    kern = functools.partial(_attn_b_sample_kernel, lam_init=lam_init, t_new=t_new, n_par=n_par)
    page_specs = [pl.BlockSpec((1, page_rows, B_QK_DIM),
                               lambda b, p, pt, k=k, half=half: (pt[b, p * n_par + k], 0, half))
                  for k in range(n_par) for half in range(2)]
    grid_spec = pltpu.PrefetchScalarGridSpec(
        num_scalar_prefetch=1, grid=(db, n_pages // n_par),
        in_specs=[pl.BlockSpec((1, rows, 2 * B_QK_DIM), lambda b, p, pt: (b, 0, 0))] + page_specs + [
                  pl.BlockSpec((1, t_new, kvw), lambda b, p, pt: (b, 0, 0)),
                  pl.BlockSpec((4, B_QK_DIM), lambda b, p, pt: (0, 0)),
                  pl.BlockSpec((1, B_V_DIM), lambda b, p, pt: (0, 0))],
        out_specs=pl.BlockSpec((1, t_new, qw), lambda b, p, pt: (b, 0, 0)),
        scratch_shapes=[pltpu.VMEM((rows, 1), F32), pltpu.VMEM((rows, 1), F32),
                        pltpu.VMEM((rows, B_V_DIM), F32),
                        pltpu.VMEM((B_HEADS, 2, n_par * 2 * PAGE_SIZE, B_QK_DIM), BF16)])
    return pl.pallas_call(
        kern, grid_spec=grid_spec,
        out_shape=jax.ShapeDtypeStruct((db, t_new, qw), F32),
        compiler_params=_params(("parallel", "arbitrary")),
        name="attn_b_sample")(page_table, qbd, *([pages] * (2 * n_par)), kv_s, lp, g_sub.reshape(1, B_V_DIM))


def _top_ranks(s):
    iota = lax.broadcasted_iota(jnp.int32, s.shape, 0).astype(F32)
    rank = jnp.full(s.shape, NOT_SELECTED, F32)
    vals = []
    work = s
    for a in range(PEER_TOPK):
        m = jnp.max(work, axis=0, keepdims=True)
        idx = jnp.min(jnp.where(work == m, iota, float(N_KEYS)), axis=0, keepdims=True)
        hit = iota == idx
        rank = jnp.where(hit, float(a), rank)
        work = jnp.where(hit, -jnp.inf, work)
        vals.append(m)
    return vals, rank


def _candidate_rows():
    pieces = [(0, 0, 16, 16)]
    for a in range(1, 8):
        pieces.append((a, 0, 8, PEER_TOPK // (a + 1)))
    return pieces


def _route_kernel(ht_ref, wq_ref, sk_ref, r1_ref, e1_ref, bq_out_ref, c0_out_ref, qt_ref, bq_ref, c0_ref):
    n_chunks = ht_ref.shape[1] // LANES
    qt_ref[...] = jnp.dot(wq_ref[...], ht_ref[...], preferred_element_type=F32)
    neg = -jnp.inf
    t = LANES

    def body(it, carry):
        h = it // n_chunks
        cs = pl.ds(pl.multiple_of((it % n_chunks) * LANES, LANES), LANES)
        vals, ranks, scores = [], [], []
        for c in range(2):
            hc = h * 2 + c
            qhc = qt_ref[pl.ds(pl.multiple_of(hc * N_KEYS, N_KEYS), N_KEYS), cs].astype(BF16)
            s = jnp.dot(sk_ref[hc].astype(BF16), qhc, preferred_element_type=F32)
            v, r = _top_ranks(s)
            vals.append(v); ranks.append(r); scores.append(s)
        v0, v1 = vals
        v1_16 = jnp.concatenate(v1, axis=0)
        v0_hi = jnp.concatenate(v0[8:16], axis=0)
        cands, flats = [], []
        for a, _, rows, nvalid in _candidate_rows():
            b_iota = lax.broadcasted_iota(jnp.int32, (rows, t), 0)
            cs_ab = v0[a] + v1_16[0:rows, :]
            cands.append(jnp.where(b_iota < nvalid, cs_ab, neg))
            flats.append((b_iota + a * PEER_TOPK).astype(F32))
        cands.append(v0_hi + v1[0])
        flats.append(((lax.broadcasted_iota(jnp.int32, (8, t), 0) + 8) * PEER_TOPK).astype(F32))
        cand = jnp.concatenate(cands, axis=0)
        flat = jnp.concatenate(flats, axis=0)
        big = float(PEER_TOPK * PEER_TOPK)
        work = cand
        sel = jnp.zeros(cand.shape, F32)
        for _ in range(PEER_TOPK):
            m = jnp.max(work, axis=0, keepdims=True)
            idx = jnp.min(jnp.where(work == m, flat, big), axis=0, keepdims=True)
            hit = flat == idx
            sel = jnp.where(hit, 1.0, sel)
            work = jnp.where(hit, neg, work)
        top = v0[0] + v1[0]
        z = jnp.sum(jnp.where(sel > 0.0, jnp.exp(cand - top), 0.0), axis=0, keepdims=True)
        counts = [jnp.sum(sel[0:16, :], axis=0, keepdims=True)]
        for k in range(1, 8):
            counts.append(jnp.sum(sel[8 + 8 * k:16 + 8 * k, :], axis=0, keepdims=True))
        hi = sel[72:80, :]
        bq = jnp.zeros(ranks[0].shape, F32)
        for a in range(PEER_TOPK):
            cnt = counts[a] if a < 8 else hi[a - 8:a - 7, :]
            bq = jnp.where(ranks[0] == float(a), cnt, bq)
        rs = pl.ds(pl.multiple_of(h * N_KEYS, N_KEYS), N_KEYS)
        bq_ref[rs, cs] = bq
        c0_ref[rs, cs] = jnp.exp(scores[0] - v0[0]) / z
        r1_ref[rs, cs] = ranks[1].astype(BF16)
        e1_ref[rs, cs] = jnp.exp(scores[1] - v1[0]).astype(BF16)
        return carry

    lax.fori_loop(0, PEER_HEADS * n_chunks, body, 0, unroll=2)
    for h in range(PEER_HEADS):
        bq_out_ref[:, h, :] = bq_ref[h * N_KEYS:(h + 1) * N_KEYS, :]
        c0_out_ref[:, h, :] = c0_ref[h * N_KEYS:(h + 1) * N_KEYS, :]


def peer_route(ht, wq_t, subkeys, *, tm=256):
    d, n = ht.shape
    rows = PEER_HEADS * N_KEYS
    slab = pl.BlockSpec((rows, tm), lambda i: (0, i))
    sds = jax.ShapeDtypeStruct((rows, n), BF16)
    slab3 = pl.BlockSpec((N_KEYS, PEER_HEADS, tm), lambda i: (0, 0, i))
    sds3 = jax.ShapeDtypeStruct((N_KEYS, PEER_HEADS, n), F32)
    return pl.pallas_call(
        _route_kernel, grid=(n // tm,),
        in_specs=[pl.BlockSpec((d, tm), lambda i: (0, i)),
                  pl.BlockSpec(wq_t.shape, lambda i: (0, 0)),
                  pl.BlockSpec(subkeys.shape, lambda i: (0, 0, 0))],
        out_specs=[slab, slab, slab3, slab3], out_shape=[sds, sds, sds3, sds3],
        scratch_shapes=[pltpu.VMEM((wq_t.shape[0], tm), F32), pltpu.VMEM((rows, tm), F32),
                        pltpu.VMEM((rows, tm), F32)],
        compiler_params=_params(("parallel",)), name="peer_route")(ht, wq_t, subkeys)


def _gelu(a):
    return 0.5 * a * (1.0 + lax.erf(a * math.sqrt(0.5)))


def _expert_kernel(x_ref, ht_ref, u_ref, v_ref, r1_ref, e1_ref, bq_ref, c0_ref, o_ref, a_ref, *, te, tm, ge):
    e = pl.program_id(1)
    rows_per_group = ge // N_KEYS
    n_chunks = tm // LANES
    packed = 2 * SUBLANES
    tiles = N_KEYS // packed

    @pl.when(e == 0)
    def _():
        o_ref[...] = x_ref[...]

    ht = ht_ref[...]
    n_groups = te // ge

    def scores(k):
        a_ref[k] = jnp.dot(u_ref[k * ge:(k + 1) * ge, :], ht, preferred_element_type=F32)

    scores(0)
    for k in range(n_groups):
        if k + 1 < n_groups:
            scores(k + 1)
        rows = []
        for r in range(rows_per_group):
            i = e * (te // N_KEYS) + k * rows_per_group + r
            cols = []
            for c in range(n_chunks):
                cs = slice(c * LANES, (c + 1) * LANES)
                bq_all = bq_ref[i, :, cs]
                c0_all = c0_ref[i, :, cs]
                w = [jnp.zeros((packed, LANES), BF16)] * tiles
                for h in range(PEER_HEADS):
                    bq = jnp.broadcast_to(bq_all[h:h + 1, :], (packed, LANES)).astype(BF16)
                    c0 = jnp.broadcast_to(c0_all[h:h + 1, :], (packed, LANES)).astype(BF16)
                    for t in range(tiles):
                        js = slice(h * N_KEYS + t * packed, h * N_KEYS + (t + 1) * packed)
                        e1 = e1_ref[js, cs]
                        w[t] = w[t] + jnp.where(r1_ref[js, cs] < bq, e1 * c0, jnp.zeros_like(e1))
                gate = jnp.concatenate(w, axis=0).astype(F32)
                cols.append(_gelu(a_ref[k, r * N_KEYS:(r + 1) * N_KEYS, cs]) * gate)
            rows.append(jnp.concatenate(cols, axis=1))
        gt = jnp.concatenate(rows, axis=0)
        g = gt.T.astype(BF16)
        o_ref[...] += jnp.dot(g, v_ref[k * ge:(k + 1) * ge, :], preferred_element_type=F32)


def peer_experts(x, ht, u_bf, v_bf, slabs, *, tm=512, te=1024, ge=256):
    d, n = ht.shape
    n_exp = u_bf.shape[0]
    while n % tm:
        tm //= 2
    te = min(te, n_exp)
    rows = PEER_HEADS * N_KEYS
    once = pl.Buffered(1)
    slab = pl.BlockSpec((rows, tm), lambda i, e: (0, i), pipeline_mode=once)
    slab3 = pl.BlockSpec((N_KEYS, PEER_HEADS, tm), lambda i, e: (0, 0, i), pipeline_mode=once)
    kern = functools.partial(_expert_kernel, te=te, tm=tm, ge=ge)
    return pl.pallas_call(
        kern, grid=(n // tm, n_exp // te),
        in_specs=[pl.BlockSpec((tm, d), lambda i, e: (i, 0), pipeline_mode=once),
                  pl.BlockSpec((d, tm), lambda i, e: (0, i), pipeline_mode=once),
                  pl.BlockSpec((te, d), lambda i, e: (e, 0)),
                  pl.BlockSpec((te, d), lambda i, e: (e, 0)),
                  slab, slab, slab3, slab3],
        out_specs=pl.BlockSpec((tm, d), lambda i, e: (i, 0)),
        out_shape=jax.ShapeDtypeStruct((n, d), F32),
        scratch_shapes=[pltpu.VMEM((te // ge, ge, tm), F32)],
        compiler_params=_params(("parallel", "arbitrary")),
        name="peer_experts")(x, ht, u_bf, v_bf, *slabs)


def _rope_table(pos):
    rot = A_HEAD_DIM // 4
    half = rot // 2
    inv_freq = ROPE_THETA ** (-jnp.arange(half, dtype=F32) / half)
    ang = pos.astype(F32)[:, None] * inv_freq[None, :]
    cos, sin = jnp.cos(ang), jnp.sin(ang)
    n = pos.shape[0]
    ones = jnp.ones((n, A_HEAD_DIM - rot), F32)
    zeros = jnp.zeros((n, A_HEAD_DIM - rot), F32)
    zh = jnp.zeros((n, half), F32)
    return jnp.concatenate([cos, cos, ones, -sin, zh, zeros, zh, sin, zeros], axis=1)


def kernel(x_prompt, x_sample, cache_a_w128, cache_a_w512, cache_a_w2048, cache_b_kv, page_table,
           p_prompt, p_sample, norm_mix, norm_ffn, norm_ple, norm_kv, norm_final,
           w_qkv_a, w_o_a, w_kv_b, w_q_b, diff_lambda, norm_sub_b, w_o_b,
           peer_wq, peer_subkeys, peer_u, peer_v, w_ple, w_ple_gate):
    batch, seq, d = x_prompt.shape
    db, t_new, _ = x_sample.shape
    depth = norm_mix.shape[0]
    n_a = w_qkv_a.shape[0]
    past_len = page_table.shape[1] * PAGE_SIZE
    n_p, n_s = batch * seq, db * t_new
    n_tot = -(-(n_p + n_s) // TOKEN_PAD) * TOKEN_PAD
    pad = n_tot - n_p - n_s
    a_caches = (cache_a_w128, cache_a_w512, cache_a_w2048)
    hw = A_HEADS * A_HEAD_DIM

    def tokens(prompt_part, sample_part):
        w = prompt_part.shape[-1]
        return jnp.concatenate([prompt_part.reshape(n_p, w), sample_part.reshape(n_s, w),
                                jnp.zeros((pad, w), prompt_part.dtype)], axis=0)

    x = tokens(x_prompt, x_sample)
    pos = jnp.concatenate([jnp.tile(jnp.arange(seq, dtype=jnp.int32), batch),
                           jnp.tile(past_len + jnp.arange(t_new, dtype=jnp.int32), db),
                           jnp.zeros((pad,), jnp.int32)])
    tab = _rope_table(pos)
    u_bf = peer_u.astype(BF16)
    v_bf = peer_v.astype(BF16)

    a_rows_p = [[] for _ in A_GROUPS]
    a_rows_s = [[] for _ in A_GROUPS]
    new_b_kv_prompt = new_b_kv_sample = kv = None
    for i in range(depth):
        h = rms_norm(x, norm_mix[i], out_h=True)[0]
        if i < n_a:
            qkv = matmul(h, w_qkv_a[i], mode="rope", tab=tab, tn=hw,
                         rope_fn=lambda col: (col // hw) % 3 != 2)
            o_p = attn_a_prompt(qkv, batch, seq)
            qkv_s = qkv[n_p:n_p + n_s].reshape(db, t_new, qkv.shape[1])
            o_s = attn_a_sample(qkv_s, a_caches, i).reshape(n_s, hw).astype(BF16)
            o_all = jnp.concatenate([o_p, o_s, jnp.zeros((pad, hw), BF16)], axis=0)
            x = matmul(o_all, w_o_a[i], mode="res", res=x)
            for g, (win, dil) in enumerate(A_GROUPS):
                wb = min(win, seq)
                c0, c1 = (g * 3 + 1) * hw, (g * 3 + 3) * hw
                rows = jnp.stack([qkv[(b + 1) * seq - wb:(b + 1) * seq, c0:c1] for b in range(batch)], axis=0)
                a_rows_p[g].append(rows.reshape(batch, wb, 2, A_HEADS, A_HEAD_DIM))
                a_rows_s[g].append(qkv[n_p:n_p + n_s, c0:c1].reshape(db, t_new, 2, A_HEADS, A_HEAD_DIM))
        else:
            j = i - n_a
            if j == 0:
                hkv = rms_norm(x, norm_kv, out_h=True)[0]
                kw = B_HEADS * 2 * B_QK_DIM
                kv = matmul(hkv, w_kv_b, mode="rope", tab=tab, tn=1024, rope_fn=lambda col: col < kw)
                new_b_kv_prompt = kv[:n_p].reshape(batch, seq, 2, B_HEADS, B_V_DIM)
                new_b_kv_sample = kv[n_p:n_p + n_s].reshape(db, t_new, 2, B_HEADS, B_V_DIM)
            lam_init = 0.8 - 0.6 * math.exp(-0.3 * i)
            q = matmul(h, w_q_b[j], mode="rope", tab=tab, tn=1024, rope_fn=lambda col: col >= 0,
                       out_scale=1.0 / math.sqrt(B_QK_DIM), out_dtype=BF16)
            o_p = attn_b_prompt(q, kv, diff_lambda[j], norm_sub_b[j], lam_init, batch, seq)
            q_s = q[n_p:n_p + n_s].reshape(db, t_new, q.shape[1])
            kv_s = kv[n_p:n_p + n_s].reshape(db, t_new, kv.shape[1])
            o_s = attn_b_sample(q_s, kv_s, cache_b_kv, page_table, diff_lambda[j], norm_sub_b[j], lam_init)
            o_all = jnp.concatenate([o_p, o_s.reshape(n_s, -1).astype(BF16),
                                     jnp.zeros((pad, o_p.shape[1]), BF16)], axis=0)
            x = matmul(o_all, w_o_b[j], mode="res", res=x)
        ht = rms_norm(x, norm_ffn[i], out_ht=True)[0]
        wq_t = peer_wq[i].T.astype(BF16)
        sk = peer_subkeys[i].reshape(PEER_HEADS * 2, N_KEYS, -1)
        slabs = peer_route(ht, wq_t, sk)
        x = peer_experts(x, ht, u_bf[i], v_bf[i], slabs)
        hn = rms_norm(x, norm_ple[i], out_h=True)[0]
        x = ple(x, tokens(p_prompt[i], p_sample[i]), hn, w_ple[i], w_ple_gate[i])

    y = rms_norm(x, norm_final, out_y=True)[0]
    y_prompt = y[:n_p].reshape(batch, seq, d)
    y_sample = y[n_p:n_p + n_s].reshape(db, t_new, d)
    outs_p = [jnp.stack(r, axis=0) for r in a_rows_p]
    outs_s = [jnp.stack(r, axis=0) for r in a_rows_s]
    return (y_prompt, y_sample, *outs_p, *outs_s, new_b_kv_prompt, new_b_kv_sample)
```

```python
import functools
import math

import jax
import jax.numpy as jnp
from jax import lax
from jax.experimental import pallas as pl
from jax.experimental.pallas import tpu as pltpu

BF16 = jnp.bfloat16
F32 = jnp.float32

ROPE_THETA = 500000.0
NORM_EPS = 1e-6
SUBLN_EPS = 1e-5
A_GROUPS = ((128, 1), (512, 4), (2048, 16))
A_HEADS = 8
A_HEAD_DIM = 128
A_BLOCK = 128
B_HEADS = 8
B_QK_DIM = 128
B_V_DIM = 256
PEER_HEADS = 8
N_KEYS = 128
PEER_TOPK = 16
PAGE_SIZE = 128

LANES = 128
SUBLANES = 8
VMEM_LIMIT = 56 * 1024 * 1024
TOKEN_PAD = 512
NOT_SELECTED = 99.0


def _params(sem):
    return pltpu.CompilerParams(dimension_semantics=sem, vmem_limit_bytes=VMEM_LIMIT)


def _rms_kernel(*refs, eps, has_add, out_x, out_h, out_ht, out_y):
    it = iter(refs)
    x_ref = next(it)
    add_ref = next(it) if has_add else None
    g_ref = next(it)
    x = x_ref[...]
    if has_add:
        x = x + add_ref[...].T
    if out_x:
        next(it)[...] = x
    y = x * lax.rsqrt(jnp.mean(x * x, axis=-1, keepdims=True) + eps) * g_ref[...]
    if out_h:
        next(it)[...] = y.astype(BF16)
    if out_ht:
        next(it)[...] = y.T.astype(BF16)
    if out_y:
        next(it)[...] = y


def rms_norm(x, g, *, add_t=None, out_x=False, out_h=False, out_ht=False, out_y=False,
             eps=NORM_EPS, tm=256):
    n, d = x.shape
    grid = (n // tm,)
    row = pl.BlockSpec((tm, d), lambda i: (i, 0))
    col = pl.BlockSpec((d, tm), lambda i: (0, i))
    in_specs = [row]
    args = [x]
    if add_t is not None:
        in_specs.append(col)
        args.append(add_t)
    in_specs.append(pl.BlockSpec((1, d), lambda i: (0, 0)))
    args.append(g.reshape(1, d))
    out_shape, out_specs = [], []
    if out_x:
        out_shape.append(jax.ShapeDtypeStruct((n, d), F32)); out_specs.append(row)
    if out_h:
        out_shape.append(jax.ShapeDtypeStruct((n, d), BF16)); out_specs.append(row)
    if out_ht:
        out_shape.append(jax.ShapeDtypeStruct((d, n), BF16)); out_specs.append(col)
    if out_y:
        out_shape.append(jax.ShapeDtypeStruct((n, d), F32)); out_specs.append(row)
    kern = functools.partial(_rms_kernel, eps=eps, has_add=add_t is not None, out_x=out_x,
                             out_h=out_h, out_ht=out_ht, out_y=out_y)
    return pl.pallas_call(kern, grid=grid, in_specs=in_specs, out_specs=out_specs,
                          out_shape=out_shape, compiler_params=_params(("parallel",)),
                          name="rms_norm")(*args)


def _rope_tile(y, tab):
    c = tab[:, 0:LANES]
    s1 = tab[:, LANES:2 * LANES]
    s2 = tab[:, 2 * LANES:3 * LANES]
    outs = []
    for g in range(y.shape[1] // LANES):
        yg = y[:, g * LANES:(g + 1) * LANES]
        outs.append(yg * c + pltpu.roll(yg, LANES - 16, 1) * s1 + pltpu.roll(yg, 16, 1) * s2)
    return jnp.concatenate(outs, axis=1) if len(outs) > 1 else outs[0]


def _mm_kernel(*refs, mode, rope_fn, tn, out_scale):
    if mode == "rope":
        x_ref, w_ref, tab_ref, o_ref, wb_ref = refs
    elif mode == "res":
        x_ref, w_ref, r_ref, o_ref, wb_ref = refs
    else:
        x_ref, w_ref, o_ref, wb_ref = refs
    j = pl.program_id(0)

    @pl.when(pl.program_id(1) == 0)
    def _():
        wb_ref[...] = w_ref[...].astype(BF16)

    y = jnp.dot(x_ref[...], wb_ref[...], preferred_element_type=F32)
    if mode == "rope":
        roped = rope_fn(j * tn)

        def finish(v):
            return (v if out_scale == 1.0 else v * out_scale).astype(o_ref.dtype)

        @pl.when(roped)
        def _():
            o_ref[...] = finish(_rope_tile(y, tab_ref[...]))

        @pl.when(jnp.logical_not(roped))
        def _():
            o_ref[...] = finish(y)
    elif mode == "res":
        o_ref[...] = r_ref[...] + y
    else:
        o_ref[...] = y


def matmul(x, w, *, mode="plain", tab=None, res=None, rope_fn=None, tm=512, tn=512,
           out_scale=1.0, out_dtype=F32):
    n, k = x.shape
    m = w.shape[1]
    tm = min(tm, n)
    tn = min(tn, m)
    while n % tm:
        tm //= 2
    grid = (m // tn, n // tm)
    in_specs = [pl.BlockSpec((tm, k), lambda j, i: (i, 0)),
                pl.BlockSpec((k, tn), lambda j, i: (0, j))]
    args = [x, w]
    if mode == "rope":
        in_specs.append(pl.BlockSpec((tm, 3 * LANES), lambda j, i: (i, 0)))
        args.append(tab)
    elif mode == "res":
        in_specs.append(pl.BlockSpec((tm, tn), lambda j, i: (i, j)))
        args.append(res)
    assert mode == "rope" or (out_scale == 1.0 and out_dtype == F32)
    kern = functools.partial(_mm_kernel, mode=mode, rope_fn=rope_fn, tn=tn, out_scale=out_scale)
    return pl.pallas_call(
        kern, grid=grid, in_specs=in_specs,
        out_specs=pl.BlockSpec((tm, tn), lambda j, i: (i, j)),
        out_shape=jax.ShapeDtypeStruct((n, m), out_dtype),
        scratch_shapes=[pltpu.VMEM((k, tn), BF16)],
        compiler_params=_params(("arbitrary", "arbitrary")),
        name="matmul_" + mode)(*args)


def _ple_kernel(x_ref, p_ref, hn_ref, wp_ref, wg_ref, o_ref, wgb_ref):
    @pl.when(pl.program_id(1) == 0)
    def _():
        wgb_ref[...] = wg_ref[...].astype(BF16)

    gate = jax.nn.sigmoid(jnp.dot(hn_ref[...], wgb_ref[...], preferred_element_type=F32))
    up = jnp.dot(p_ref[...].astype(BF16), wp_ref[...].astype(BF16), preferred_element_type=F32)
    o_ref[...] = x_ref[...] + up * gate


def ple(x, p, hn, w_p, w_gate, *, tm=512, tn=512):
    n, d = x.shape
    kp = p.shape[1]
    tn = min(tn, d)
    while n % tm:
        tm //= 2
    grid = (d // tn, n // tm)
    return pl.pallas_call(
        _ple_kernel, grid=grid,
        in_specs=[pl.BlockSpec((tm, tn), lambda j, i: (i, j)),
                  pl.BlockSpec((tm, kp), lambda j, i: (i, 0)),
                  pl.BlockSpec((tm, d), lambda j, i: (i, 0)),
                  pl.BlockSpec((kp, tn), lambda j, i: (0, j)),
                  pl.BlockSpec((d, tn), lambda j, i: (0, j))],
        out_specs=pl.BlockSpec((tm, tn), lambda j, i: (i, j)),
        out_shape=jax.ShapeDtypeStruct((n, d), F32),
        scratch_shapes=[pltpu.VMEM((d, tn), BF16)],
        compiler_params=_params(("arbitrary", "arbitrary")),
        name="ple")(x, p, hn, w_p, w_gate)


def _attn_a_kernel(*refs):
    n_in = 5 * len(A_GROUPS)
    o_ref, og_ref, lg_ref = refs[n_in:n_in + 3]
    blk = pl.program_id(1)
    t_blk = o_ref.shape[0]
    qi = lax.broadcasted_iota(jnp.int32, (A_BLOCK, A_BLOCK), 0)
    kj = lax.broadcasted_iota(jnp.int32, (A_BLOCK, A_BLOCK), 1)
    far = kj >= qi
    near = kj <= qi
    scale = 1.0 / math.sqrt(A_HEAD_DIM)
    nt = (((1,), (1,)), ((), ()))
    for g, (win, dil) in enumerate(A_GROUPS):
        q_ref, kc_ref, vc_ref, kp_ref, vp_ref = refs[5 * g:5 * g + 5]
        span = A_BLOCK * dil

        def body(it, carry, g=g, dil=dil, span=span, q_ref=q_ref, kc_ref=kc_ref, vc_ref=vc_ref,
                 kp_ref=kp_ref, vp_ref=vp_ref):
            n = it // dil
            r = it % dil
            start = n * span + r
            rows = pl.ds(start, A_BLOCK, stride=dil)
            before = pl.ds(jnp.maximum(start - span, 0), A_BLOCK, stride=dil)
            outside = pl.ds(r, A_BLOCK, stride=dil)
            first = n == 0
            q = q_ref[rows, :].astype(BF16)
            kc = kc_ref[rows, :].astype(BF16)
            vc = vc_ref[rows, :].astype(BF16)
            if span == t_blk:
                kp = kp_ref[outside, :].astype(BF16)
                vp = vp_ref[outside, :].astype(BF16)
            else:
                kp = jnp.where(first, kp_ref[outside, :], kc_ref[before, :]).astype(BF16)
                vp = jnp.where(first, vp_ref[outside, :], vc_ref[before, :]).astype(BF16)
            has_prev = jnp.logical_or(n > 0, blk > 0)
            sp = lax.dot_general(q, kp, nt, preferred_element_type=F32) * scale
            sc = lax.dot_general(q, kc, nt, preferred_element_type=F32) * scale
            sp = jnp.where(jnp.logical_and(far, has_prev), sp, -jnp.inf)
            sc = jnp.where(near, sc, -jnp.inf)
            m = jnp.maximum(jnp.max(sp, axis=-1, keepdims=True), jnp.max(sc, axis=-1, keepdims=True))
            ep = jnp.exp(sp - m)
            ec = jnp.exp(sc - m)
            den = jnp.sum(ep, axis=-1, keepdims=True) + jnp.sum(ec, axis=-1, keepdims=True)
            o = (jnp.dot(ep.astype(BF16), vp, preferred_element_type=F32)
                 + jnp.dot(ec.astype(BF16), vc, preferred_element_type=F32))
            og_ref[g, rows, :] = o / den
            lg_ref[g, rows, :] = jnp.broadcast_to(m + jnp.log(den), (A_BLOCK, A_HEAD_DIM))
            return carry

        lax.fori_loop(0, t_blk // A_BLOCK, body, 0, unroll=8)
    l0, l1, l2 = lg_ref[0], lg_ref[1], lg_ref[2]
    m = jnp.maximum(jnp.maximum(l0, l1), l2)
    e0, e1, e2 = jnp.exp(l0 - m), jnp.exp(l1 - m), jnp.exp(l2 - m)
    out = (e0 * og_ref[0] + e1 * og_ref[1] + e2 * og_ref[2]) / (e0 + e1 + e2)
    o_ref[...] = out.astype(BF16)


def attn_a_prompt(qkv, batch, seq):
    n_tot, width = qkv.shape
    hw = A_HEADS * A_HEAD_DIM
    t_blk = A_BLOCK * max(d for _, d in A_GROUPS)
    assert seq % t_blk == 0
    nblk = seq // t_blk
    in_specs, args = [], []
    for g, (win, dil) in enumerate(A_GROUPS):
        span = A_BLOCK * dil
        per_blk = t_blk // span
        for which in range(3):
            col = (g * 3 + which) * A_HEADS
            in_specs.append(pl.BlockSpec((t_blk, A_HEAD_DIM),
                                         lambda b, k, h, col=col: (b * nblk + k, col + h)))
            args.append(qkv)
        for which in (1, 2):
            col = (g * 3 + which) * A_HEADS
            in_specs.append(pl.BlockSpec(
                (span, A_HEAD_DIM),
                lambda b, k, h, col=col, per_blk=per_blk: (jnp.maximum((b * nblk + k) * per_blk - 1, 0), col + h)))
            args.append(qkv)
    scratch = pltpu.VMEM((len(A_GROUPS), t_blk, A_HEAD_DIM), F32)
    return pl.pallas_call(
        _attn_a_kernel, grid=(batch, nblk, A_HEADS), in_specs=in_specs,
        out_specs=pl.BlockSpec((t_blk, A_HEAD_DIM), lambda b, k, h: (b * nblk + k, h)),
        out_shape=jax.ShapeDtypeStruct((batch * seq, hw), BF16),
        scratch_shapes=[scratch, scratch],
        compiler_params=_params(("parallel", "parallel", "parallel")),
        name="attn_a_prompt")(*args)


def _attn_a_sample_kernel(qkv_ref, c0_ref, c1_ref, c2_ref, o_ref):
    t_new = qkv_ref.shape[1]
    hw = A_HEADS * A_HEAD_DIM
    scale = 1.0 / math.sqrt(A_HEAD_DIM)
    nt = (((1,), (1,)), ((), ()))
    caches = (c0_ref, c1_ref, c2_ref)
    ti = lax.broadcasted_iota(jnp.int32, (t_new, A_BLOCK), 0)
    ni = lax.broadcasted_iota(jnp.int32, (t_new, A_BLOCK), 1)
    tq = lax.broadcasted_iota(jnp.int32, (t_new, t_new), 0)
    tj = lax.broadcasted_iota(jnp.int32, (t_new, t_new), 1)
    for h in range(A_HEADS):
        outs, lses = [], []
        for g, (win, dil) in enumerate(A_GROUPS):
            base = g * 3 * hw + h * A_HEAD_DIM
            q = qkv_ref[0, :, base:base + A_HEAD_DIM].astype(BF16)
            kn = qkv_ref[0, :, base + hw:base + hw + A_HEAD_DIM].astype(BF16)
            vn = qkv_ref[0, :, base + 2 * hw:base + 2 * hw + A_HEAD_DIM].astype(BF16)
            n_sub = min(dil, t_new)
            s_new = lax.dot_general(q, kn, nt, preferred_element_type=F32) * scale
            ok_new = jnp.logical_and(tj <= tq, jnp.bitwise_and(tq - tj, dil - 1) == 0)
            s_new = jnp.where(ok_new, s_new, -jnp.inf)
            ss, vs = [], []
            for r in range(n_sub):
                kb = caches[g][0, :, r * 2 * hw + h * A_HEAD_DIM:r * 2 * hw + (h + 1) * A_HEAD_DIM]
                vb = caches[g][0, :, r * 2 * hw + hw + h * A_HEAD_DIM:r * 2 * hw + hw + (h + 1) * A_HEAD_DIM]
                s = lax.dot_general(q, kb.astype(BF16), nt, preferred_element_type=F32) * scale
                ok = jnp.logical_and(jnp.bitwise_and(ti, dil - 1) == r, ni * dil + r >= ti)
                ss.append(jnp.where(ok, s, -jnp.inf))
                vs.append(vb.astype(BF16))
            m = jnp.max(s_new, axis=-1, keepdims=True)
            for s in ss:
                m = jnp.maximum(m, jnp.max(s, axis=-1, keepdims=True))
            e_new = jnp.exp(s_new - m)
            den = jnp.sum(e_new, axis=-1, keepdims=True)
            o = jnp.dot(e_new.astype(BF16), vn, preferred_element_type=F32)
            for s, vb in zip(ss, vs):
                e = jnp.exp(s - m)
                den = den + jnp.sum(e, axis=-1, keepdims=True)
                o = o + jnp.dot(e.astype(BF16), vb, preferred_element_type=F32)
            outs.append(o / den)
            lses.append(m + jnp.log(den))
        lm = jnp.maximum(jnp.maximum(lses[0], lses[1]), lses[2])
        ws = [jnp.exp(l - lm) for l in lses]
        tot = ws[0] + ws[1] + ws[2]
        comb = (ws[0] * outs[0] + ws[1] * outs[1] + ws[2] * outs[2]) / tot
        o_ref[0, :, h * A_HEAD_DIM:(h + 1) * A_HEAD_DIM] = comb


def attn_a_sample(qkv_s, caches, layer):
    db, t_new, width = qkv_s.shape
    hw = A_HEADS * A_HEAD_DIM
    in_specs = [pl.BlockSpec((1, t_new, width), lambda b: (b, 0, 0))]
    args = [qkv_s]
    for (win, dil), c in zip(A_GROUPS, caches):
        nl = c.shape[0]
        view = c.reshape(nl * db, win // dil, dil * 2 * hw)
        n_sub = min(dil, t_new)
        in_specs.append(pl.BlockSpec((1, win // dil, n_sub * 2 * hw),
                                     lambda b, layer=layer: (layer * db + b, 0, 0)))
        args.append(view)
    return pl.pallas_call(
        _attn_a_sample_kernel, grid=(db,), in_specs=in_specs,
        out_specs=pl.BlockSpec((1, t_new, hw), lambda b: (b, 0, 0)),
        out_shape=jax.ShapeDtypeStruct((db, t_new, hw), F32),
        compiler_params=_params(("parallel",)), name="attn_a_sample")(*args)


def _lambda(lp_ref, lam_init):
    lp = lp_ref[...]
    a = jnp.sum(lp[0:1, :] * lp[1:2, :], axis=-1, keepdims=True)
    b = jnp.sum(lp[2:3, :] * lp[3:4, :], axis=-1, keepdims=True)
    return jnp.exp(a) - jnp.exp(b) + lam_init


def _sub_ln(o, g, lam_init):
    on = o * lax.rsqrt(jnp.mean(o * o, axis=-1, keepdims=True) + SUBLN_EPS) * g
    return on * (1.0 - lam_init)


def _attn_b_kernel(q_ref, k_ref, v_ref, lp_ref, g_ref, o_ref, m_ref, l_ref, acc_ref, *, lam_init, tq, tk):
    qi = pl.program_id(2)
    ki = pl.program_id(3)
    nk = pl.num_programs(3)
    nt = (((1,), (1,)), ((), ()))
    sub = min(tq, 256)

    @pl.when(ki == 0)
    def _():
        m_ref[...] = jnp.full(m_ref.shape, -jnp.inf, F32)
        l_ref[...] = jnp.zeros(l_ref.shape, F32)
        acc_ref[...] = jnp.zeros(acc_ref.shape, F32)

    def step(masked):
        v = v_ref[...].astype(BF16)
        if masked:
            rows = qi * tq + lax.broadcasted_iota(jnp.int32, (tq, tk), 0)
            cols = ki * tk + lax.broadcasted_iota(jnp.int32, (tq, tk), 1)
            mask = cols <= rows
        for c in range(2):
            sl = slice(c * B_QK_DIM, (c + 1) * B_QK_DIM)
            kc = k_ref[:, sl].astype(BF16)
            for r in range(tq // sub):
                rs = slice(r * sub, (r + 1) * sub)
                s = lax.dot_general(q_ref[rs, sl], kc, nt, preferred_element_type=F32)
                if masked:
                    s = jnp.where(mask[rs, :], s, -jnp.inf)
                m_old = m_ref[c, rs, :]
                m_new = jnp.maximum(m_old, jnp.max(s, axis=-1, keepdims=True))
                alpha = jnp.exp(m_old - m_new)
                p = jnp.exp(s - m_new)
                l_ref[c, rs, :] = alpha * l_ref[c, rs, :] + jnp.sum(p, axis=-1, keepdims=True)
                acc_ref[c, rs, :] = (alpha * acc_ref[c, rs, :]
                                     + jnp.dot(p.astype(BF16), v, preferred_element_type=F32))
                m_ref[c, rs, :] = m_new

    first_row = qi * tq
    last_col = ki * tk + tk - 1

    @pl.when(last_col <= first_row)
    def _():
        step(False)

    @pl.when(jnp.logical_and(last_col > first_row, ki * tk <= first_row + tq - 1))
    def _():
        step(True)

    @pl.when(ki == nk - 1)
    def _():
        lam = _lambda(lp_ref, lam_init)
        o = acc_ref[0] / l_ref[0] - lam * (acc_ref[1] / l_ref[1])
        o_ref[...] = _sub_ln(o, g_ref[...], lam_init).astype(BF16)


def attn_b_prompt(q, kv, lp, g_sub, lam_init, batch, seq, *, tq=512, tk=512):
    tq = min(tq, seq)
    tk = min(tk, seq)
    nq, nk = seq // tq, seq // tk
    hd = 2 * B_QK_DIM

    def kmap(b, h, i, j):
        return (b * nk + jnp.minimum(j, (i * tq + tq - 1) // tk), h)

    def vmap_(b, h, i, j):
        return (b * nk + jnp.minimum(j, (i * tq + tq - 1) // tk), B_HEADS + h)

    kern = functools.partial(_attn_b_kernel, lam_init=lam_init, tq=tq, tk=tk)
    return pl.pallas_call(
        kern, grid=(batch, B_HEADS, nq, nk),
        in_specs=[pl.BlockSpec((tq, hd), lambda b, h, i, j: (b * nq + i, h)),
                  pl.BlockSpec((tk, hd), kmap),
                  pl.BlockSpec((tk, B_V_DIM), vmap_),
                  pl.BlockSpec((4, B_QK_DIM), lambda b, h, i, j: (0, 0)),
                  pl.BlockSpec((1, B_V_DIM), lambda b, h, i, j: (0, 0))],
        out_specs=pl.BlockSpec((tq, B_V_DIM), lambda b, h, i, j: (b * nq + i, h)),
        out_shape=jax.ShapeDtypeStruct((batch * seq, B_HEADS * B_V_DIM), BF16),
        scratch_shapes=[pltpu.VMEM((2, tq, 1), F32), pltpu.VMEM((2, tq, 1), F32),
                        pltpu.VMEM((2, tq, B_V_DIM), F32)],
        compiler_params=_params(("parallel", "parallel", "parallel", "arbitrary")),
        name="attn_b_prompt")(q, kv, kv, lp, g_sub.reshape(1, B_V_DIM))


def _attn_b_sample_kernel(*refs, lam_init, t_new, n_par):
    pt_ref, q_ref = refs[0], refs[1]
    page_refs = refs[2:2 + 2 * n_par]
    kvn_ref, lp_ref, g_ref, o_ref, m_ref, l_ref, acc_ref, xs_ref = refs[2 + 2 * n_par:]
    p = pl.program_id(1)
    n_steps = pl.num_programs(1)
    kw = B_HEADS * 2 * B_QK_DIM
    nt = (((1,), (1,)), ((), ()))
    rph = 2 * t_new

    @pl.when(p == 0)
    def _():
        m_ref[...] = jnp.full(m_ref.shape, -jnp.inf, F32)
        l_ref[...] = jnp.zeros(l_ref.shape, F32)
        acc_ref[...] = jnp.zeros(acc_ref.shape, F32)

    def update(state, s, weigh):
        m_old, l_old, acc_old = state
        m_new = jnp.maximum(m_old, jnp.max(s, axis=-1, keepdims=True))
        alpha = jnp.exp(m_old - m_new)
        e = jnp.exp(s - m_new)
        return m_new, alpha * l_old + jnp.sum(e, axis=-1, keepdims=True), alpha * acc_old + weigh(e)

    rows_per_page = 2 * PAGE_SIZE
    for h in range(B_HEADS):
        for half in range(2):
            for k in range(n_par):
                xs_ref[h, half, k * rows_per_page:(k + 1) * rows_per_page, :] = (
                    page_refs[2 * k + half][0, pl.ds(h, rows_per_page, stride=B_HEADS), :].astype(BF16))

    parts = []
    for h in range(B_HEADS):
        qh = q_ref[0, h * rph:(h + 1) * rph, :]
        parts.append(lax.dot_general(qh[:, 0:B_QK_DIM], xs_ref[h, 0], nt, preferred_element_type=F32)
                     + lax.dot_general(qh[:, B_QK_DIM:], xs_ref[h, 1], nt, preferred_element_type=F32))
    s = jnp.concatenate(parts, axis=0)
    is_key = jnp.bitwise_and(lax.broadcasted_iota(jnp.int32, s.shape, 1), 1) == 0
    s = jnp.where(is_key, s, -jnp.inf)

    def weigh(e):
        ev = pltpu.roll(e, 1, 1)
        outs = []
        for h in range(B_HEADS):
            evh = ev[h * rph:(h + 1) * rph, :].astype(BF16)
            outs.append(jnp.concatenate([jnp.dot(evh, xs_ref[h, 0], preferred_element_type=F32),
                                         jnp.dot(evh, xs_ref[h, 1], preferred_element_type=F32)], axis=1))
        return jnp.concatenate(outs, axis=0)

    m_new, l_new, acc_new = update((m_ref[...], l_ref[...], acc_ref[...]), s, weigh)
    m_ref[...] = m_new
    l_ref[...] = l_new
    acc_ref[...] = acc_new

    @pl.when(p == n_steps - 1)
    def _():
        lam = _lambda(lp_ref, lam_init)
        for h in range(B_HEADS):
            rs = slice(h * rph, (h + 1) * rph)
            qh = q_ref[0, rs, :]
            kn = kvn_ref[0, :, h * B_V_DIM:(h + 1) * B_V_DIM].astype(BF16)
            vn = kvn_ref[0, :, kw + h * B_V_DIM:kw + (h + 1) * B_V_DIM].astype(BF16)
            sn = lax.dot_general(qh, kn, nt, preferred_element_type=F32)
            rq = jnp.bitwise_and(lax.broadcasted_iota(jnp.int32, sn.shape, 0), t_new - 1)
            cj = lax.broadcasted_iota(jnp.int32, sn.shape, 1)
            _, l_fin, acc_fin = update(
                (m_ref[rs, :], l_ref[rs, :], acc_ref[rs, :]), jnp.where(cj <= rq, sn, -jnp.inf),
                lambda e, vn=vn: jnp.dot(e.astype(BF16), vn, preferred_element_type=F32))
            on = acc_fin / l_fin
            o = on[0:t_new, :] - lam * on[t_new:2 * t_new, :]
            o_ref[0, :, h * B_V_DIM:(h + 1) * B_V_DIM] = _sub_ln(o, g_ref[...], lam_init)


def attn_b_sample(q_s, kv_s, cache_b_kv, page_table, lp, g_sub, lam_init):
    db, t_new, qw = q_s.shape
    n_pages = page_table.shape[1]
    n_phys = cache_b_kv.shape[0]
    assert t_new & (t_new - 1) == 0
    n_par = 4 if n_pages % 4 == 0 else 1
    kvw = 2 * B_HEADS * B_V_DIM
    page_rows = PAGE_SIZE * 2 * B_HEADS
    pages = cache_b_kv.reshape(n_phys, page_rows, B_V_DIM)
    q5 = q_s.reshape(db, t_new, B_HEADS, 2, B_QK_DIM).transpose(0, 2, 3, 1, 4)
    eye = jnp.eye(2, dtype=q_s.dtype)
    qbd = q5[:, :, :, :, None, :] * eye[None, None, :, None, :, None]
    rows = B_HEADS * 2 * t_new
    qbd = qbd.reshape(db, rows, 2 * B_QK_DIM).astype(BF16)
    kern = functools.partial(_attn_b_sample_kernel, lam_init=lam_init, t_new=t_new, n_par=n_par)
    page_specs = [pl.BlockSpec((1, page_rows, B_QK_DIM),
                               lambda b, p, pt, k=k, half=half: (pt[b, p * n_par + k], 0, half))
                  for k in range(n_par) for half in range(2)]
    grid_spec = pltpu.PrefetchScalarGridSpec(
        num_scalar_prefetch=1, grid=(db, n_pages // n_par),
        in_specs=[pl.BlockSpec((1, rows, 2 * B_QK_DIM), lambda b, p, pt: (b, 0, 0))] + page_specs + [
                  pl.BlockSpec((1, t_new, kvw), lambda b, p, pt: (b, 0, 0)),
                  pl.BlockSpec((4, B_QK_DIM), lambda b, p, pt: (0, 0)),
                  pl.BlockSpec((1, B_V_DIM), lambda b, p, pt: (0, 0))],
        out_specs=pl.BlockSpec((1, t_new, qw), lambda b, p, pt: (b, 0, 0)),
        scratch_shapes=[pltpu.VMEM((rows, 1), F32), pltpu.VMEM((rows, 1), F32),
                        pltpu.VMEM((rows, B_V_DIM), F32),
                        pltpu.VMEM((B_HEADS, 2, n_par * 2 * PAGE_SIZE, B_QK_DIM), BF16)])
    return pl.pallas_call(
        kern, grid_spec=grid_spec,
        out_shape=jax.ShapeDtypeStruct((db, t_new, qw), F32),
        compiler_params=_params(("parallel", "arbitrary")),
        name="attn_b_sample")(page_table, qbd, *([pages] * (2 * n_par)), kv_s, lp, g_sub.reshape(1, B_V_DIM))


def _top_ranks(s, exact):
    iota = lax.broadcasted_iota(jnp.int32, s.shape, 0).astype(F32)
    rank = jnp.full(s.shape, NOT_SELECTED, F32)
    vals = []
    work = s
    for a in range(PEER_TOPK):
        m = jnp.max(work, axis=0, keepdims=True)
        if exact:
            idx = jnp.min(jnp.where(work == m, iota, float(N_KEYS)), axis=0, keepdims=True)
            hit = iota == idx
        else:
            hit = work == m
        rank = jnp.where(hit, float(a), rank)
        work = jnp.where(hit, -jnp.inf, work)
        vals.append(m)
    taken = jnp.sum(jnp.where(rank < NOT_SELECTED, 1.0, 0.0), axis=0, keepdims=True)
    return vals, rank, taken


def _candidate_rows():
    pieces = [(0, 0, 16, 16)]
    for a in range(1, 8):
        pieces.append((a, 0, 8, PEER_TOPK // (a + 1)))
    return pieces


def _route_kernel(ht_ref, wq_ref, sk_ref, r1_ref, e1_ref, bq_out_ref, c0_out_ref, qt_ref, bq_ref, c0_ref):
    n_chunks = ht_ref.shape[1] // LANES
    qt_ref[...] = jnp.dot(wq_ref[...], ht_ref[...], preferred_element_type=F32)
    neg = -jnp.inf
    t = LANES

    def route_chunk(h, cs, scores, exact):
        vals, ranks = [], []
        off = jnp.zeros((1, t), F32)
        for c in range(2):
            v, r, taken = _top_ranks(scores[c], exact)
            vals.append(v); ranks.append(r)
            off = off + jnp.abs(taken - float(PEER_TOPK))
        v0, v1 = vals
        v1_16 = jnp.concatenate(v1, axis=0)
        v0_hi = jnp.concatenate(v0[8:16], axis=0)
        cands, flats = [], []
        for a, _, rows, nvalid in _candidate_rows():
            b_iota = lax.broadcasted_iota(jnp.int32, (rows, t), 0)
            cs_ab = v0[a] + v1_16[0:rows, :]
            cands.append(jnp.where(b_iota < nvalid, cs_ab, neg))
            flats.append((b_iota + a * PEER_TOPK).astype(F32))
        cands.append(v0_hi + v1[0])
        flats.append(((lax.broadcasted_iota(jnp.int32, (8, t), 0) + 8) * PEER_TOPK).astype(F32))
        cand = jnp.concatenate(cands, axis=0)
        flat = jnp.concatenate(flats, axis=0)
        big = float(PEER_TOPK * PEER_TOPK)
        work = cand
        sel = jnp.zeros(cand.shape, F32)
        for _ in range(PEER_TOPK):
            m = jnp.max(work, axis=0, keepdims=True)
            if exact:
                idx = jnp.min(jnp.where(work == m, flat, big), axis=0, keepdims=True)
                hit = flat == idx
            else:
                hit = work == m
            sel = jnp.where(hit, 1.0, sel)
            work = jnp.where(hit, neg, work)
        off = off + jnp.abs(jnp.sum(sel, axis=0, keepdims=True) - float(PEER_TOPK))
        top = v0[0] + v1[0]
        z = jnp.sum(jnp.where(sel > 0.0, jnp.exp(cand - top), 0.0), axis=0, keepdims=True)
        counts = [jnp.sum(sel[0:16, :], axis=0, keepdims=True)]
        for k in range(1, 8):
            counts.append(jnp.sum(sel[8 + 8 * k:16 + 8 * k, :], axis=0, keepdims=True))
        hi = sel[72:80, :]
        bq = jnp.zeros(ranks[0].shape, F32)
        for a in range(PEER_TOPK):
            cnt = counts[a] if a < 8 else hi[a - 8:a - 7, :]
            bq = jnp.where(ranks[0] == float(a), cnt, bq)
        rs = pl.ds(pl.multiple_of(h * N_KEYS, N_KEYS), N_KEYS)
        bq_ref[rs, cs] = bq
        c0_ref[rs, cs] = jnp.exp(scores[0] - v0[0]) / z
        r1_ref[rs, cs] = ranks[1].astype(BF16)
        e1_ref[rs, cs] = jnp.exp(scores[1] - v1[0]).astype(BF16)
        return off

    def body(h, carry):
        work = []
        for k in range(n_chunks):
            cs = slice(k * LANES, (k + 1) * LANES)
            scores = []
            for c in range(2):
                hc = h * 2 + c
                qhc = qt_ref[pl.ds(pl.multiple_of(hc * N_KEYS, N_KEYS), N_KEYS), cs].astype(BF16)
                scores.append(jnp.dot(sk_ref[hc].astype(BF16), qhc, preferred_element_type=F32))
            work.append((cs, scores, route_chunk(h, cs, scores, exact=False)))
        for cs, scores, off in work:
            @pl.when(jnp.max(off) > 0.0)
            def _(cs=cs, scores=scores):
                route_chunk(h, cs, scores, exact=True)

        return carry

    lax.fori_loop(0, PEER_HEADS, body, 0)
    for h in range(PEER_HEADS):
        bq_out_ref[:, h, :] = bq_ref[h * N_KEYS:(h + 1) * N_KEYS, :]
        c0_out_ref[:, h, :] = c0_ref[h * N_KEYS:(h + 1) * N_KEYS, :]


def peer_route(ht, wq_t, subkeys, *, tm=512):
    d, n = ht.shape
    rows = PEER_HEADS * N_KEYS
    slab = pl.BlockSpec((rows, tm), lambda i: (0, i))
    sds = jax.ShapeDtypeStruct((rows, n), BF16)
    slab3 = pl.BlockSpec((N_KEYS, PEER_HEADS, tm), lambda i: (0, 0, i))
    sds3 = jax.ShapeDtypeStruct((N_KEYS, PEER_HEADS, n), F32)
    return pl.pallas_call(
        _route_kernel, grid=(n // tm,),
        in_specs=[pl.BlockSpec((d, tm), lambda i: (0, i)),
                  pl.BlockSpec(wq_t.shape, lambda i: (0, 0)),
                  pl.BlockSpec(subkeys.shape, lambda i: (0, 0, 0))],
        out_specs=[slab, slab, slab3, slab3], out_shape=[sds, sds, sds3, sds3],
        scratch_shapes=[pltpu.VMEM((wq_t.shape[0], tm), F32), pltpu.VMEM((rows, tm), F32),
                        pltpu.VMEM((rows, tm), F32)],
        compiler_params=_params(("parallel",)), name="peer_route")(ht, wq_t, subkeys)


def _gelu(a):
    return 0.5 * a * (1.0 + lax.erf(a * math.sqrt(0.5)))


def _expert_kernel(x_ref, ht_ref, u_ref, v_ref, r1_ref, e1_ref, bq_ref, c0_ref, o_ref, a_ref, *, te, tm, ge):
    e = pl.program_id(1)
    rows_per_group = ge // N_KEYS
    n_chunks = tm // LANES
    packed = 2 * SUBLANES
    tiles = N_KEYS // packed

    @pl.when(e == 0)
    def _():
        o_ref[...] = x_ref[...]

    ht = ht_ref[...]
    n_groups = te // ge

    def scores(k):
        a_ref[k] = jnp.dot(u_ref[k * ge:(k + 1) * ge, :], ht, preferred_element_type=F32)

    scores(0)
    for k in range(n_groups):
        if k + 1 < n_groups:
            scores(k + 1)
        rows = []
        for r in range(rows_per_group):
            i = e * (te // N_KEYS) + k * rows_per_group + r
            cols = []
            for c in range(n_chunks):
                cs = slice(c * LANES, (c + 1) * LANES)
                bq_all = bq_ref[i, :, cs]
                c0_all = c0_ref[i, :, cs]
                w = [jnp.zeros((packed, LANES), BF16)] * tiles
                for h in range(PEER_HEADS):
                    bq = jnp.broadcast_to(bq_all[h:h + 1, :], (packed, LANES)).astype(BF16)
                    c0 = jnp.broadcast_to(c0_all[h:h + 1, :], (packed, LANES)).astype(BF16)
                    for t in range(tiles):
                        js = slice(h * N_KEYS + t * packed, h * N_KEYS + (t + 1) * packed)
                        e1 = e1_ref[js, cs]
                        w[t] = w[t] + jnp.where(r1_ref[js, cs] < bq, e1 * c0, jnp.zeros_like(e1))
                gate = jnp.concatenate(w, axis=0).astype(F32)
                cols.append(_gelu(a_ref[k, r * N_KEYS:(r + 1) * N_KEYS, cs]) * gate)
            rows.append(jnp.concatenate(cols, axis=1))
        gt = jnp.concatenate(rows, axis=0)
        g = gt.T.astype(BF16)
        o_ref[...] += jnp.dot(g, v_ref[k * ge:(k + 1) * ge, :], preferred_element_type=F32)


def peer_experts(x, ht, u_bf, v_bf, slabs, *, tm=512, te=1024, ge=256):
    d, n = ht.shape
    n_exp = u_bf.shape[0]
    while n % tm:
        tm //= 2
    te = min(te, n_exp)
    rows = PEER_HEADS * N_KEYS
    once = pl.Buffered(1)
    slab = pl.BlockSpec((rows, tm), lambda i, e: (0, i), pipeline_mode=once)
    slab3 = pl.BlockSpec((N_KEYS, PEER_HEADS, tm), lambda i, e: (0, 0, i), pipeline_mode=once)
    kern = functools.partial(_expert_kernel, te=te, tm=tm, ge=ge)
    return pl.pallas_call(
        kern, grid=(n // tm, n_exp // te),
        in_specs=[pl.BlockSpec((tm, d), lambda i, e: (i, 0), pipeline_mode=once),
                  pl.BlockSpec((d, tm), lambda i, e: (0, i), pipeline_mode=once),
                  pl.BlockSpec((te, d), lambda i, e: (e, 0)),
                  pl.BlockSpec((te, d), lambda i, e: (e, 0)),
                  slab, slab, slab3, slab3],
        out_specs=pl.BlockSpec((tm, d), lambda i, e: (i, 0)),
        out_shape=jax.ShapeDtypeStruct((n, d), F32),
        scratch_shapes=[pltpu.VMEM((te // ge, ge, tm), F32)],
        compiler_params=_params(("parallel", "arbitrary")),
        name="peer_experts")(x, ht, u_bf, v_bf, *slabs)


def _rope_table(pos):
    rot = A_HEAD_DIM // 4
    half = rot // 2
    inv_freq = ROPE_THETA ** (-jnp.arange(half, dtype=F32) / half)
    ang = pos.astype(F32)[:, None] * inv_freq[None, :]
    cos, sin = jnp.cos(ang), jnp.sin(ang)
    n = pos.shape[0]
    ones = jnp.ones((n, A_HEAD_DIM - rot), F32)
    zeros = jnp.zeros((n, A_HEAD_DIM - rot), F32)
    zh = jnp.zeros((n, half), F32)
    return jnp.concatenate([cos, cos, ones, -sin, zh, zeros, zh, sin, zeros], axis=1)


def kernel(x_prompt, x_sample, cache_a_w128, cache_a_w512, cache_a_w2048, cache_b_kv, page_table,
           p_prompt, p_sample, norm_mix, norm_ffn, norm_ple, norm_kv, norm_final,
           w_qkv_a, w_o_a, w_kv_b, w_q_b, diff_lambda, norm_sub_b, w_o_b,
           peer_wq, peer_subkeys, peer_u, peer_v, w_ple, w_ple_gate):
    batch, seq, d = x_prompt.shape
    db, t_new, _ = x_sample.shape
    depth = norm_mix.shape[0]
    n_a = w_qkv_a.shape[0]
    past_len = page_table.shape[1] * PAGE_SIZE
    n_p, n_s = batch * seq, db * t_new
    n_tot = -(-(n_p + n_s) // TOKEN_PAD) * TOKEN_PAD
    pad = n_tot - n_p - n_s
    a_caches = (cache_a_w128, cache_a_w512, cache_a_w2048)
    hw = A_HEADS * A_HEAD_DIM

    def tokens(prompt_part, sample_part):
        w = prompt_part.shape[-1]
        return jnp.concatenate([prompt_part.reshape(n_p, w), sample_part.reshape(n_s, w),
                                jnp.zeros((pad, w), prompt_part.dtype)], axis=0)

    x = tokens(x_prompt, x_sample)
    pos = jnp.concatenate([jnp.tile(jnp.arange(seq, dtype=jnp.int32), batch),
                           jnp.tile(past_len + jnp.arange(t_new, dtype=jnp.int32), db),
                           jnp.zeros((pad,), jnp.int32)])
    tab = _rope_table(pos)
    u_bf = peer_u.astype(BF16)
    v_bf = peer_v.astype(BF16)

    a_rows_p = [[] for _ in A_GROUPS]
    a_rows_s = [[] for _ in A_GROUPS]
    new_b_kv_prompt = new_b_kv_sample = kv = None
    for i in range(depth):
        h = rms_norm(x, norm_mix[i], out_h=True)[0]
        if i < n_a:
            qkv = matmul(h, w_qkv_a[i], mode="rope", tab=tab, tn=hw,
                         rope_fn=lambda col: (col // hw) % 3 != 2)
            o_p = attn_a_prompt(qkv, batch, seq)
            qkv_s = qkv[n_p:n_p + n_s].reshape(db, t_new, qkv.shape[1])
            o_s = attn_a_sample(qkv_s, a_caches, i).reshape(n_s, hw).astype(BF16)
            o_all = jnp.concatenate([o_p, o_s, jnp.zeros((pad, hw), BF16)], axis=0)
            x = matmul(o_all, w_o_a[i], mode="res", res=x)
            for g, (win, dil) in enumerate(A_GROUPS):
                wb = min(win, seq)
                c0, c1 = (g * 3 + 1) * hw, (g * 3 + 3) * hw
                rows = jnp.stack([qkv[(b + 1) * seq - wb:(b + 1) * seq, c0:c1] for b in range(batch)], axis=0)
                a_rows_p[g].append(rows.reshape(batch, wb, 2, A_HEADS, A_HEAD_DIM))
                a_rows_s[g].append(qkv[n_p:n_p + n_s, c0:c1].reshape(db, t_new, 2, A_HEADS, A_HEAD_DIM))
        else:
            j = i - n_a
            if j == 0:
                hkv = rms_norm(x, norm_kv, out_h=True)[0]
                kw = B_HEADS * 2 * B_QK_DIM
                kv = matmul(hkv, w_kv_b, mode="rope", tab=tab, tn=1024, rope_fn=lambda col: col < kw)
                new_b_kv_prompt = kv[:n_p].reshape(batch, seq, 2, B_HEADS, B_V_DIM)
                new_b_kv_sample = kv[n_p:n_p + n_s].reshape(db, t_new, 2, B_HEADS, B_V_DIM)
            lam_init = 0.8 - 0.6 * math.exp(-0.3 * i)
            q = matmul(h, w_q_b[j], mode="rope", tab=tab, tn=1024, rope_fn=lambda col: col >= 0,
                       out_scale=1.0 / math.sqrt(B_QK_DIM), out_dtype=BF16)
            o_p = attn_b_prompt(q, kv, diff_lambda[j], norm_sub_b[j], lam_init, batch, seq)
            q_s = q[n_p:n_p + n_s].reshape(db, t_new, q.shape[1])
            kv_s = kv[n_p:n_p + n_s].reshape(db, t_new, kv.shape[1])
            o_s = attn_b_sample(q_s, kv_s, cache_b_kv, page_table, diff_lambda[j], norm_sub_b[j], lam_init)
            o_all = jnp.concatenate([o_p, o_s.reshape(n_s, -1).astype(BF16),
                                     jnp.zeros((pad, o_p.shape[1]), BF16)], axis=0)
            x = matmul(o_all, w_o_b[j], mode="res", res=x)
        ht = rms_norm(x, norm_ffn[i], out_ht=True)[0]
        wq_t = peer_wq[i].T.astype(BF16)
        sk = peer_subkeys[i].reshape(PEER_HEADS * 2, N_KEYS, -1)
        slabs = peer_route(ht, wq_t, sk)
        x = peer_experts(x, ht, u_bf[i], v_bf[i], slabs)
        hn = rms_norm(x, norm_ple[i], out_h=True)[0]
        x = ple(x, tokens(p_prompt[i], p_sample[i]), hn, w_ple[i], w_ple_gate[i])

    y = rms_norm(x, norm_final, out_y=True)[0]
    y_prompt = y[:n_p].reshape(batch, seq, d)
    y_sample = y[n_p:n_p + n_s].reshape(db, t_new, d)
    outs_p = [jnp.stack(r, axis=0) for r in a_rows_p]
    outs_s = [jnp.stack(r, axis=0) for r in a_rows_s]
    return (y_prompt, y_sample, *outs_p, *outs_s, new_b_kv_prompt, new_b_kv_sample)
```

```python
import functools
import math

import jax
import jax.numpy as jnp
from jax import lax
from jax.experimental import pallas as pl
from jax.experimental.pallas import tpu as pltpu

BF16 = jnp.bfloat16
F32 = jnp.float32

ROPE_THETA = 500000.0
NORM_EPS = 1e-6
SUBLN_EPS = 1e-5
A_GROUPS = ((128, 1), (512, 4), (2048, 16))
A_HEADS = 8
A_HEAD_DIM = 128
A_BLOCK = 128
B_HEADS = 8
B_QK_DIM = 128
B_V_DIM = 256
PEER_HEADS = 8
N_KEYS = 128
PEER_TOPK = 16
PAGE_SIZE = 128

LANES = 128
SUBLANES = 8
VMEM_LIMIT = 56 * 1024 * 1024
TOKEN_PAD = 512
MATMUL_ROW_CAP = 1100
NOT_SELECTED = 99.0


def _params(sem):
    return pltpu.CompilerParams(dimension_semantics=sem, vmem_limit_bytes=VMEM_LIMIT)


def _token_tile(n, cap):
    packed = 2 * SUBLANES
    for parts in range(1, n // packed + 1):
        if n % parts == 0 and (n // parts) % packed == 0 and n // parts <= cap:
            return n // parts
    raise ValueError(f"no token tile for {n} rows under {cap}")


def _rms_kernel(*refs, eps, has_add, out_x, out_h, out_ht, out_y):
    it = iter(refs)
    x_ref = next(it)
    add_ref = next(it) if has_add else None
    g_ref = next(it)
    x = x_ref[...]
    if has_add:
        x = x + add_ref[...].T
    if out_x:
        next(it)[...] = x
    y = x * lax.rsqrt(jnp.mean(x * x, axis=-1, keepdims=True) + eps) * g_ref[...]
    if out_h:
        next(it)[...] = y.astype(BF16)
    if out_ht:
        next(it)[...] = y.T.astype(BF16)
    if out_y:
        next(it)[...] = y


def rms_norm(x, g, *, add_t=None, out_x=False, out_h=False, out_ht=False, out_y=False,
             eps=NORM_EPS, tm=256):
    n, d = x.shape
    grid = (n // tm,)
    row = pl.BlockSpec((tm, d), lambda i: (i, 0))
    col = pl.BlockSpec((d, tm), lambda i: (0, i))
    in_specs = [row]
    args = [x]
    if add_t is not None:
        in_specs.append(col)
        args.append(add_t)
    in_specs.append(pl.BlockSpec((1, d), lambda i: (0, 0)))
    args.append(g.reshape(1, d))
    out_shape, out_specs = [], []
    if out_x:
        out_shape.append(jax.ShapeDtypeStruct((n, d), F32)); out_specs.append(row)
    if out_h:
        out_shape.append(jax.ShapeDtypeStruct((n, d), BF16)); out_specs.append(row)
    if out_ht:
        out_shape.append(jax.ShapeDtypeStruct((d, n), BF16)); out_specs.append(col)
    if out_y:
        out_shape.append(jax.ShapeDtypeStruct((n, d), F32)); out_specs.append(row)
    kern = functools.partial(_rms_kernel, eps=eps, has_add=add_t is not None, out_x=out_x,
                             out_h=out_h, out_ht=out_ht, out_y=out_y)
    return pl.pallas_call(kern, grid=grid, in_specs=in_specs, out_specs=out_specs,
                          out_shape=out_shape, compiler_params=_params(("parallel",)),
                          name="rms_norm")(*args)


def _rope_tile(y, tab):
    c = tab[:, 0:LANES]
    s1 = tab[:, LANES:2 * LANES]
    s2 = tab[:, 2 * LANES:3 * LANES]
    outs = []
    for g in range(y.shape[1] // LANES):
        yg = y[:, g * LANES:(g + 1) * LANES]
        outs.append(yg * c + pltpu.roll(yg, LANES - 16, 1) * s1 + pltpu.roll(yg, 16, 1) * s2)
    return jnp.concatenate(outs, axis=1) if len(outs) > 1 else outs[0]


def _mm_kernel(*refs, mode, rope_fn, tn, out_scale):
    if mode == "rope":
        x_ref, w_ref, tab_ref, o_ref, wb_ref = refs
    elif mode == "res":
        x_ref, w_ref, r_ref, o_ref, wb_ref = refs
    else:
        x_ref, w_ref, o_ref, wb_ref = refs
    j = pl.program_id(0)

    @pl.when(pl.program_id(1) == 0)
    def _():
        wb_ref[...] = w_ref[...].astype(BF16)

    y = jnp.dot(x_ref[...], wb_ref[...], preferred_element_type=F32)
    if mode == "rope":
        roped = rope_fn(j * tn)

        def finish(v):
            return (v if out_scale == 1.0 else v * out_scale).astype(o_ref.dtype)

        @pl.when(roped)
        def _():
            o_ref[...] = finish(_rope_tile(y, tab_ref[...]))

        @pl.when(jnp.logical_not(roped))
        def _():
            o_ref[...] = finish(y)
    elif mode == "res":
        o_ref[...] = r_ref[...] + y
    else:
        o_ref[...] = y


def matmul(x, w, *, mode="plain", tab=None, res=None, rope_fn=None, tm=MATMUL_ROW_CAP, tn=512,
           out_scale=1.0, out_dtype=F32):
    n, k = x.shape
    m = w.shape[1]
    tm = _token_tile(n, tm)
    tn = min(tn, m)
    grid = (m // tn, n // tm)
    in_specs = [pl.BlockSpec((tm, k), lambda j, i: (i, 0)),
                pl.BlockSpec((k, tn), lambda j, i: (0, j))]
    args = [x, w]
    if mode == "rope":
        in_specs.append(pl.BlockSpec((tm, 3 * LANES), lambda j, i: (i, 0)))
        args.append(tab)
    elif mode == "res":
        in_specs.append(pl.BlockSpec((tm, tn), lambda j, i: (i, j)))
        args.append(res)
    assert mode == "rope" or (out_scale == 1.0 and out_dtype == F32)
    kern = functools.partial(_mm_kernel, mode=mode, rope_fn=rope_fn, tn=tn, out_scale=out_scale)
    return pl.pallas_call(
        kern, grid=grid, in_specs=in_specs,
        out_specs=pl.BlockSpec((tm, tn), lambda j, i: (i, j)),
        out_shape=jax.ShapeDtypeStruct((n, m), out_dtype),
        scratch_shapes=[pltpu.VMEM((k, tn), BF16)],
        compiler_params=_params(("arbitrary", "arbitrary")),
        name="matmul_" + mode)(*args)


def _ple_kernel(x_ref, p_ref, hn_ref, wp_ref, wg_ref, o_ref, wgb_ref):
    @pl.when(pl.program_id(1) == 0)
    def _():
        wgb_ref[...] = wg_ref[...].astype(BF16)

    gate = jax.nn.sigmoid(jnp.dot(hn_ref[...], wgb_ref[...], preferred_element_type=F32))
    up = jnp.dot(p_ref[...].astype(BF16), wp_ref[...].astype(BF16), preferred_element_type=F32)
    o_ref[...] = x_ref[...] + up * gate


def ple(x, p, hn, w_p, w_gate, *, tm=MATMUL_ROW_CAP, tn=512):
    n, d = x.shape
    kp = p.shape[1]
    tn = min(tn, d)
    tm = _token_tile(n, tm)
    grid = (d // tn, n // tm)
    return pl.pallas_call(
        _ple_kernel, grid=grid,
        in_specs=[pl.BlockSpec((tm, tn), lambda j, i: (i, j)),
                  pl.BlockSpec((tm, kp), lambda j, i: (i, 0)),
                  pl.BlockSpec((tm, d), lambda j, i: (i, 0)),
                  pl.BlockSpec((kp, tn), lambda j, i: (0, j)),
                  pl.BlockSpec((d, tn), lambda j, i: (0, j))],
        out_specs=pl.BlockSpec((tm, tn), lambda j, i: (i, j)),
        out_shape=jax.ShapeDtypeStruct((n, d), F32),
        scratch_shapes=[pltpu.VMEM((d, tn), BF16)],
        compiler_params=_params(("arbitrary", "arbitrary")),
        name="ple")(x, p, hn, w_p, w_gate)


def _attn_a_kernel(*refs):
    n_in = 5 * len(A_GROUPS)
    o_ref, og_ref, lg_ref = refs[n_in:n_in + 3]
    blk = pl.program_id(1)
    t_blk = o_ref.shape[0]
    qi = lax.broadcasted_iota(jnp.int32, (A_BLOCK, A_BLOCK), 0)
    kj = lax.broadcasted_iota(jnp.int32, (A_BLOCK, A_BLOCK), 1)
    far = kj >= qi
    near = kj <= qi
    scale = 1.0 / math.sqrt(A_HEAD_DIM)
    nt = (((1,), (1,)), ((), ()))
    for g, (win, dil) in enumerate(A_GROUPS):
        q_ref, kc_ref, vc_ref, kp_ref, vp_ref = refs[5 * g:5 * g + 5]
        span = A_BLOCK * dil

        def body(it, carry, g=g, dil=dil, span=span, q_ref=q_ref, kc_ref=kc_ref, vc_ref=vc_ref,
                 kp_ref=kp_ref, vp_ref=vp_ref):
            n = it // dil
            r = it % dil
            start = n * span + r
            rows = pl.ds(start, A_BLOCK, stride=dil)
            before = pl.ds(jnp.maximum(start - span, 0), A_BLOCK, stride=dil)
            outside = pl.ds(r, A_BLOCK, stride=dil)
            first = n == 0
            q = q_ref[rows, :].astype(BF16)
            kc = kc_ref[rows, :].astype(BF16)
            vc = vc_ref[rows, :].astype(BF16)
            if span == t_blk:
                kp = kp_ref[outside, :].astype(BF16)
                vp = vp_ref[outside, :].astype(BF16)
            else:
                kp = jnp.where(first, kp_ref[outside, :], kc_ref[before, :]).astype(BF16)
                vp = jnp.where(first, vp_ref[outside, :], vc_ref[before, :]).astype(BF16)
            has_prev = jnp.logical_or(n > 0, blk > 0)
            sp = lax.dot_general(q, kp, nt, preferred_element_type=F32) * scale
            sc = lax.dot_general(q, kc, nt, preferred_element_type=F32) * scale
            sp = jnp.where(jnp.logical_and(far, has_prev), sp, -jnp.inf)
            sc = jnp.where(near, sc, -jnp.inf)
            m = jnp.maximum(jnp.max(sp, axis=-1, keepdims=True), jnp.max(sc, axis=-1, keepdims=True))
            ep = jnp.exp(sp - m)
            ec = jnp.exp(sc - m)
            den = jnp.sum(ep, axis=-1, keepdims=True) + jnp.sum(ec, axis=-1, keepdims=True)
            o = (jnp.dot(ep.astype(BF16), vp, preferred_element_type=F32)
                 + jnp.dot(ec.astype(BF16), vc, preferred_element_type=F32))
            og_ref[g, rows, :] = o / den
            lg_ref[g, rows, :] = jnp.broadcast_to(m + jnp.log(den), (A_BLOCK, A_HEAD_DIM))
            return carry

        lax.fori_loop(0, t_blk // A_BLOCK, body, 0, unroll=8)
    l0, l1, l2 = lg_ref[0], lg_ref[1], lg_ref[2]
    m = jnp.maximum(jnp.maximum(l0, l1), l2)
    e0, e1, e2 = jnp.exp(l0 - m), jnp.exp(l1 - m), jnp.exp(l2 - m)
    out = (e0 * og_ref[0] + e1 * og_ref[1] + e2 * og_ref[2]) / (e0 + e1 + e2)
    o_ref[...] = out.astype(BF16)


def attn_a_prompt(qkv, batch, seq):
    n_tot, width = qkv.shape
    hw = A_HEADS * A_HEAD_DIM
    t_blk = A_BLOCK * max(d for _, d in A_GROUPS)
    assert seq % t_blk == 0
    nblk = seq // t_blk
    in_specs, args = [], []
    for g, (win, dil) in enumerate(A_GROUPS):
        span = A_BLOCK * dil
        per_blk = t_blk // span
        for which in range(3):
            col = (g * 3 + which) * A_HEADS
            in_specs.append(pl.BlockSpec((t_blk, A_HEAD_DIM),
                                         lambda b, k, h, col=col: (b * nblk + k, col + h)))
            args.append(qkv)
        for which in (1, 2):
            col = (g * 3 + which) * A_HEADS
            in_specs.append(pl.BlockSpec(
                (span, A_HEAD_DIM),
                lambda b, k, h, col=col, per_blk=per_blk: (jnp.maximum((b * nblk + k) * per_blk - 1, 0), col + h)))
            args.append(qkv)
    scratch = pltpu.VMEM((len(A_GROUPS), t_blk, A_HEAD_DIM), F32)
    return pl.pallas_call(
        _attn_a_kernel, grid=(batch, nblk, A_HEADS), in_specs=in_specs,
        out_specs=pl.BlockSpec((t_blk, A_HEAD_DIM), lambda b, k, h: (b * nblk + k, h)),
        out_shape=jax.ShapeDtypeStruct((batch * seq, hw), BF16),
        scratch_shapes=[scratch, scratch],
        compiler_params=_params(("parallel", "parallel", "parallel")),
        name="attn_a_prompt")(*args)


def _attn_a_sample_kernel(qkv_ref, c0_ref, c1_ref, c2_ref, o_ref):
    t_new = qkv_ref.shape[1]
    hw = A_HEADS * A_HEAD_DIM
    scale = 1.0 / math.sqrt(A_HEAD_DIM)
    nt = (((1,), (1,)), ((), ()))
    caches = (c0_ref, c1_ref, c2_ref)
    ti = lax.broadcasted_iota(jnp.int32, (t_new, A_BLOCK), 0)
    ni = lax.broadcasted_iota(jnp.int32, (t_new, A_BLOCK), 1)
    tq = lax.broadcasted_iota(jnp.int32, (t_new, t_new), 0)
    tj = lax.broadcasted_iota(jnp.int32, (t_new, t_new), 1)
    for h in range(A_HEADS):
        outs, lses = [], []
        for g, (win, dil) in enumerate(A_GROUPS):
            base = g * 3 * hw + h * A_HEAD_DIM
            q = qkv_ref[0, :, base:base + A_HEAD_DIM].astype(BF16)
            kn = qkv_ref[0, :, base + hw:base + hw + A_HEAD_DIM].astype(BF16)
            vn = qkv_ref[0, :, base + 2 * hw:base + 2 * hw + A_HEAD_DIM].astype(BF16)
            n_sub = min(dil, t_new)
            s_new = lax.dot_general(q, kn, nt, preferred_element_type=F32) * scale
            ok_new = jnp.logical_and(tj <= tq, jnp.bitwise_and(tq - tj, dil - 1) == 0)
            s_new = jnp.where(ok_new, s_new, -jnp.inf)
            ss, vs = [], []
            for r in range(n_sub):
                kb = caches[g][0, :, r * 2 * hw + h * A_HEAD_DIM:r * 2 * hw + (h + 1) * A_HEAD_DIM]
                vb = caches[g][0, :, r * 2 * hw + hw + h * A_HEAD_DIM:r * 2 * hw + hw + (h + 1) * A_HEAD_DIM]
                s = lax.dot_general(q, kb.astype(BF16), nt, preferred_element_type=F32) * scale
                ok = jnp.logical_and(jnp.bitwise_and(ti, dil - 1) == r, ni * dil + r >= ti)
                ss.append(jnp.where(ok, s, -jnp.inf))
                vs.append(vb.astype(BF16))
            m = jnp.max(s_new, axis=-1, keepdims=True)
            for s in ss:
                m = jnp.maximum(m, jnp.max(s, axis=-1, keepdims=True))
            e_new = jnp.exp(s_new - m)
            den = jnp.sum(e_new, axis=-1, keepdims=True)
            o = jnp.dot(e_new.astype(BF16), vn, preferred_element_type=F32)
            for s, vb in zip(ss, vs):
                e = jnp.exp(s - m)
                den = den + jnp.sum(e, axis=-1, keepdims=True)
                o = o + jnp.dot(e.astype(BF16), vb, preferred_element_type=F32)
            outs.append(o / den)
            lses.append(m + jnp.log(den))
        lm = jnp.maximum(jnp.maximum(lses[0], lses[1]), lses[2])
        ws = [jnp.exp(l - lm) for l in lses]
        tot = ws[0] + ws[1] + ws[2]
        comb = (ws[0] * outs[0] + ws[1] * outs[1] + ws[2] * outs[2]) / tot
        o_ref[0, :, h * A_HEAD_DIM:(h + 1) * A_HEAD_DIM] = comb


def attn_a_sample(qkv_s, caches, layer):
    db, t_new, width = qkv_s.shape
    hw = A_HEADS * A_HEAD_DIM
    in_specs = [pl.BlockSpec((1, t_new, width), lambda b: (b, 0, 0))]
    args = [qkv_s]
    for (win, dil), c in zip(A_GROUPS, caches):
        nl = c.shape[0]
        view = c.reshape(nl * db, win // dil, dil * 2 * hw)
        n_sub = min(dil, t_new)
        in_specs.append(pl.BlockSpec((1, win // dil, n_sub * 2 * hw),
                                     lambda b, layer=layer: (layer * db + b, 0, 0)))
        args.append(view)
    return pl.pallas_call(
        _attn_a_sample_kernel, grid=(db,), in_specs=in_specs,
        out_specs=pl.BlockSpec((1, t_new, hw), lambda b: (b, 0, 0)),
        out_shape=jax.ShapeDtypeStruct((db, t_new, hw), F32),
        compiler_params=_params(("parallel",)), name="attn_a_sample")(*args)


def _lambda(lp_ref, lam_init):
    lp = lp_ref[...]
    a = jnp.sum(lp[0:1, :] * lp[1:2, :], axis=-1, keepdims=True)
    b = jnp.sum(lp[2:3, :] * lp[3:4, :], axis=-1, keepdims=True)
    return jnp.exp(a) - jnp.exp(b) + lam_init


def _sub_ln(o, g, lam_init):
    on = o * lax.rsqrt(jnp.mean(o * o, axis=-1, keepdims=True) + SUBLN_EPS) * g
    return on * (1.0 - lam_init)


def _attn_b_kernel(q_ref, k_ref, v_ref, lp_ref, g_ref, o_ref, m_ref, l_ref, acc_ref, *, lam_init, tq, tk):
    qi = pl.program_id(2)
    ki = pl.program_id(3)
    nk = pl.num_programs(3)
    nt = (((1,), (1,)), ((), ()))
    sub = min(tq, 256)

    @pl.when(ki == 0)
    def _():
        m_ref[...] = jnp.full(m_ref.shape, -jnp.inf, F32)
        l_ref[...] = jnp.zeros(l_ref.shape, F32)
        acc_ref[...] = jnp.zeros(acc_ref.shape, F32)

    def step(masked):
        v = v_ref[...].astype(BF16)
        if masked:
            rows = qi * tq + lax.broadcasted_iota(jnp.int32, (tq, tk), 0)
            cols = ki * tk + lax.broadcasted_iota(jnp.int32, (tq, tk), 1)
            mask = cols <= rows
        for c in range(2):
            sl = slice(c * B_QK_DIM, (c + 1) * B_QK_DIM)
            kc = k_ref[:, sl].astype(BF16)
            for r in range(tq // sub):
                rs = slice(r * sub, (r + 1) * sub)
                s = lax.dot_general(q_ref[rs, sl], kc, nt, preferred_element_type=F32)
                if masked:
                    s = jnp.where(mask[rs, :], s, -jnp.inf)
                m_old = m_ref[c, rs, :]
                m_new = jnp.maximum(m_old, jnp.max(s, axis=-1, keepdims=True))
                alpha = jnp.exp(m_old - m_new)
                p = jnp.exp(s - m_new)
                l_ref[c, rs, :] = alpha * l_ref[c, rs, :] + jnp.sum(p, axis=-1, keepdims=True)
                acc_ref[c, rs, :] = (alpha * acc_ref[c, rs, :]
                                     + jnp.dot(p.astype(BF16), v, preferred_element_type=F32))
                m_ref[c, rs, :] = m_new

    first_row = qi * tq
    last_col = ki * tk + tk - 1

    @pl.when(last_col <= first_row)
    def _():
        step(False)

    @pl.when(jnp.logical_and(last_col > first_row, ki * tk <= first_row + tq - 1))
    def _():
        step(True)

    @pl.when(ki == nk - 1)
    def _():
        lam = _lambda(lp_ref, lam_init)
        o = acc_ref[0] / l_ref[0] - lam * (acc_ref[1] / l_ref[1])
        o_ref[...] = _sub_ln(o, g_ref[...], lam_init).astype(BF16)


def attn_b_prompt(q, kv, lp, g_sub, lam_init, batch, seq, *, tq=512, tk=512):
    tq = min(tq, seq)
    tk = min(tk, seq)
    nq, nk = seq // tq, seq // tk
    hd = 2 * B_QK_DIM

    def kmap(b, h, i, j):
        return (b * nk + jnp.minimum(j, (i * tq + tq - 1) // tk), h)

    def vmap_(b, h, i, j):
        return (b * nk + jnp.minimum(j, (i * tq + tq - 1) // tk), B_HEADS + h)

    kern = functools.partial(_attn_b_kernel, lam_init=lam_init, tq=tq, tk=tk)
    return pl.pallas_call(
        kern, grid=(batch, B_HEADS, nq, nk),
        in_specs=[pl.BlockSpec((tq, hd), lambda b, h, i, j: (b * nq + i, h)),
                  pl.BlockSpec((tk, hd), kmap),
                  pl.BlockSpec((tk, B_V_DIM), vmap_),
                  pl.BlockSpec((4, B_QK_DIM), lambda b, h, i, j: (0, 0)),
                  pl.BlockSpec((1, B_V_DIM), lambda b, h, i, j: (0, 0))],
        out_specs=pl.BlockSpec((tq, B_V_DIM), lambda b, h, i, j: (b * nq + i, h)),
        out_shape=jax.ShapeDtypeStruct((batch * seq, B_HEADS * B_V_DIM), BF16),
        scratch_shapes=[pltpu.VMEM((2, tq, 1), F32), pltpu.VMEM((2, tq, 1), F32),
                        pltpu.VMEM((2, tq, B_V_DIM), F32)],
        compiler_params=_params(("parallel", "parallel", "parallel", "arbitrary")),
        name="attn_b_prompt")(q, kv, kv, lp, g_sub.reshape(1, B_V_DIM))


def _attn_b_sample_kernel(*refs, lam_init, t_new, n_par):
    pt_ref, q_ref = refs[0], refs[1]
    page_refs = refs[2:2 + 2 * n_par]
    kvn_ref, lp_ref, g_ref, o_ref, m_ref, l_ref, acc_ref, xs_ref = refs[2 + 2 * n_par:]
    p = pl.program_id(1)
    n_steps = pl.num_programs(1)
    kw = B_HEADS * 2 * B_QK_DIM
    nt = (((1,), (1,)), ((), ()))
    rph = 2 * t_new

    @pl.when(p == 0)
    def _():
        m_ref[...] = jnp.full(m_ref.shape, -jnp.inf, F32)
        l_ref[...] = jnp.zeros(l_ref.shape, F32)
        acc_ref[...] = jnp.zeros(acc_ref.shape, F32)

    def update(state, s, weigh):
        m_old, l_old, acc_old = state
        m_new = jnp.maximum(m_old, jnp.max(s, axis=-1, keepdims=True))
        alpha = jnp.exp(m_old - m_new)
        e = jnp.exp(s - m_new)
        return m_new, alpha * l_old + jnp.sum(e, axis=-1, keepdims=True), alpha * acc_old + weigh(e)

    half_heads = B_HEADS // 2
    rows_per_page = 4 * PAGE_SIZE
    for g in range(half_heads):
        for half in range(2):
            for k in range(n_par):
                xs_ref[g, half, k * rows_per_page:(k + 1) * rows_per_page, :] = (
                    page_refs[2 * k + half][0, pl.ds(g, rows_per_page, stride=half_heads), :].astype(BF16))

    def pair_rows(x, g):
        return jnp.concatenate([x[g * rph:(g + 1) * rph], x[(g + half_heads) * rph:(g + half_heads + 1) * rph]],
                               axis=0)

    q_all = q_ref[0]
    lo, hi = [None] * half_heads, [None] * half_heads
    for g in range(half_heads):
        qg = pair_rows(q_all, g)
        sg = (lax.dot_general(qg[:, 0:B_QK_DIM], xs_ref[g, 0], nt, preferred_element_type=F32)
              + lax.dot_general(qg[:, B_QK_DIM:], xs_ref[g, 1], nt, preferred_element_type=F32))
        lo[g], hi[g] = sg[0:rph], sg[rph:2 * rph]
    s = jnp.concatenate(lo + hi, axis=0)
    kind = jnp.bitwise_and(lax.broadcasted_iota(jnp.int32, s.shape, 1), 3)
    own = (lax.broadcasted_iota(jnp.int32, s.shape, 0) >= half_heads * rph).astype(jnp.int32)
    s = jnp.where(kind == own, s, -jnp.inf)

    def weigh(e):
        ev = pltpu.roll(e, 2, 1)
        lo_o, hi_o = [None] * half_heads, [None] * half_heads
        for g in range(half_heads):
            eg = pair_rows(ev, g).astype(BF16)
            og = jnp.concatenate([jnp.dot(eg, xs_ref[g, 0], preferred_element_type=F32),
                                  jnp.dot(eg, xs_ref[g, 1], preferred_element_type=F32)], axis=1)
            lo_o[g], hi_o[g] = og[0:rph], og[rph:2 * rph]
        return jnp.concatenate(lo_o + hi_o, axis=0)

    m_new, l_new, acc_new = update((m_ref[...], l_ref[...], acc_ref[...]), s, weigh)
    m_ref[...] = m_new
    l_ref[...] = l_new
    acc_ref[...] = acc_new

    @pl.when(p == n_steps - 1)
    def _():
        lam = _lambda(lp_ref, lam_init)
        for h in range(B_HEADS):
            rs = slice(h * rph, (h + 1) * rph)
            qh = q_ref[0, rs, :]
            kn = kvn_ref[0, :, h * B_V_DIM:(h + 1) * B_V_DIM].astype(BF16)
            vn = kvn_ref[0, :, kw + h * B_V_DIM:kw + (h + 1) * B_V_DIM].astype(BF16)
            sn = lax.dot_general(qh, kn, nt, preferred_element_type=F32)
            rq = jnp.bitwise_and(lax.broadcasted_iota(jnp.int32, sn.shape, 0), t_new - 1)
            cj = lax.broadcasted_iota(jnp.int32, sn.shape, 1)
            _, l_fin, acc_fin = update(
                (m_ref[rs, :], l_ref[rs, :], acc_ref[rs, :]), jnp.where(cj <= rq, sn, -jnp.inf),
                lambda e, vn=vn: jnp.dot(e.astype(BF16), vn, preferred_element_type=F32))
            on = acc_fin / l_fin
            o = on[0:t_new, :] - lam * on[t_new:2 * t_new, :]
            o_ref[0, :, h * B_V_DIM:(h + 1) * B_V_DIM] = _sub_ln(o, g_ref[...], lam_init)


def attn_b_sample(q_s, kv_s, cache_b_kv, page_table, lp, g_sub, lam_init):
    db, t_new, qw = q_s.shape
    n_pages = page_table.shape[1]
    n_phys = cache_b_kv.shape[0]
    assert t_new & (t_new - 1) == 0
    n_par = next(k for k in (8, 4, 2, 1) if n_pages % k == 0)
    kvw = 2 * B_HEADS * B_V_DIM
    page_rows = PAGE_SIZE * 2 * B_HEADS
    pages = cache_b_kv.reshape(n_phys, page_rows, B_V_DIM)
    q5 = q_s.reshape(db, t_new, B_HEADS, 2, B_QK_DIM).transpose(0, 2, 3, 1, 4)
    eye = jnp.eye(2, dtype=q_s.dtype)
    qbd = q5[:, :, :, :, None, :] * eye[None, None, :, None, :, None]
    rows = B_HEADS * 2 * t_new
    qbd = qbd.reshape(db, rows, 2 * B_QK_DIM).astype(BF16)
    kern = functools.partial(_attn_b_sample_kernel, lam_init=lam_init, t_new=t_new, n_par=n_par)
    page_specs = [pl.BlockSpec((1, page_rows, B_QK_DIM),
                               lambda b, p, pt, k=k, half=half: (pt[b, p * n_par + k], 0, half))
                  for k in range(n_par) for half in range(2)]
    grid_spec = pltpu.PrefetchScalarGridSpec(
        num_scalar_prefetch=1, grid=(db, n_pages // n_par),
        in_specs=[pl.BlockSpec((1, rows, 2 * B_QK_DIM), lambda b, p, pt: (b, 0, 0))] + page_specs + [
                  pl.BlockSpec((1, t_new, kvw), lambda b, p, pt: (b, 0, 0)),
                  pl.BlockSpec((4, B_QK_DIM), lambda b, p, pt: (0, 0)),
                  pl.BlockSpec((1, B_V_DIM), lambda b, p, pt: (0, 0))],
        out_specs=pl.BlockSpec((1, t_new, qw), lambda b, p, pt: (b, 0, 0)),
        scratch_shapes=[pltpu.VMEM((rows, 1), F32), pltpu.VMEM((rows, 1), F32),
                        pltpu.VMEM((rows, B_V_DIM), F32),
                        pltpu.VMEM((B_HEADS // 2, 2, n_par * 4 * PAGE_SIZE, B_QK_DIM), BF16)])
    return pl.pallas_call(
        kern, grid_spec=grid_spec,
        out_shape=jax.ShapeDtypeStruct((db, t_new, qw), F32),
        compiler_params=_params(("parallel", "arbitrary")),
        name="attn_b_sample")(page_table, qbd, *([pages] * (2 * n_par)), kv_s, lp, g_sub.reshape(1, B_V_DIM))


def _top_ranks(s, exact):
    iota = lax.broadcasted_iota(jnp.int32, s.shape, 0).astype(F32)
    rank = jnp.full(s.shape, NOT_SELECTED, F32)
    vals = []
    work = s
    for a in range(PEER_TOPK):
        m = jnp.max(work, axis=0, keepdims=True)
        if exact:
            idx = jnp.min(jnp.where(work == m, iota, float(N_KEYS)), axis=0, keepdims=True)
            hit = iota == idx
        else:
            hit = work == m
        rank = jnp.where(hit, float(a), rank)
        work = jnp.where(hit, -jnp.inf, work)
        vals.append(m)
    taken = jnp.sum(jnp.where(rank < NOT_SELECTED, 1.0, 0.0), axis=0, keepdims=True)
    return vals, rank, taken


def _candidate_rows():
    pieces = [(0, 0, 16, 16)]
    for a in range(1, 8):
        pieces.append((a, 0, 8, PEER_TOPK // (a + 1)))
    return pieces


def _route_kernel(ht_ref, wq_ref, sk_ref, r1_ref, e1_ref, bq_out_ref, c0_out_ref, qt_ref, bq_ref, c0_ref):
    n_chunks = ht_ref.shape[1] // LANES
    qt_ref[...] = jnp.dot(wq_ref[...], ht_ref[...], preferred_element_type=F32)
    neg = -jnp.inf
    t = LANES

    def route_chunk(h, cs, scores, exact):
        vals, ranks = [], []
        off = jnp.zeros((1, t), F32)
        for c in range(2):
            v, r, taken = _top_ranks(scores[c], exact)
            vals.append(v); ranks.append(r)
            off = off + jnp.abs(taken - float(PEER_TOPK))
        v0, v1 = vals
        v1_16 = jnp.concatenate(v1, axis=0)
        v0_hi = jnp.concatenate(v0[8:16], axis=0)
        cands, flats = [], []
        for a, _, rows, nvalid in _candidate_rows():
            b_iota = lax.broadcasted_iota(jnp.int32, (rows, t), 0)
            cs_ab = v0[a] + v1_16[0:rows, :]
            cands.append(jnp.where(b_iota < nvalid, cs_ab, neg))
            flats.append((b_iota + a * PEER_TOPK).astype(F32))
        cands.append(v0_hi + v1[0])
        flats.append(((lax.broadcasted_iota(jnp.int32, (8, t), 0) + 8) * PEER_TOPK).astype(F32))
        cand = jnp.concatenate(cands, axis=0)
        flat = jnp.concatenate(flats, axis=0)
        big = float(PEER_TOPK * PEER_TOPK)
        work = cand
        sel = jnp.zeros(cand.shape, F32)
        for _ in range(PEER_TOPK):
            m = jnp.max(work, axis=0, keepdims=True)
            if exact:
                idx = jnp.min(jnp.where(work == m, flat, big), axis=0, keepdims=True)
                hit = flat == idx
            else:
                hit = work == m
            sel = jnp.where(hit, 1.0, sel)
            work = jnp.where(hit, neg, work)
        off = off + jnp.abs(jnp.sum(sel, axis=0, keepdims=True) - float(PEER_TOPK))
        top = v0[0] + v1[0]
        z = jnp.sum(jnp.where(sel > 0.0, jnp.exp(cand - top), 0.0), axis=0, keepdims=True)
        counts = [jnp.sum(sel[0:16, :], axis=0, keepdims=True)]
        for k in range(1, 8):
            counts.append(jnp.sum(sel[8 + 8 * k:16 + 8 * k, :], axis=0, keepdims=True))
        hi = sel[72:80, :]
        bq = jnp.zeros(ranks[0].shape, F32)
        for a in range(PEER_TOPK):
            cnt = counts[a] if a < 8 else hi[a - 8:a - 7, :]
            bq = jnp.where(ranks[0] == float(a), cnt, bq)
        rs = pl.ds(pl.multiple_of(h * N_KEYS, N_KEYS), N_KEYS)
        bq_ref[rs, cs] = bq
        c0_ref[rs, cs] = jnp.exp(scores[0] - v0[0]) / z
        r1_ref[rs, cs] = ranks[1].astype(BF16)
        e1_ref[rs, cs] = jnp.exp(scores[1] - v1[0]).astype(BF16)
        return off

    def body(h, carry):
        work = []
        for k in range(n_chunks):
            cs = slice(k * LANES, (k + 1) * LANES)
            scores = []
            for c in range(2):
                hc = h * 2 + c
                qhc = qt_ref[pl.ds(pl.multiple_of(hc * N_KEYS, N_KEYS), N_KEYS), cs].astype(BF16)
                scores.append(jnp.dot(sk_ref[hc].astype(BF16), qhc, preferred_element_type=F32))
            work.append((cs, scores, route_chunk(h, cs, scores, exact=False)))
        for cs, scores, off in work:
            @pl.when(jnp.max(off) > 0.0)
            def _(cs=cs, scores=scores):
                route_chunk(h, cs, scores, exact=True)

        return carry

    lax.fori_loop(0, PEER_HEADS, body, 0)
    for h in range(PEER_HEADS):
        bq_out_ref[:, h, :] = bq_ref[h * N_KEYS:(h + 1) * N_KEYS, :]
        c0_out_ref[:, h, :] = c0_ref[h * N_KEYS:(h + 1) * N_KEYS, :]


def peer_route(ht, wq_t, subkeys, *, tm=512):
    d, n = ht.shape
    rows = PEER_HEADS * N_KEYS
    slab = pl.BlockSpec((rows, tm), lambda i: (0, i))
    sds = jax.ShapeDtypeStruct((rows, n), BF16)
    slab3 = pl.BlockSpec((N_KEYS, PEER_HEADS, tm), lambda i: (0, 0, i))
    sds3 = jax.ShapeDtypeStruct((N_KEYS, PEER_HEADS, n), F32)
    return pl.pallas_call(
        _route_kernel, grid=(n // tm,),
        in_specs=[pl.BlockSpec((d, tm), lambda i: (0, i)),
                  pl.BlockSpec(wq_t.shape, lambda i: (0, 0)),
                  pl.BlockSpec(subkeys.shape, lambda i: (0, 0, 0))],
        out_specs=[slab, slab, slab3, slab3], out_shape=[sds, sds, sds3, sds3],
        scratch_shapes=[pltpu.VMEM((wq_t.shape[0], tm), F32), pltpu.VMEM((rows, tm), F32),
                        pltpu.VMEM((rows, tm), F32)],
        compiler_params=_params(("parallel",)), name="peer_route")(ht, wq_t, subkeys)


def _gelu(a):
    return 0.5 * a * (1.0 + lax.erf(a * math.sqrt(0.5)))


def _expert_kernel(x_ref, ht_ref, u_ref, v_ref, r1_ref, e1_ref, bq_ref, c0_ref, o_ref, a_ref, *, te, tm, ge):
    e = pl.program_id(1)
    rows_per_group = ge // N_KEYS
    n_chunks = tm // LANES
    packed = 2 * SUBLANES
    tiles = N_KEYS // packed

    @pl.when(e == 0)
    def _():
        o_ref[...] = x_ref[...]

    ht = ht_ref[...]
    n_groups = te // ge

    def scores(k):
        a_ref[k] = jnp.dot(u_ref[k * ge:(k + 1) * ge, :], ht, preferred_element_type=F32)

    scores(0)
    for k in range(n_groups):
        if k + 1 < n_groups:
            scores(k + 1)
        rows = []
        for r in range(rows_per_group):
            i = e * (te // N_KEYS) + k * rows_per_group + r
            cols = []
            for c in range(n_chunks):
                cs = slice(c * LANES, (c + 1) * LANES)
                bq_all = bq_ref[i, :, cs]
                c0_all = c0_ref[i, :, cs]
                w = [jnp.zeros((packed, LANES), BF16)] * tiles
                for h in range(PEER_HEADS):
                    bq = jnp.broadcast_to(bq_all[h:h + 1, :], (packed, LANES)).astype(BF16)
                    c0 = jnp.broadcast_to(c0_all[h:h + 1, :], (packed, LANES)).astype(BF16)
                    for t in range(tiles):
                        js = slice(h * N_KEYS + t * packed, h * N_KEYS + (t + 1) * packed)
                        e1 = e1_ref[js, cs]
                        w[t] = w[t] + jnp.where(r1_ref[js, cs] < bq, e1 * c0, jnp.zeros_like(e1))
                gate = jnp.concatenate(w, axis=0).astype(F32)
                cols.append(_gelu(a_ref[k, r * N_KEYS:(r + 1) * N_KEYS, cs]) * gate)
            rows.append(jnp.concatenate(cols, axis=1))
        gt = jnp.concatenate(rows, axis=0)
        g = gt.T.astype(BF16)
        o_ref[...] += jnp.dot(g, v_ref[k * ge:(k + 1) * ge, :], preferred_element_type=F32)


def peer_experts(x, ht, u_bf, v_bf, slabs, *, tm=512, te=1024, ge=256):
    d, n = ht.shape
    n_exp = u_bf.shape[0]
    while n % tm:
        tm //= 2
    te = min(te, n_exp)
    rows = PEER_HEADS * N_KEYS
    once = pl.Buffered(1)
    slab = pl.BlockSpec((rows, tm), lambda i, e: (0, i), pipeline_mode=once)
    slab3 = pl.BlockSpec((N_KEYS, PEER_HEADS, tm), lambda i, e: (0, 0, i), pipeline_mode=once)
    kern = functools.partial(_expert_kernel, te=te, tm=tm, ge=ge)
    return pl.pallas_call(
        kern, grid=(n // tm, n_exp // te),
        in_specs=[pl.BlockSpec((tm, d), lambda i, e: (i, 0), pipeline_mode=once),
                  pl.BlockSpec((d, tm), lambda i, e: (0, i), pipeline_mode=once),
                  pl.BlockSpec((te, d), lambda i, e: (e, 0)),
                  pl.BlockSpec((te, d), lambda i, e: (e, 0)),
                  slab, slab, slab3, slab3],
        out_specs=pl.BlockSpec((tm, d), lambda i, e: (i, 0)),
        out_shape=jax.ShapeDtypeStruct((n, d), F32),
        scratch_shapes=[pltpu.VMEM((te // ge, ge, tm), F32)],
        compiler_params=_params(("parallel", "arbitrary")),
        name="peer_experts")(x, ht, u_bf, v_bf, *slabs)


def _rope_table(pos):
    rot = A_HEAD_DIM // 4
    half = rot // 2
    inv_freq = ROPE_THETA ** (-jnp.arange(half, dtype=F32) / half)
    ang = pos.astype(F32)[:, None] * inv_freq[None, :]
    cos, sin = jnp.cos(ang), jnp.sin(ang)
    n = pos.shape[0]
    ones = jnp.ones((n, A_HEAD_DIM - rot), F32)
    zeros = jnp.zeros((n, A_HEAD_DIM - rot), F32)
    zh = jnp.zeros((n, half), F32)
    return jnp.concatenate([cos, cos, ones, -sin, zh, zeros, zh, sin, zeros], axis=1)


def kernel(x_prompt, x_sample, cache_a_w128, cache_a_w512, cache_a_w2048, cache_b_kv, page_table,
           p_prompt, p_sample, norm_mix, norm_ffn, norm_ple, norm_kv, norm_final,
           w_qkv_a, w_o_a, w_kv_b, w_q_b, diff_lambda, norm_sub_b, w_o_b,
           peer_wq, peer_subkeys, peer_u, peer_v, w_ple, w_ple_gate):
    batch, seq, d = x_prompt.shape
    db, t_new, _ = x_sample.shape
    depth = norm_mix.shape[0]
    n_a = w_qkv_a.shape[0]
    past_len = page_table.shape[1] * PAGE_SIZE
    n_p, n_s = batch * seq, db * t_new
    n_tot = -(-(n_p + n_s) // TOKEN_PAD) * TOKEN_PAD
    pad = n_tot - n_p - n_s
    a_caches = (cache_a_w128, cache_a_w512, cache_a_w2048)
    hw = A_HEADS * A_HEAD_DIM

    def tokens(prompt_part, sample_part):
        w = prompt_part.shape[-1]
        return jnp.concatenate([prompt_part.reshape(n_p, w), sample_part.reshape(n_s, w),
                                jnp.zeros((pad, w), prompt_part.dtype)], axis=0)

    x = tokens(x_prompt, x_sample)
    pos = jnp.concatenate([jnp.tile(jnp.arange(seq, dtype=jnp.int32), batch),
                           jnp.tile(past_len + jnp.arange(t_new, dtype=jnp.int32), db),
                           jnp.zeros((pad,), jnp.int32)])
    tab = _rope_table(pos)
    u_bf = peer_u.astype(BF16)
    v_bf = peer_v.astype(BF16)

    a_rows_p = [[] for _ in A_GROUPS]
    a_rows_s = [[] for _ in A_GROUPS]
    new_b_kv_prompt = new_b_kv_sample = kv = None
    for i in range(depth):
        h = rms_norm(x, norm_mix[i], out_h=True)[0]
        if i < n_a:
            qkv = matmul(h, w_qkv_a[i], mode="rope", tab=tab, tn=hw,
                         rope_fn=lambda col: (col // hw) % 3 != 2)
            o_p = attn_a_prompt(qkv, batch, seq)
            qkv_s = qkv[n_p:n_p + n_s].reshape(db, t_new, qkv.shape[1])
            o_s = attn_a_sample(qkv_s, a_caches, i).reshape(n_s, hw).astype(BF16)
            o_all = jnp.concatenate([o_p, o_s, jnp.zeros((pad, hw), BF16)], axis=0)
            x = matmul(o_all, w_o_a[i], mode="res", res=x)
            for g, (win, dil) in enumerate(A_GROUPS):
                wb = min(win, seq)
                c0, c1 = (g * 3 + 1) * hw, (g * 3 + 3) * hw
                rows = jnp.stack([qkv[(b + 1) * seq - wb:(b + 1) * seq, c0:c1] for b in range(batch)], axis=0)
                a_rows_p[g].append(rows.reshape(batch, wb, 2, A_HEADS, A_HEAD_DIM))
                a_rows_s[g].append(qkv[n_p:n_p + n_s, c0:c1].reshape(db, t_new, 2, A_HEADS, A_HEAD_DIM))
        else:
            j = i - n_a
            if j == 0:
                hkv = rms_norm(x, norm_kv, out_h=True)[0]
                kw = B_HEADS * 2 * B_QK_DIM
                kv = matmul(hkv, w_kv_b, mode="rope", tab=tab, tn=1024, rope_fn=lambda col: col < kw)
                new_b_kv_prompt = kv[:n_p].reshape(batch, seq, 2, B_HEADS, B_V_DIM)
                new_b_kv_sample = kv[n_p:n_p + n_s].reshape(db, t_new, 2, B_HEADS, B_V_DIM)
            lam_init = 0.8 - 0.6 * math.exp(-0.3 * i)
            q = matmul(h, w_q_b[j], mode="rope", tab=tab, tn=1024, rope_fn=lambda col: col >= 0,
                       out_scale=1.0 / math.sqrt(B_QK_DIM), out_dtype=BF16)
            o_p = attn_b_prompt(q, kv, diff_lambda[j], norm_sub_b[j], lam_init, batch, seq)
            q_s = q[n_p:n_p + n_s].reshape(db, t_new, q.shape[1])
            kv_s = kv[n_p:n_p + n_s].reshape(db, t_new, kv.shape[1])
            o_s = attn_b_sample(q_s, kv_s, cache_b_kv, page_table, diff_lambda[j], norm_sub_b[j], lam_init)
            o_all = jnp.concatenate([o_p, o_s.reshape(n_s, -1).astype(BF16),
                                     jnp.zeros((pad, o_p.shape[1]), BF16)], axis=0)
            x = matmul(o_all, w_o_b[j], mode="res", res=x)
        ht = rms_norm(x, norm_ffn[i], out_ht=True)[0]
        wq_t = peer_wq[i].T.astype(BF16)
        sk = peer_subkeys[i].reshape(PEER_HEADS * 2, N_KEYS, -1)
        slabs = peer_route(ht, wq_t, sk)
        x = peer_experts(x, ht, u_bf[i], v_bf[i], slabs)
        hn = rms_norm(x, norm_ple[i], out_h=True)[0]
        x = ple(x, tokens(p_prompt[i], p_sample[i]), hn, w_ple[i], w_ple_gate[i])

    y = rms_norm(x, norm_final, out_y=True)[0]
    y_prompt = y[:n_p].reshape(batch, seq, d)
    y_sample = y[n_p:n_p + n_s].reshape(db, t_new, d)
    outs_p = [jnp.stack(r, axis=0) for r in a_rows_p]
    outs_s = [jnp.stack(r, axis=0) for r in a_rows_s]
    return (y_prompt, y_sample, *outs_p, *outs_s, new_b_kv_prompt, new_b_kv_sample)
```

```python
import functools
import math

import jax
import jax.numpy as jnp
from jax import lax
from jax.experimental import pallas as pl
from jax.experimental.pallas import tpu as pltpu

BF16 = jnp.bfloat16
F32 = jnp.float32

ROPE_THETA = 500000.0
NORM_EPS = 1e-6
SUBLN_EPS = 1e-5
A_GROUPS = ((128, 1), (512, 4), (2048, 16))
A_HEADS = 8
A_HEAD_DIM = 128
A_BLOCK = 128
B_HEADS = 8
B_QK_DIM = 128
B_V_DIM = 256
PEER_HEADS = 8
N_KEYS = 128
PEER_TOPK = 16
PAGE_SIZE = 128

LANES = 128
SUBLANES = 8
VMEM_LIMIT = 56 * 1024 * 1024
TOKEN_PAD = 512
MATMUL_ROW_CAP = 1100
NOT_SELECTED = 99.0


def _params(sem):
    return pltpu.CompilerParams(dimension_semantics=sem, vmem_limit_bytes=VMEM_LIMIT)


def _token_tile(n, cap):
    packed = 2 * SUBLANES
    for parts in range(1, n // packed + 1):
        if n % parts == 0 and (n // parts) % packed == 0 and n // parts <= cap:
            return n // parts
    raise ValueError(f"no token tile for {n} rows under {cap}")


def _rms_kernel(*refs, eps, has_add, out_x, out_h, out_ht, out_y):
    it = iter(refs)
    x_ref = next(it)
    add_ref = next(it) if has_add else None
    g_ref = next(it)
    x = x_ref[...]
    if has_add:
        x = x + add_ref[...].T
    if out_x:
        next(it)[...] = x
    y = x * lax.rsqrt(jnp.mean(x * x, axis=-1, keepdims=True) + eps) * g_ref[...]
    if out_h:
        next(it)[...] = y.astype(BF16)
    if out_ht:
        next(it)[...] = y.T.astype(BF16)
    if out_y:
        next(it)[...] = y


def rms_norm(x, g, *, add_t=None, out_x=False, out_h=False, out_ht=False, out_y=False,
             eps=NORM_EPS, tm=256):
    n, d = x.shape
    grid = (n // tm,)
    row = pl.BlockSpec((tm, d), lambda i: (i, 0))
    col = pl.BlockSpec((d, tm), lambda i: (0, i))
    in_specs = [row]
    args = [x]
    if add_t is not None:
        in_specs.append(col)
        args.append(add_t)
    in_specs.append(pl.BlockSpec((1, d), lambda i: (0, 0)))
    args.append(g.reshape(1, d))
    out_shape, out_specs = [], []
    if out_x:
        out_shape.append(jax.ShapeDtypeStruct((n, d), F32)); out_specs.append(row)
    if out_h:
        out_shape.append(jax.ShapeDtypeStruct((n, d), BF16)); out_specs.append(row)
    if out_ht:
        out_shape.append(jax.ShapeDtypeStruct((d, n), BF16)); out_specs.append(col)
    if out_y:
        out_shape.append(jax.ShapeDtypeStruct((n, d), F32)); out_specs.append(row)
    kern = functools.partial(_rms_kernel, eps=eps, has_add=add_t is not None, out_x=out_x,
                             out_h=out_h, out_ht=out_ht, out_y=out_y)
    return pl.pallas_call(kern, grid=grid, in_specs=in_specs, out_specs=out_specs,
                          out_shape=out_shape, compiler_params=_params(("parallel",)),
                          name="rms_norm")(*args)


def _rope_tile(y, tab):
    c = tab[:, 0:LANES]
    s1 = tab[:, LANES:2 * LANES]
    s2 = tab[:, 2 * LANES:3 * LANES]
    outs = []
    for g in range(y.shape[1] // LANES):
        yg = y[:, g * LANES:(g + 1) * LANES]
        outs.append(yg * c + pltpu.roll(yg, LANES - 16, 1) * s1 + pltpu.roll(yg, 16, 1) * s2)
    return jnp.concatenate(outs, axis=1) if len(outs) > 1 else outs[0]


def _mm_kernel(*refs, mode, rope_fn, tn, out_scale):
    if mode == "rope":
        x_ref, w_ref, tab_ref, o_ref, wb_ref = refs
    elif mode == "res":
        x_ref, w_ref, r_ref, o_ref, wb_ref = refs
    else:
        x_ref, w_ref, o_ref, wb_ref = refs
    j = pl.program_id(0)

    @pl.when(pl.program_id(1) == 0)
    def _():
        wb_ref[...] = w_ref[...].astype(BF16)

    y = jnp.dot(x_ref[...], wb_ref[...], preferred_element_type=F32)
    if mode == "rope":
        roped = rope_fn(j * tn)

        def finish(v):
            return (v if out_scale == 1.0 else v * out_scale).astype(o_ref.dtype)

        @pl.when(roped)
        def _():
            o_ref[...] = finish(_rope_tile(y, tab_ref[...]))

        @pl.when(jnp.logical_not(roped))
        def _():
            o_ref[...] = finish(y)
    elif mode == "res":
        o_ref[...] = r_ref[...] + y
    else:
        o_ref[...] = y


def matmul(x, w, *, mode="plain", tab=None, res=None, rope_fn=None, tm=MATMUL_ROW_CAP, tn=512,
           out_scale=1.0, out_dtype=F32):
    n, k = x.shape
    m = w.shape[1]
    tm = _token_tile(n, tm)
    tn = min(tn, m)
    grid = (m // tn, n // tm)
    in_specs = [pl.BlockSpec((tm, k), lambda j, i: (i, 0)),
                pl.BlockSpec((k, tn), lambda j, i: (0, j))]
    args = [x, w]
    if mode == "rope":
        in_specs.append(pl.BlockSpec((tm, 3 * LANES), lambda j, i: (i, 0)))
        args.append(tab)
    elif mode == "res":
        in_specs.append(pl.BlockSpec((tm, tn), lambda j, i: (i, j)))
        args.append(res)
    assert mode == "rope" or (out_scale == 1.0 and out_dtype == F32)
    kern = functools.partial(_mm_kernel, mode=mode, rope_fn=rope_fn, tn=tn, out_scale=out_scale)
    return pl.pallas_call(
        kern, grid=grid, in_specs=in_specs,
        out_specs=pl.BlockSpec((tm, tn), lambda j, i: (i, j)),
        out_shape=jax.ShapeDtypeStruct((n, m), out_dtype),
        scratch_shapes=[pltpu.VMEM((k, tn), BF16)],
        compiler_params=_params(("arbitrary", "arbitrary")),
        name="matmul_" + mode)(*args)


def _ple_kernel(x_ref, p_ref, hn_ref, wp_ref, wg_ref, o_ref, wgb_ref):
    @pl.when(pl.program_id(1) == 0)
    def _():
        wgb_ref[...] = wg_ref[...].astype(BF16)

    gate = jax.nn.sigmoid(jnp.dot(hn_ref[...], wgb_ref[...], preferred_element_type=F32))
    up = jnp.dot(p_ref[...].astype(BF16), wp_ref[...].astype(BF16), preferred_element_type=F32)
    o_ref[...] = x_ref[...] + up * gate


def ple(x, p, hn, w_p, w_gate, *, tm=MATMUL_ROW_CAP, tn=512):
    n, d = x.shape
    kp = p.shape[1]
    tn = min(tn, d)
    tm = _token_tile(n, tm)
    grid = (d // tn, n // tm)
    return pl.pallas_call(
        _ple_kernel, grid=grid,
        in_specs=[pl.BlockSpec((tm, tn), lambda j, i: (i, j)),
                  pl.BlockSpec((tm, kp), lambda j, i: (i, 0)),
                  pl.BlockSpec((tm, d), lambda j, i: (i, 0)),
                  pl.BlockSpec((kp, tn), lambda j, i: (0, j)),
                  pl.BlockSpec((d, tn), lambda j, i: (0, j))],
        out_specs=pl.BlockSpec((tm, tn), lambda j, i: (i, j)),
        out_shape=jax.ShapeDtypeStruct((n, d), F32),
        scratch_shapes=[pltpu.VMEM((d, tn), BF16)],
        compiler_params=_params(("arbitrary", "arbitrary")),
        name="ple")(x, p, hn, w_p, w_gate)


def _attn_a_kernel(*refs):
    n_in = 5 * len(A_GROUPS)
    o_ref, og_ref, lg_ref = refs[n_in:n_in + 3]
    blk = pl.program_id(1)
    t_blk = o_ref.shape[0]
    qi = lax.broadcasted_iota(jnp.int32, (A_BLOCK, A_BLOCK), 0)
    kj = lax.broadcasted_iota(jnp.int32, (A_BLOCK, A_BLOCK), 1)
    far = kj >= qi
    near = kj <= qi
    scale = math.log2(math.e) / math.sqrt(A_HEAD_DIM)
    nt = (((1,), (1,)), ((), ()))
    for g, (win, dil) in enumerate(A_GROUPS):
        q_ref, kc_ref, vc_ref, kp_ref, vp_ref = refs[5 * g:5 * g + 5]
        span = A_BLOCK * dil

        def body(it, carry, g=g, dil=dil, span=span, q_ref=q_ref, kc_ref=kc_ref, vc_ref=vc_ref,
                 kp_ref=kp_ref, vp_ref=vp_ref):
            n = it // dil
            r = it % dil
            start = n * span + r
            rows = pl.ds(start, A_BLOCK, stride=dil)
            before = pl.ds(jnp.maximum(start - span, 0), A_BLOCK, stride=dil)
            outside = pl.ds(r, A_BLOCK, stride=dil)
            first = n == 0
            q = q_ref[rows, :].astype(BF16)
            kc = kc_ref[rows, :].astype(BF16)
            vc = vc_ref[rows, :].astype(BF16)
            if span == t_blk:
                kp = kp_ref[outside, :].astype(BF16)
                vp = vp_ref[outside, :].astype(BF16)
            else:
                kp = jnp.where(first, kp_ref[outside, :], kc_ref[before, :]).astype(BF16)
                vp = jnp.where(first, vp_ref[outside, :], vc_ref[before, :]).astype(BF16)
            has_prev = jnp.logical_or(n > 0, blk > 0)
            sp = lax.dot_general(q, kp, nt, preferred_element_type=F32) * scale
            sc = lax.dot_general(q, kc, nt, preferred_element_type=F32) * scale
            sp = jnp.where(jnp.logical_and(far, has_prev), sp, -jnp.inf)
            sc = jnp.where(near, sc, -jnp.inf)
            m = jnp.maximum(jnp.max(sp, axis=-1, keepdims=True), jnp.max(sc, axis=-1, keepdims=True))
            ep = jnp.exp2(sp - m)
            ec = jnp.exp2(sc - m)
            den = jnp.sum(ep, axis=-1, keepdims=True) + jnp.sum(ec, axis=-1, keepdims=True)
            o = (jnp.dot(ep.astype(BF16), vp, preferred_element_type=F32)
                 + jnp.dot(ec.astype(BF16), vc, preferred_element_type=F32))
            og_ref[g, rows, :] = o / den
            lg_ref[g, rows, :] = jnp.broadcast_to(m + jnp.log2(den), (A_BLOCK, A_HEAD_DIM))
            return carry

        lax.fori_loop(0, t_blk // A_BLOCK, body, 0, unroll=8)
    l0, l1, l2 = lg_ref[0], lg_ref[1], lg_ref[2]
    m = jnp.maximum(jnp.maximum(l0, l1), l2)
    e0, e1, e2 = jnp.exp2(l0 - m), jnp.exp2(l1 - m), jnp.exp2(l2 - m)
    out = (e0 * og_ref[0] + e1 * og_ref[1] + e2 * og_ref[2]) / (e0 + e1 + e2)
    o_ref[...] = out.astype(BF16)


def attn_a_prompt(qkv, batch, seq):
    n_tot, width = qkv.shape
    hw = A_HEADS * A_HEAD_DIM
    t_blk = A_BLOCK * max(d for _, d in A_GROUPS)
    assert seq % t_blk == 0
    nblk = seq // t_blk
    in_specs, args = [], []
    for g, (win, dil) in enumerate(A_GROUPS):
        span = A_BLOCK * dil
        per_blk = t_blk // span
        for which in range(3):
            col = (g * 3 + which) * A_HEADS
            in_specs.append(pl.BlockSpec((t_blk, A_HEAD_DIM),
                                         lambda b, k, h, col=col: (b * nblk + k, col + h)))
            args.append(qkv)
        for which in (1, 2):
            col = (g * 3 + which) * A_HEADS
            in_specs.append(pl.BlockSpec(
                (span, A_HEAD_DIM),
                lambda b, k, h, col=col, per_blk=per_blk: (jnp.maximum((b * nblk + k) * per_blk - 1, 0), col + h)))
            args.append(qkv)
    scratch = pltpu.VMEM((len(A_GROUPS), t_blk, A_HEAD_DIM), F32)
    return pl.pallas_call(
        _attn_a_kernel, grid=(batch, nblk, A_HEADS), in_specs=in_specs,
        out_specs=pl.BlockSpec((t_blk, A_HEAD_DIM), lambda b, k, h: (b * nblk + k, h)),
        out_shape=jax.ShapeDtypeStruct((batch * seq, hw), BF16),
        scratch_shapes=[scratch, scratch],
        compiler_params=_params(("parallel", "parallel", "parallel")),
        name="attn_a_prompt")(*args)


def _attn_a_sample_kernel(qkv_ref, c0_ref, c1_ref, c2_ref, o_ref):
    t_new = qkv_ref.shape[1]
    hw = A_HEADS * A_HEAD_DIM
    scale = 1.0 / math.sqrt(A_HEAD_DIM)
    nt = (((1,), (1,)), ((), ()))
    caches = (c0_ref, c1_ref, c2_ref)
    ti = lax.broadcasted_iota(jnp.int32, (t_new, A_BLOCK), 0)
    ni = lax.broadcasted_iota(jnp.int32, (t_new, A_BLOCK), 1)
    tq = lax.broadcasted_iota(jnp.int32, (t_new, t_new), 0)
    tj = lax.broadcasted_iota(jnp.int32, (t_new, t_new), 1)
    for h in range(A_HEADS):
        outs, lses = [], []
        for g, (win, dil) in enumerate(A_GROUPS):
            base = g * 3 * hw + h * A_HEAD_DIM
            q = qkv_ref[0, :, base:base + A_HEAD_DIM].astype(BF16)
            kn = qkv_ref[0, :, base + hw:base + hw + A_HEAD_DIM].astype(BF16)
            vn = qkv_ref[0, :, base + 2 * hw:base + 2 * hw + A_HEAD_DIM].astype(BF16)
            n_sub = min(dil, t_new)
            s_new = lax.dot_general(q, kn, nt, preferred_element_type=F32) * scale
            ok_new = jnp.logical_and(tj <= tq, jnp.bitwise_and(tq - tj, dil - 1) == 0)
            s_new = jnp.where(ok_new, s_new, -jnp.inf)
            ss, vs = [], []
            for r in range(n_sub):
                kb = caches[g][0, :, r * 2 * hw + h * A_HEAD_DIM:r * 2 * hw + (h + 1) * A_HEAD_DIM]
                vb = caches[g][0, :, r * 2 * hw + hw + h * A_HEAD_DIM:r * 2 * hw + hw + (h + 1) * A_HEAD_DIM]
                s = lax.dot_general(q, kb.astype(BF16), nt, preferred_element_type=F32) * scale
                ok = jnp.logical_and(jnp.bitwise_and(ti, dil - 1) == r, ni * dil + r >= ti)
                ss.append(jnp.where(ok, s, -jnp.inf))
                vs.append(vb.astype(BF16))
            m = jnp.max(s_new, axis=-1, keepdims=True)
            for s in ss:
                m = jnp.maximum(m, jnp.max(s, axis=-1, keepdims=True))
            e_new = jnp.exp(s_new - m)
            den = jnp.sum(e_new, axis=-1, keepdims=True)
            o = jnp.dot(e_new.astype(BF16), vn, preferred_element_type=F32)
            for s, vb in zip(ss, vs):
                e = jnp.exp(s - m)
                den = den + jnp.sum(e, axis=-1, keepdims=True)
                o = o + jnp.dot(e.astype(BF16), vb, preferred_element_type=F32)
            outs.append(o / den)
            lses.append(m + jnp.log(den))
        lm = jnp.maximum(jnp.maximum(lses[0], lses[1]), lses[2])
        ws = [jnp.exp(l - lm) for l in lses]
        tot = ws[0] + ws[1] + ws[2]
        comb = (ws[0] * outs[0] + ws[1] * outs[1] + ws[2] * outs[2]) / tot
        o_ref[0, :, h * A_HEAD_DIM:(h + 1) * A_HEAD_DIM] = comb


def attn_a_sample(qkv_s, caches, layer):
    db, t_new, width = qkv_s.shape
    hw = A_HEADS * A_HEAD_DIM
    in_specs = [pl.BlockSpec((1, t_new, width), lambda b: (b, 0, 0))]
    args = [qkv_s]
    for (win, dil), c in zip(A_GROUPS, caches):
        nl = c.shape[0]
        view = c.reshape(nl * db, win // dil, dil * 2 * hw)
        n_sub = min(dil, t_new)
        in_specs.append(pl.BlockSpec((1, win // dil, n_sub * 2 * hw),
                                     lambda b, layer=layer: (layer * db + b, 0, 0)))
        args.append(view)
    return pl.pallas_call(
        _attn_a_sample_kernel, grid=(db,), in_specs=in_specs,
        out_specs=pl.BlockSpec((1, t_new, hw), lambda b: (b, 0, 0)),
        out_shape=jax.ShapeDtypeStruct((db, t_new, hw), F32),
        compiler_params=_params(("parallel",)), name="attn_a_sample")(*args)


def _lambda(lp_ref, lam_init):
    lp = lp_ref[...]
    a = jnp.sum(lp[0:1, :] * lp[1:2, :], axis=-1, keepdims=True)
    b = jnp.sum(lp[2:3, :] * lp[3:4, :], axis=-1, keepdims=True)
    return jnp.exp(a) - jnp.exp(b) + lam_init


def _sub_ln(o, g, lam_init):
    on = o * lax.rsqrt(jnp.mean(o * o, axis=-1, keepdims=True) + SUBLN_EPS) * g
    return on * (1.0 - lam_init)


def _attn_b_kernel(q_ref, k_ref, v_ref, lp_ref, g_ref, o_ref, m_ref, l_ref, acc_ref, *, lam_init, tq, tk):
    qi = pl.program_id(2)
    ki = pl.program_id(3)
    nk = pl.num_programs(3)
    nt = (((1,), (1,)), ((), ()))
    sub = min(tq, 256)

    @pl.when(ki == 0)
    def _():
        m_ref[...] = jnp.full(m_ref.shape, -jnp.inf, F32)
        l_ref[...] = jnp.zeros(l_ref.shape, F32)
        acc_ref[...] = jnp.zeros(acc_ref.shape, F32)

    def step(masked):
        v = v_ref[...].astype(BF16)
        if masked:
            rows = qi * tq + lax.broadcasted_iota(jnp.int32, (tq, tk), 0)
            cols = ki * tk + lax.broadcasted_iota(jnp.int32, (tq, tk), 1)
            mask = cols <= rows
        for c in range(2):
            sl = slice(c * B_QK_DIM, (c + 1) * B_QK_DIM)
            kc = k_ref[:, sl].astype(BF16)
            for r in range(tq // sub):
                rs = slice(r * sub, (r + 1) * sub)
                s = lax.dot_general(q_ref[rs, sl], kc, nt, preferred_element_type=F32)
                if masked:
                    s = jnp.where(mask[rs, :], s, -jnp.inf)
                m_old = m_ref[c, rs, :]
                m_new = jnp.maximum(m_old, jnp.max(s, axis=-1, keepdims=True))
                alpha = jnp.exp2(m_old - m_new)
                p = jnp.exp2(s - m_new)
                l_ref[c, rs, :] = alpha * l_ref[c, rs, :] + jnp.sum(p, axis=-1, keepdims=True)
                acc_ref[c, rs, :] = (alpha * acc_ref[c, rs, :]
                                     + jnp.dot(p.astype(BF16), v, preferred_element_type=F32))
                m_ref[c, rs, :] = m_new

    first_row = qi * tq
    last_col = ki * tk + tk - 1

    @pl.when(last_col <= first_row)
    def _():
        step(False)

    @pl.when(jnp.logical_and(last_col > first_row, ki * tk <= first_row + tq - 1))
    def _():
        step(True)

    @pl.when(ki == nk - 1)
    def _():
        lam = _lambda(lp_ref, lam_init)
        o = acc_ref[0] / l_ref[0] - lam * (acc_ref[1] / l_ref[1])
        o_ref[...] = _sub_ln(o, g_ref[...], lam_init).astype(BF16)


def attn_b_prompt(q, kv, lp, g_sub, lam_init, batch, seq, *, tq=512, tk=512):
    tq = min(tq, seq)
    tk = min(tk, seq)
    nq, nk = seq // tq, seq // tk
    hd = 2 * B_QK_DIM

    def kmap(b, h, i, j):
        return (b * nk + jnp.minimum(j, (i * tq + tq - 1) // tk), h)

    def vmap_(b, h, i, j):
        return (b * nk + jnp.minimum(j, (i * tq + tq - 1) // tk), B_HEADS + h)

    kern = functools.partial(_attn_b_kernel, lam_init=lam_init, tq=tq, tk=tk)
    return pl.pallas_call(
        kern, grid=(batch, B_HEADS, nq, nk),
        in_specs=[pl.BlockSpec((tq, hd), lambda b, h, i, j: (b * nq + i, h)),
                  pl.BlockSpec((tk, hd), kmap),
                  pl.BlockSpec((tk, B_V_DIM), vmap_),
                  pl.BlockSpec((4, B_QK_DIM), lambda b, h, i, j: (0, 0)),
                  pl.BlockSpec((1, B_V_DIM), lambda b, h, i, j: (0, 0))],
        out_specs=pl.BlockSpec((tq, B_V_DIM), lambda b, h, i, j: (b * nq + i, h)),
        out_shape=jax.ShapeDtypeStruct((batch * seq, B_HEADS * B_V_DIM), BF16),
        scratch_shapes=[pltpu.VMEM((2, tq, 1), F32), pltpu.VMEM((2, tq, 1), F32),
                        pltpu.VMEM((2, tq, B_V_DIM), F32)],
        compiler_params=_params(("parallel", "parallel", "parallel", "arbitrary")),
        name="attn_b_prompt")(q, kv, kv, lp, g_sub.reshape(1, B_V_DIM))


def _attn_b_sample_kernel(*refs, lam_init, t_new, n_par):
    pt_ref, q_ref = refs[0], refs[1]
    page_refs = refs[2:2 + 2 * n_par]
    kvn_ref, lp_ref, g_ref, o_ref, m_ref, l_ref, acc_ref, xs_ref = refs[2 + 2 * n_par:]
    p = pl.program_id(1)
    n_steps = pl.num_programs(1)
    kw = B_HEADS * 2 * B_QK_DIM
    nt = (((1,), (1,)), ((), ()))
    rph = 2 * t_new

    @pl.when(p == 0)
    def _():
        m_ref[...] = jnp.full(m_ref.shape, -jnp.inf, F32)
        l_ref[...] = jnp.zeros(l_ref.shape, F32)
        acc_ref[...] = jnp.zeros(acc_ref.shape, F32)

    def update(state, s, weigh):
        m_old, l_old, acc_old = state
        m_new = jnp.maximum(m_old, jnp.max(s, axis=-1, keepdims=True))
        alpha = jnp.exp2(m_old - m_new)
        e = jnp.exp2(s - m_new)
        return m_new, alpha * l_old + jnp.sum(e, axis=-1, keepdims=True), alpha * acc_old + weigh(e)

    half_heads = B_HEADS // 2
    rows_per_page = 4 * PAGE_SIZE
    for g in range(half_heads):
        for half in range(2):
            for k in range(n_par):
                xs_ref[g, half, k * rows_per_page:(k + 1) * rows_per_page, :] = (
                    page_refs[2 * k + half][0, pl.ds(g, rows_per_page, stride=half_heads), :].astype(BF16))

    def pair_rows(x, g):
        return jnp.concatenate([x[g * rph:(g + 1) * rph], x[(g + half_heads) * rph:(g + half_heads + 1) * rph]],
                               axis=0)

    q_all = q_ref[0]
    lo, hi = [None] * half_heads, [None] * half_heads
    for g in range(half_heads):
        qg = pair_rows(q_all, g)
        sg = (lax.dot_general(qg[:, 0:B_QK_DIM], xs_ref[g, 0], nt, preferred_element_type=F32)
              + lax.dot_general(qg[:, B_QK_DIM:], xs_ref[g, 1], nt, preferred_element_type=F32))
        lo[g], hi[g] = sg[0:rph], sg[rph:2 * rph]
    s = jnp.concatenate(lo + hi, axis=0)
    kind = jnp.bitwise_and(lax.broadcasted_iota(jnp.int32, s.shape, 1), 3)
    own = (lax.broadcasted_iota(jnp.int32, s.shape, 0) >= half_heads * rph).astype(jnp.int32)
    s = jnp.where(kind == own, s, -jnp.inf)

    def weigh(e):
        ev = pltpu.roll(e, 2, 1)
        lo_o, hi_o = [None] * half_heads, [None] * half_heads
        for g in range(half_heads):
            eg = pair_rows(ev, g).astype(BF16)
            og = jnp.concatenate([jnp.dot(eg, xs_ref[g, 0], preferred_element_type=F32),
                                  jnp.dot(eg, xs_ref[g, 1], preferred_element_type=F32)], axis=1)
            lo_o[g], hi_o[g] = og[0:rph], og[rph:2 * rph]
        return jnp.concatenate(lo_o + hi_o, axis=0)

    m_new, l_new, acc_new = update((m_ref[...], l_ref[...], acc_ref[...]), s, weigh)
    m_ref[...] = m_new
    l_ref[...] = l_new
    acc_ref[...] = acc_new

    @pl.when(p == n_steps - 1)
    def _():
        lam = _lambda(lp_ref, lam_init)
        for h in range(B_HEADS):
            rs = slice(h * rph, (h + 1) * rph)
            qh = q_ref[0, rs, :]
            kn = kvn_ref[0, :, h * B_V_DIM:(h + 1) * B_V_DIM].astype(BF16)
            vn = kvn_ref[0, :, kw + h * B_V_DIM:kw + (h + 1) * B_V_DIM].astype(BF16)
            sn = lax.dot_general(qh, kn, nt, preferred_element_type=F32)
            rq = jnp.bitwise_and(lax.broadcasted_iota(jnp.int32, sn.shape, 0), t_new - 1)
            cj = lax.broadcasted_iota(jnp.int32, sn.shape, 1)
            _, l_fin, acc_fin = update(
                (m_ref[rs, :], l_ref[rs, :], acc_ref[rs, :]), jnp.where(cj <= rq, sn, -jnp.inf),
                lambda e, vn=vn: jnp.dot(e.astype(BF16), vn, preferred_element_type=F32))
            on = acc_fin / l_fin
            o = on[0:t_new, :] - lam * on[t_new:2 * t_new, :]
            o_ref[0, :, h * B_V_DIM:(h + 1) * B_V_DIM] = _sub_ln(o, g_ref[...], lam_init)


def attn_b_sample(q_s, kv_s, cache_b_kv, page_table, lp, g_sub, lam_init):
    db, t_new, qw = q_s.shape
    n_pages = page_table.shape[1]
    n_phys = cache_b_kv.shape[0]
    assert t_new & (t_new - 1) == 0
    n_par = next(k for k in (8, 4, 2, 1) if n_pages % k == 0)
    kvw = 2 * B_HEADS * B_V_DIM
    page_rows = PAGE_SIZE * 2 * B_HEADS
    pages = cache_b_kv.reshape(n_phys, page_rows, B_V_DIM)
    q5 = q_s.reshape(db, t_new, B_HEADS, 2, B_QK_DIM).transpose(0, 2, 3, 1, 4)
    eye = jnp.eye(2, dtype=q_s.dtype)
    qbd = q5[:, :, :, :, None, :] * eye[None, None, :, None, :, None]
    rows = B_HEADS * 2 * t_new
    qbd = qbd.reshape(db, rows, 2 * B_QK_DIM).astype(BF16)
    kern = functools.partial(_attn_b_sample_kernel, lam_init=lam_init, t_new=t_new, n_par=n_par)
    page_specs = [pl.BlockSpec((1, page_rows, B_QK_DIM),
                               lambda b, p, pt, k=k, half=half: (pt[b, p * n_par + k], 0, half))
                  for k in range(n_par) for half in range(2)]
    grid_spec = pltpu.PrefetchScalarGridSpec(
        num_scalar_prefetch=1, grid=(db, n_pages // n_par),
        in_specs=[pl.BlockSpec((1, rows, 2 * B_QK_DIM), lambda b, p, pt: (b, 0, 0))] + page_specs + [
                  pl.BlockSpec((1, t_new, kvw), lambda b, p, pt: (b, 0, 0)),
                  pl.BlockSpec((4, B_QK_DIM), lambda b, p, pt: (0, 0)),
                  pl.BlockSpec((1, B_V_DIM), lambda b, p, pt: (0, 0))],
        out_specs=pl.BlockSpec((1, t_new, qw), lambda b, p, pt: (b, 0, 0)),
        scratch_shapes=[pltpu.VMEM((rows, 1), F32), pltpu.VMEM((rows, 1), F32),
                        pltpu.VMEM((rows, B_V_DIM), F32),
                        pltpu.VMEM((B_HEADS // 2, 2, n_par * 4 * PAGE_SIZE, B_QK_DIM), BF16)])
    return pl.pallas_call(
        kern, grid_spec=grid_spec,
        out_shape=jax.ShapeDtypeStruct((db, t_new, qw), F32),
        compiler_params=_params(("parallel", "arbitrary")),
        name="attn_b_sample")(page_table, qbd, *([pages] * (2 * n_par)), kv_s, lp, g_sub.reshape(1, B_V_DIM))


def _top_ranks(s, exact):
    iota = lax.broadcasted_iota(jnp.int32, s.shape, 0).astype(F32)
    rank = jnp.full(s.shape, NOT_SELECTED, F32)
    vals = []
    work = s
    for a in range(PEER_TOPK):
        m = jnp.max(work, axis=0, keepdims=True)
        if exact:
            idx = jnp.min(jnp.where(work == m, iota, float(N_KEYS)), axis=0, keepdims=True)
            hit = iota == idx
        else:
            hit = work == m
        rank = jnp.where(hit, float(a), rank)
        work = jnp.where(hit, -jnp.inf, work)
        vals.append(m)
    taken = jnp.sum(jnp.where(rank < NOT_SELECTED, 1.0, 0.0), axis=0, keepdims=True)
    return vals, rank, taken


def _candidate_rows():
    pieces = [(0, 0, 16, 16)]
    for a in range(1, 8):
        pieces.append((a, 0, 8, PEER_TOPK // (a + 1)))
    return pieces


def _route_kernel(ht_ref, wq_ref, sk_ref, r1_ref, e1_ref, bq_out_ref, c0_out_ref, qt_ref, bq_ref, c0_ref):
    n_chunks = ht_ref.shape[1] // LANES
    qt_ref[...] = jnp.dot(wq_ref[...], ht_ref[...], preferred_element_type=F32)
    neg = -jnp.inf
    t = LANES

    def route_chunk(h, cs, scores, exact):
        vals, ranks = [], []
        off = jnp.zeros((1, t), F32)
        for c in range(2):
            v, r, taken = _top_ranks(scores[c], exact)
            vals.append(v); ranks.append(r)
            off = off + jnp.abs(taken - float(PEER_TOPK))
        v0, v1 = vals
        v1_16 = jnp.concatenate(v1, axis=0)
        v0_hi = jnp.concatenate(v0[8:16], axis=0)
        cands, flats = [], []
        for a, _, rows, nvalid in _candidate_rows():
            b_iota = lax.broadcasted_iota(jnp.int32, (rows, t), 0)
            cs_ab = v0[a] + v1_16[0:rows, :]
            cands.append(jnp.where(b_iota < nvalid, cs_ab, neg))
            flats.append((b_iota + a * PEER_TOPK).astype(F32))
        cands.append(v0_hi + v1[0])
        flats.append(((lax.broadcasted_iota(jnp.int32, (8, t), 0) + 8) * PEER_TOPK).astype(F32))
        cand = jnp.concatenate(cands, axis=0)
        flat = jnp.concatenate(flats, axis=0)
        big = float(PEER_TOPK * PEER_TOPK)
        work = cand
        sel = jnp.zeros(cand.shape, F32)
        for _ in range(PEER_TOPK):
            m = jnp.max(work, axis=0, keepdims=True)
            if exact:
                idx = jnp.min(jnp.where(work == m, flat, big), axis=0, keepdims=True)
                hit = flat == idx
            else:
                hit = work == m
            sel = jnp.where(hit, 1.0, sel)
            work = jnp.where(hit, neg, work)
        off = off + jnp.abs(jnp.sum(sel, axis=0, keepdims=True) - float(PEER_TOPK))
        top = v0[0] + v1[0]
        z = jnp.sum(jnp.where(sel > 0.0, jnp.exp(cand - top), 0.0), axis=0, keepdims=True)
        counts = [jnp.sum(sel[0:16, :], axis=0, keepdims=True)]
        for k in range(1, 8):
            counts.append(jnp.sum(sel[8 + 8 * k:16 + 8 * k, :], axis=0, keepdims=True))
        hi = sel[72:80, :]
        bq = jnp.zeros(ranks[0].shape, F32)
        for a in range(PEER_TOPK):
            cnt = counts[a] if a < 8 else hi[a - 8:a - 7, :]
            bq = jnp.where(ranks[0] == float(a), cnt, bq)
        rs = pl.ds(pl.multiple_of(h * N_KEYS, N_KEYS), N_KEYS)
        bq_ref[rs, cs] = bq
        c0_ref[rs, cs] = jnp.exp(scores[0] - v0[0]) * (0.5 / z)
        r1_ref[rs, cs] = ranks[1].astype(BF16)
        e1_ref[rs, cs] = jnp.exp(scores[1] - v1[0]).astype(BF16)
        return off

    def body(h, carry):
        work = []
        for k in range(n_chunks):
            cs = slice(k * LANES, (k + 1) * LANES)
            scores = []
            for c in range(2):
                hc = h * 2 + c
                qhc = qt_ref[pl.ds(pl.multiple_of(hc * N_KEYS, N_KEYS), N_KEYS), cs].astype(BF16)
                scores.append(jnp.dot(sk_ref[hc].astype(BF16), qhc, preferred_element_type=F32))
            work.append((cs, scores, route_chunk(h, cs, scores, exact=False)))
        for cs, scores, off in work:
            @pl.when(jnp.max(off) > 0.0)
            def _(cs=cs, scores=scores):
                route_chunk(h, cs, scores, exact=True)

        return carry

    lax.fori_loop(0, PEER_HEADS, body, 0)
    for h in range(PEER_HEADS):
        bq_out_ref[:, h, :] = bq_ref[h * N_KEYS:(h + 1) * N_KEYS, :]
        c0_out_ref[:, h, :] = c0_ref[h * N_KEYS:(h + 1) * N_KEYS, :]


def peer_route(ht, wq_t, subkeys, *, tm=512):
    d, n = ht.shape
    rows = PEER_HEADS * N_KEYS
    slab = pl.BlockSpec((rows, tm), lambda i: (0, i))
    sds = jax.ShapeDtypeStruct((rows, n), BF16)
    slab3 = pl.BlockSpec((N_KEYS, PEER_HEADS, tm), lambda i: (0, 0, i))
    sds3 = jax.ShapeDtypeStruct((N_KEYS, PEER_HEADS, n), F32)
    return pl.pallas_call(
        _route_kernel, grid=(n // tm,),
        in_specs=[pl.BlockSpec((d, tm), lambda i: (0, i)),
                  pl.BlockSpec(wq_t.shape, lambda i: (0, 0)),
                  pl.BlockSpec(subkeys.shape, lambda i: (0, 0, 0))],
        out_specs=[slab, slab, slab3, slab3], out_shape=[sds, sds, sds3, sds3],
        scratch_shapes=[pltpu.VMEM((wq_t.shape[0], tm), F32), pltpu.VMEM((rows, tm), F32),
                        pltpu.VMEM((rows, tm), F32)],
        compiler_params=_params(("parallel",)), name="peer_route")(ht, wq_t, subkeys)


def _gelu_x2(a):
    return a * (1.0 + lax.erf(a * math.sqrt(0.5)))


def _expert_kernel(x_ref, ht_ref, u_ref, v_ref, r1_ref, e1_ref, bq_ref, c0_ref, o_ref, *a_refs, te, tm, ge):
    e = pl.program_id(1)
    rows_per_group = ge // N_KEYS
    n_chunks = tm // LANES
    packed = 2 * SUBLANES
    tiles = N_KEYS // packed
    nn = (((1,), (0,)), ((), ()))

    @pl.when(e == 0)
    def _():
        o_ref[...] = x_ref[...]

    n_groups = te // ge
    for k in range(n_groups):
        a_refs[k][...] = lax.dot_general(u_ref[k * ge:(k + 1) * ge, :], ht_ref[...], nn,
                                         preferred_element_type=F32)
    groups = []
    for k in range(n_groups):
        rows = []
        for r in range(rows_per_group):
            i = e * (te // N_KEYS) + k * rows_per_group + r
            cols = []
            for c in range(n_chunks):
                cs = slice(c * LANES, (c + 1) * LANES)
                bq_all = bq_ref[i, :, cs]
                c0_all = c0_ref[i, :, cs]
                w = [jnp.zeros((packed, LANES), BF16)] * tiles
                for h in range(PEER_HEADS):
                    bq = jnp.broadcast_to(bq_all[h:h + 1, :], (packed, LANES)).astype(BF16)
                    c0 = jnp.broadcast_to(c0_all[h:h + 1, :], (packed, LANES)).astype(BF16)
                    for t in range(tiles):
                        js = slice(h * N_KEYS + t * packed, h * N_KEYS + (t + 1) * packed)
                        e1 = e1_ref[js, cs]
                        w[t] = w[t] + jnp.where(r1_ref[js, cs] < bq, e1 * c0, jnp.zeros_like(e1))
                gate = jnp.concatenate(w, axis=0).astype(F32)
                cols.append(_gelu_x2(a_refs[k][r * N_KEYS:(r + 1) * N_KEYS, cs]) * gate)
            rows.append(jnp.concatenate(cols, axis=1))
        groups.append(jnp.concatenate(rows, axis=0).T)
    g = jnp.concatenate(groups, axis=1)
    o_ref[...] += lax.dot_general(g, v_ref[...], nn, preferred_element_type=F32)


def peer_experts(x, ht, u, v_bf, slabs, *, tm=512, te=1024, ge=256):
    d, n = ht.shape
    n_exp = u.shape[0]
    while n % tm:
        tm //= 2
    te = min(te, n_exp)
    rows = PEER_HEADS * N_KEYS
    once = pl.Buffered(1)
    slab = pl.BlockSpec((rows, tm), lambda i, e: (0, i), pipeline_mode=once)
    slab3 = pl.BlockSpec((N_KEYS, PEER_HEADS, tm), lambda i, e: (0, 0, i), pipeline_mode=once)
    kern = functools.partial(_expert_kernel, te=te, tm=tm, ge=ge)
    return pl.pallas_call(
        kern, grid=(n // tm, n_exp // te),
        in_specs=[pl.BlockSpec((tm, d), lambda i, e: (i, 0), pipeline_mode=once),
                  pl.BlockSpec((d, tm), lambda i, e: (0, i), pipeline_mode=once),
                  pl.BlockSpec((te, d), lambda i, e: (e, 0)),
                  pl.BlockSpec((te, d), lambda i, e: (e, 0)),
                  slab, slab, slab3, slab3],
        out_specs=pl.BlockSpec((tm, d), lambda i, e: (i, 0)),
        out_shape=jax.ShapeDtypeStruct((n, d), F32),
        scratch_shapes=[pltpu.VMEM((ge, tm), F32) for _ in range(te // ge)],
        compiler_params=_params(("parallel", "arbitrary")),
        name="peer_experts")(x, ht, u, v_bf, *slabs)


def _rope_table(pos):
    rot = A_HEAD_DIM // 4
    half = rot // 2
    inv_freq = ROPE_THETA ** (-jnp.arange(half, dtype=F32) / half)
    ang = pos.astype(F32)[:, None] * inv_freq[None, :]
    cos, sin = jnp.cos(ang), jnp.sin(ang)
    n = pos.shape[0]
    ones = jnp.ones((n, A_HEAD_DIM - rot), F32)
    zeros = jnp.zeros((n, A_HEAD_DIM - rot), F32)
    zh = jnp.zeros((n, half), F32)
    return jnp.concatenate([cos, cos, ones, -sin, zh, zeros, zh, sin, zeros], axis=1)


def kernel(x_prompt, x_sample, cache_a_w128, cache_a_w512, cache_a_w2048, cache_b_kv, page_table,
           p_prompt, p_sample, norm_mix, norm_ffn, norm_ple, norm_kv, norm_final,
           w_qkv_a, w_o_a, w_kv_b, w_q_b, diff_lambda, norm_sub_b, w_o_b,
           peer_wq, peer_subkeys, peer_u, peer_v, w_ple, w_ple_gate):
    batch, seq, d = x_prompt.shape
    db, t_new, _ = x_sample.shape
    depth = norm_mix.shape[0]
    n_a = w_qkv_a.shape[0]
    past_len = page_table.shape[1] * PAGE_SIZE
    n_p, n_s = batch * seq, db * t_new
    n_tot = -(-(n_p + n_s) // TOKEN_PAD) * TOKEN_PAD
    pad = n_tot - n_p - n_s
    a_caches = (cache_a_w128, cache_a_w512, cache_a_w2048)
    hw = A_HEADS * A_HEAD_DIM

    def tokens(prompt_part, sample_part):
        w = prompt_part.shape[-1]
        return jnp.concatenate([prompt_part.reshape(n_p, w), sample_part.reshape(n_s, w),
                                jnp.zeros((pad, w), prompt_part.dtype)], axis=0)

    x = tokens(x_prompt, x_sample)
    pos = jnp.concatenate([jnp.tile(jnp.arange(seq, dtype=jnp.int32), batch),
                           jnp.tile(past_len + jnp.arange(t_new, dtype=jnp.int32), db),
                           jnp.zeros((pad,), jnp.int32)])
    tab = _rope_table(pos)
    v_bf = peer_v.astype(BF16)

    a_rows_p = [[] for _ in A_GROUPS]
    a_rows_s = [[] for _ in A_GROUPS]
    new_b_kv_prompt = new_b_kv_sample = kv = None
    for i in range(depth):
        h = rms_norm(x, norm_mix[i], out_h=True)[0]
        if i < n_a:
            qkv = matmul(h, w_qkv_a[i], mode="rope", tab=tab, tn=hw,
                         rope_fn=lambda col: (col // hw) % 3 != 2)
            o_p = attn_a_prompt(qkv, batch, seq)
            qkv_s = qkv[n_p:n_p + n_s].reshape(db, t_new, qkv.shape[1])
            o_s = attn_a_sample(qkv_s, a_caches, i).reshape(n_s, hw).astype(BF16)
            o_all = jnp.concatenate([o_p, o_s, jnp.zeros((pad, hw), BF16)], axis=0)
            x = matmul(o_all, w_o_a[i], mode="res", res=x)
            for g, (win, dil) in enumerate(A_GROUPS):
                wb = min(win, seq)
                c0, c1 = (g * 3 + 1) * hw, (g * 3 + 3) * hw
                rows = jnp.stack([qkv[(b + 1) * seq - wb:(b + 1) * seq, c0:c1] for b in range(batch)], axis=0)
                a_rows_p[g].append(rows.reshape(batch, wb, 2, A_HEADS, A_HEAD_DIM))
                a_rows_s[g].append(qkv[n_p:n_p + n_s, c0:c1].reshape(db, t_new, 2, A_HEADS, A_HEAD_DIM))
        else:
            j = i - n_a
            if j == 0:
                hkv = rms_norm(x, norm_kv, out_h=True)[0]
                kw = B_HEADS * 2 * B_QK_DIM
                kv = matmul(hkv, w_kv_b, mode="rope", tab=tab, tn=1024, rope_fn=lambda col: col < kw)
                new_b_kv_prompt = kv[:n_p].reshape(batch, seq, 2, B_HEADS, B_V_DIM)
                new_b_kv_sample = kv[n_p:n_p + n_s].reshape(db, t_new, 2, B_HEADS, B_V_DIM)
            lam_init = 0.8 - 0.6 * math.exp(-0.3 * i)
            q = matmul(h, w_q_b[j], mode="rope", tab=tab, tn=1024, rope_fn=lambda col: col >= 0,
                       out_scale=math.log2(math.e) / math.sqrt(B_QK_DIM), out_dtype=BF16)
            o_p = attn_b_prompt(q, kv, diff_lambda[j], norm_sub_b[j], lam_init, batch, seq)
            q_s = q[n_p:n_p + n_s].reshape(db, t_new, q.shape[1])
            kv_s = kv[n_p:n_p + n_s].reshape(db, t_new, kv.shape[1])
            o_s = attn_b_sample(q_s, kv_s, cache_b_kv, page_table, diff_lambda[j], norm_sub_b[j], lam_init)
            o_all = jnp.concatenate([o_p, o_s.reshape(n_s, -1).astype(BF16),
                                     jnp.zeros((pad, o_p.shape[1]), BF16)], axis=0)
            x = matmul(o_all, w_o_b[j], mode="res", res=x)
        ht = rms_norm(x, norm_ffn[i], out_ht=True)[0]
        wq_t = peer_wq[i].T.astype(BF16)
        sk = peer_subkeys[i].reshape(PEER_HEADS * 2, N_KEYS, -1)
        slabs = peer_route(ht, wq_t, sk)
        x = peer_experts(x, ht, peer_u[i], v_bf[i], slabs)
        hn = rms_norm(x, norm_ple[i], out_h=True)[0]
        x = ple(x, tokens(p_prompt[i], p_sample[i]), hn, w_ple[i], w_ple_gate[i])

    y = rms_norm(x, norm_final, out_y=True)[0]
    y_prompt = y[:n_p].reshape(batch, seq, d)
    y_sample = y[n_p:n_p + n_s].reshape(db, t_new, d)
    outs_p = [jnp.stack(r, axis=0) for r in a_rows_p]
    outs_s = [jnp.stack(r, axis=0) for r in a_rows_s]
    return (y_prompt, y_sample, *outs_p, *outs_s, new_b_kv_prompt, new_b_kv_sample)
```

```python
import functools
import math

import jax
import jax.numpy as jnp
from jax import lax
from jax.experimental import pallas as pl
from jax.experimental.pallas import tpu as pltpu

BF16 = jnp.bfloat16
F32 = jnp.float32

ROPE_THETA = 500000.0
NORM_EPS = 1e-6
SUBLN_EPS = 1e-5
A_GROUPS = ((128, 1), (512, 4), (2048, 16))
A_HEADS = 8
A_HEAD_DIM = 128
A_BLOCK = 128
B_HEADS = 8
B_QK_DIM = 128
B_V_DIM = 256
PEER_HEADS = 8
N_KEYS = 128
PEER_TOPK = 16
PAGE_SIZE = 128

LANES = 128
SUBLANES = 8
VMEM_LIMIT = 56 * 1024 * 1024
TOKEN_PAD = 512
MATMUL_ROW_CAP = 1100
NOT_SELECTED = 99.0


def _params(sem):
    return pltpu.CompilerParams(dimension_semantics=sem, vmem_limit_bytes=VMEM_LIMIT)


def _token_tile(n, cap):
    packed = 2 * SUBLANES
    for parts in range(1, n // packed + 1):
        if n % parts == 0 and (n // parts) % packed == 0 and n // parts <= cap:
            return n // parts
    raise ValueError(f"no token tile for {n} rows under {cap}")


def _rms_kernel(*refs, eps, has_add, out_x, out_h, out_ht, out_y):
    it = iter(refs)
    x_ref = next(it)
    add_ref = next(it) if has_add else None
    g_ref = next(it)
    x = x_ref[...]
    if has_add:
        x = x + add_ref[...].T
    if out_x:
        next(it)[...] = x
    y = x * lax.rsqrt(jnp.mean(x * x, axis=-1, keepdims=True) + eps) * g_ref[...]
    if out_h:
        next(it)[...] = y.astype(BF16)
    if out_ht:
        next(it)[...] = y.T.astype(BF16)
    if out_y:
        next(it)[...] = y


def rms_norm(x, g, *, add_t=None, out_x=False, out_h=False, out_ht=False, out_y=False,
             eps=NORM_EPS, tm=256):
    n, d = x.shape
    grid = (n // tm,)
    row = pl.BlockSpec((tm, d), lambda i: (i, 0))
    col = pl.BlockSpec((d, tm), lambda i: (0, i))
    in_specs = [row]
    args = [x]
    if add_t is not None:
        in_specs.append(col)
        args.append(add_t)
    in_specs.append(pl.BlockSpec((1, d), lambda i: (0, 0)))
    args.append(g.reshape(1, d))
    out_shape, out_specs = [], []
    if out_x:
        out_shape.append(jax.ShapeDtypeStruct((n, d), F32)); out_specs.append(row)
    if out_h:
        out_shape.append(jax.ShapeDtypeStruct((n, d), BF16)); out_specs.append(row)
    if out_ht:
        out_shape.append(jax.ShapeDtypeStruct((d, n), BF16)); out_specs.append(col)
    if out_y:
        out_shape.append(jax.ShapeDtypeStruct((n, d), F32)); out_specs.append(row)
    kern = functools.partial(_rms_kernel, eps=eps, has_add=add_t is not None, out_x=out_x,
                             out_h=out_h, out_ht=out_ht, out_y=out_y)
    return pl.pallas_call(kern, grid=grid, in_specs=in_specs, out_specs=out_specs,
                          out_shape=out_shape, compiler_params=_params(("parallel",)),
                          name="rms_norm")(*args)


def _rope_tile(y, tab):
    c = tab[:, 0:LANES]
    s1 = tab[:, LANES:2 * LANES]
    s2 = tab[:, 2 * LANES:3 * LANES]
    outs = []
    for g in range(y.shape[1] // LANES):
        yg = y[:, g * LANES:(g + 1) * LANES]
        outs.append(yg * c + pltpu.roll(yg, LANES - 16, 1) * s1 + pltpu.roll(yg, 16, 1) * s2)
    return jnp.concatenate(outs, axis=1) if len(outs) > 1 else outs[0]


def _mm_kernel(*refs, mode, rope_fn, tn, out_scale):
    if mode == "rope":
        x_ref, w_ref, tab_ref, o_ref, wb_ref = refs
    elif mode == "res":
        x_ref, w_ref, r_ref, o_ref, wb_ref = refs
    else:
        x_ref, w_ref, o_ref, wb_ref = refs
    j = pl.program_id(0)

    @pl.when(pl.program_id(1) == 0)
    def _():
        wb_ref[...] = w_ref[...].astype(BF16)

    y = jnp.dot(x_ref[...], wb_ref[...], preferred_element_type=F32)
    if mode == "rope":
        roped = rope_fn(j * tn)

        def finish(v):
            return (v if out_scale == 1.0 else v * out_scale).astype(o_ref.dtype)

        @pl.when(roped)
        def _():
            o_ref[...] = finish(_rope_tile(y, tab_ref[...]))

        @pl.when(jnp.logical_not(roped))
        def _():
            o_ref[...] = finish(y)
    elif mode == "res":
        o_ref[...] = r_ref[...] + y
    else:
        o_ref[...] = y


def matmul(x, w, *, mode="plain", tab=None, res=None, rope_fn=None, tm=MATMUL_ROW_CAP, tn=512,
           out_scale=1.0, out_dtype=F32, layer=None):
    n, k = x.shape
    m = w.shape[-1]
    tm = _token_tile(n, tm)
    tn = min(tn, m)
    grid = (m // tn, n // tm)
    if layer is None:
        w_spec = pl.BlockSpec((k, tn), lambda j, i: (0, j))
    else:
        w_spec = pl.BlockSpec((None, k, tn), lambda j, i: (layer, 0, j))
    in_specs = [pl.BlockSpec((tm, k), lambda j, i: (i, 0)), w_spec]
    args = [x, w]
    if mode == "rope":
        in_specs.append(pl.BlockSpec((tm, 3 * LANES), lambda j, i: (i, 0)))
        args.append(tab)
    elif mode == "res":
        in_specs.append(pl.BlockSpec((tm, tn), lambda j, i: (i, j)))
        args.append(res)
    assert mode == "rope" or (out_scale == 1.0 and out_dtype == F32)
    kern = functools.partial(_mm_kernel, mode=mode, rope_fn=rope_fn, tn=tn, out_scale=out_scale)
    return pl.pallas_call(
        kern, grid=grid, in_specs=in_specs,
        out_specs=pl.BlockSpec((tm, tn), lambda j, i: (i, j)),
        out_shape=jax.ShapeDtypeStruct((n, m), out_dtype),
        scratch_shapes=[pltpu.VMEM((k, tn), BF16)],
        compiler_params=_params(("arbitrary", "arbitrary")),
        name="matmul_" + mode)(*args)


def _ple_kernel(x_ref, p_ref, hn_ref, wp_ref, wg_ref, o_ref, wgb_ref):
    @pl.when(pl.program_id(1) == 0)
    def _():
        wgb_ref[...] = wg_ref[...].astype(BF16)

    gate = jax.nn.sigmoid(jnp.dot(hn_ref[...], wgb_ref[...], preferred_element_type=F32))
    up = jnp.dot(p_ref[...].astype(BF16), wp_ref[...].astype(BF16), preferred_element_type=F32)
    o_ref[...] = x_ref[...] + up * gate


def ple(x, p, hn, w_p, w_gate, layer, *, tm=MATMUL_ROW_CAP, tn=512):
    n, d = x.shape
    kp = p.shape[1]
    tn = min(tn, d)
    tm = _token_tile(n, tm)
    grid = (d // tn, n // tm)
    return pl.pallas_call(
        _ple_kernel, grid=grid,
        in_specs=[pl.BlockSpec((tm, tn), lambda j, i: (i, j)),
                  pl.BlockSpec((tm, kp), lambda j, i: (i, 0)),
                  pl.BlockSpec((tm, d), lambda j, i: (i, 0)),
                  pl.BlockSpec((None, kp, tn), lambda j, i: (layer, 0, j)),
                  pl.BlockSpec((None, d, tn), lambda j, i: (layer, 0, j))],
        out_specs=pl.BlockSpec((tm, tn), lambda j, i: (i, j)),
        out_shape=jax.ShapeDtypeStruct((n, d), F32),
        scratch_shapes=[pltpu.VMEM((d, tn), BF16)],
        compiler_params=_params(("arbitrary", "arbitrary")),
        name="ple")(x, p, hn, w_p, w_gate)


def _attn_a_kernel(*refs):
    n_in = 5 * len(A_GROUPS)
    o_ref, og_ref, lg_ref = refs[n_in:n_in + 3]
    blk = pl.program_id(1)
    t_blk = o_ref.shape[0]
    qi = lax.broadcasted_iota(jnp.int32, (A_BLOCK, A_BLOCK), 0)
    kj = lax.broadcasted_iota(jnp.int32, (A_BLOCK, A_BLOCK), 1)
    far = kj >= qi
    near = kj <= qi
    scale = math.log2(math.e) / math.sqrt(A_HEAD_DIM)
    nt = (((1,), (1,)), ((), ()))
    for g, (win, dil) in enumerate(A_GROUPS):
        q_ref, kc_ref, vc_ref, kp_ref, vp_ref = refs[5 * g:5 * g + 5]
        span = A_BLOCK * dil

        def body(it, carry, g=g, dil=dil, span=span, q_ref=q_ref, kc_ref=kc_ref, vc_ref=vc_ref,
                 kp_ref=kp_ref, vp_ref=vp_ref):
            n = it // dil
            r = it % dil
            start = n * span + r
            rows = pl.ds(start, A_BLOCK, stride=dil)
            before = pl.ds(jnp.maximum(start - span, 0), A_BLOCK, stride=dil)
            outside = pl.ds(r, A_BLOCK, stride=dil)
            first = n == 0
            q = q_ref[rows, :].astype(BF16)
            kc = kc_ref[rows, :].astype(BF16)
            vc = vc_ref[rows, :].astype(BF16)
            if span == t_blk:
                kp = kp_ref[outside, :].astype(BF16)
                vp = vp_ref[outside, :].astype(BF16)
            else:
                kp = jnp.where(first, kp_ref[outside, :], kc_ref[before, :]).astype(BF16)
                vp = jnp.where(first, vp_ref[outside, :], vc_ref[before, :]).astype(BF16)
            has_prev = jnp.logical_or(n > 0, blk > 0)
            sp = lax.dot_general(q, kp, nt, preferred_element_type=F32) * scale
            sc = lax.dot_general(q, kc, nt, preferred_element_type=F32) * scale
            sp = jnp.where(jnp.logical_and(far, has_prev), sp, -jnp.inf)
            sc = jnp.where(near, sc, -jnp.inf)
            m = jnp.maximum(jnp.max(sp, axis=-1, keepdims=True), jnp.max(sc, axis=-1, keepdims=True))
            ep = jnp.exp2(sp - m)
            ec = jnp.exp2(sc - m)
            den = jnp.sum(ep, axis=-1, keepdims=True) + jnp.sum(ec, axis=-1, keepdims=True)
            o = (jnp.dot(ep.astype(BF16), vp, preferred_element_type=F32)
                 + jnp.dot(ec.astype(BF16), vc, preferred_element_type=F32))
            og_ref[g, rows, :] = o / den
            lg_ref[g, rows, :] = jnp.broadcast_to(m + jnp.log2(den), (A_BLOCK, A_HEAD_DIM))
            return carry

        lax.fori_loop(0, t_blk // A_BLOCK, body, 0, unroll=8)
    l0, l1, l2 = lg_ref[0], lg_ref[1], lg_ref[2]
    m = jnp.maximum(jnp.maximum(l0, l1), l2)
    e0, e1, e2 = jnp.exp2(l0 - m), jnp.exp2(l1 - m), jnp.exp2(l2 - m)
    out = (e0 * og_ref[0] + e1 * og_ref[1] + e2 * og_ref[2]) / (e0 + e1 + e2)
    o_ref[...] = out.astype(BF16)


def attn_a_prompt(qkv, batch, seq):
    n_tot, width = qkv.shape
    hw = A_HEADS * A_HEAD_DIM
    t_blk = A_BLOCK * max(d for _, d in A_GROUPS)
    assert seq % t_blk == 0
    nblk = seq // t_blk
    in_specs, args = [], []
    for g, (win, dil) in enumerate(A_GROUPS):
        span = A_BLOCK * dil
        per_blk = t_blk // span
        for which in range(3):
            col = (g * 3 + which) * A_HEADS
            in_specs.append(pl.BlockSpec((t_blk, A_HEAD_DIM),
                                         lambda b, k, h, col=col: (b * nblk + k, col + h)))
            args.append(qkv)
        for which in (1, 2):
            col = (g * 3 + which) * A_HEADS
            in_specs.append(pl.BlockSpec(
                (span, A_HEAD_DIM),
                lambda b, k, h, col=col, per_blk=per_blk: (jnp.maximum((b * nblk + k) * per_blk - 1, 0), col + h)))
            args.append(qkv)
    scratch = pltpu.VMEM((len(A_GROUPS), t_blk, A_HEAD_DIM), F32)
    return pl.pallas_call(
        _attn_a_kernel, grid=(batch, nblk, A_HEADS), in_specs=in_specs,
        out_specs=pl.BlockSpec((t_blk, A_HEAD_DIM), lambda b, k, h: (b * nblk + k, h)),
        out_shape=jax.ShapeDtypeStruct((batch * seq, hw), BF16),
        scratch_shapes=[scratch, scratch],
        compiler_params=_params(("parallel", "parallel", "parallel")),
        name="attn_a_prompt")(*args)


def _attn_a_sample_kernel(qkv_ref, c0_ref, c1_ref, c2_ref, o_ref):
    t_new = qkv_ref.shape[1]
    hw = A_HEADS * A_HEAD_DIM
    scale = 1.0 / math.sqrt(A_HEAD_DIM)
    nt = (((1,), (1,)), ((), ()))
    caches = (c0_ref, c1_ref, c2_ref)
    ti = lax.broadcasted_iota(jnp.int32, (t_new, A_BLOCK), 0)
    ni = lax.broadcasted_iota(jnp.int32, (t_new, A_BLOCK), 1)
    tq = lax.broadcasted_iota(jnp.int32, (t_new, t_new), 0)
    tj = lax.broadcasted_iota(jnp.int32, (t_new, t_new), 1)
    for h in range(A_HEADS):
        outs, lses = [], []
        for g, (win, dil) in enumerate(A_GROUPS):
            base = g * 3 * hw + h * A_HEAD_DIM
            q = qkv_ref[0, :, base:base + A_HEAD_DIM].astype(BF16)
            kn = qkv_ref[0, :, base + hw:base + hw + A_HEAD_DIM].astype(BF16)
            vn = qkv_ref[0, :, base + 2 * hw:base + 2 * hw + A_HEAD_DIM].astype(BF16)
            n_sub = min(dil, t_new)
            s_new = lax.dot_general(q, kn, nt, preferred_element_type=F32) * scale
            ok_new = jnp.logical_and(tj <= tq, jnp.bitwise_and(tq - tj, dil - 1) == 0)
            s_new = jnp.where(ok_new, s_new, -jnp.inf)
            ss, vs = [], []
            for r in range(n_sub):
                kb = caches[g][0, :, r * 2 * hw + h * A_HEAD_DIM:r * 2 * hw + (h + 1) * A_HEAD_DIM]
                vb = caches[g][0, :, r * 2 * hw + hw + h * A_HEAD_DIM:r * 2 * hw + hw + (h + 1) * A_HEAD_DIM]
                s = lax.dot_general(q, kb.astype(BF16), nt, preferred_element_type=F32) * scale
                ok = jnp.logical_and(jnp.bitwise_and(ti, dil - 1) == r, ni * dil + r >= ti)
                ss.append(jnp.where(ok, s, -jnp.inf))
                vs.append(vb.astype(BF16))
            m = jnp.max(s_new, axis=-1, keepdims=True)
            for s in ss:
                m = jnp.maximum(m, jnp.max(s, axis=-1, keepdims=True))
            e_new = jnp.exp(s_new - m)
            den = jnp.sum(e_new, axis=-1, keepdims=True)
            o = jnp.dot(e_new.astype(BF16), vn, preferred_element_type=F32)
            for s, vb in zip(ss, vs):
                e = jnp.exp(s - m)
                den = den + jnp.sum(e, axis=-1, keepdims=True)
                o = o + jnp.dot(e.astype(BF16), vb, preferred_element_type=F32)
            outs.append(o / den)
            lses.append(m + jnp.log(den))
        lm = jnp.maximum(jnp.maximum(lses[0], lses[1]), lses[2])
        ws = [jnp.exp(l - lm) for l in lses]
        tot = ws[0] + ws[1] + ws[2]
        comb = (ws[0] * outs[0] + ws[1] * outs[1] + ws[2] * outs[2]) / tot
        o_ref[0, :, h * A_HEAD_DIM:(h + 1) * A_HEAD_DIM] = comb


def attn_a_sample(qkv_s, caches, layer):
    db, t_new, width = qkv_s.shape
    hw = A_HEADS * A_HEAD_DIM
    in_specs = [pl.BlockSpec((1, t_new, width), lambda b: (b, 0, 0))]
    args = [qkv_s]
    for (win, dil), c in zip(A_GROUPS, caches):
        nl = c.shape[0]
        view = c.reshape(nl * db, win // dil, dil * 2 * hw)
        n_sub = min(dil, t_new)
        in_specs.append(pl.BlockSpec((1, win // dil, n_sub * 2 * hw),
                                     lambda b, layer=layer: (layer * db + b, 0, 0)))
        args.append(view)
    return pl.pallas_call(
        _attn_a_sample_kernel, grid=(db,), in_specs=in_specs,
        out_specs=pl.BlockSpec((1, t_new, hw), lambda b: (b, 0, 0)),
        out_shape=jax.ShapeDtypeStruct((db, t_new, hw), F32),
        compiler_params=_params(("parallel",)), name="attn_a_sample")(*args)


def _lambda(lp_ref, lam_init):
    lp = lp_ref[...]
    a = jnp.sum(lp[0:1, :] * lp[1:2, :], axis=-1, keepdims=True)
    b = jnp.sum(lp[2:3, :] * lp[3:4, :], axis=-1, keepdims=True)
    return jnp.exp(a) - jnp.exp(b) + lam_init


def _sub_ln(o, g, lam_init):
    on = o * lax.rsqrt(jnp.mean(o * o, axis=-1, keepdims=True) + SUBLN_EPS) * g
    return on * (1.0 - lam_init)


def _attn_b_kernel(q_ref, k_ref, v_ref, lp_ref, g_ref, o_ref, m_ref, l_ref, acc_ref, *, lam_init, tq, tk, sub):
    qi = pl.program_id(2)
    ki = pl.program_id(3)
    nk = pl.num_programs(3)
    nt = (((1,), (1,)), ((), ()))
    sub = min(tq, sub)

    @pl.when(ki == 0)
    def _():
        m_ref[...] = jnp.full(m_ref.shape, -jnp.inf, F32)
        l_ref[...] = jnp.zeros(l_ref.shape, F32)
        acc_ref[...] = jnp.zeros(acc_ref.shape, F32)

    def step(masked):
        v = v_ref[...].astype(BF16)
        if masked:
            rows = qi * tq + lax.broadcasted_iota(jnp.int32, (tq, tk), 0)
            cols = ki * tk + lax.broadcasted_iota(jnp.int32, (tq, tk), 1)
            mask = cols <= rows
        for c in range(2):
            sl = slice(c * B_QK_DIM, (c + 1) * B_QK_DIM)
            kc = k_ref[:, sl].astype(BF16)
            for r in range(tq // sub):
                rs = slice(r * sub, (r + 1) * sub)
                s = lax.dot_general(q_ref[rs, sl], kc, nt, preferred_element_type=F32)
                if masked:
                    s = jnp.where(mask[rs, :], s, -jnp.inf)
                m_old = m_ref[c, rs, :]
                m_new = jnp.maximum(m_old, jnp.max(s, axis=-1, keepdims=True))
                alpha = jnp.exp2(m_old - m_new)
                p = jnp.exp2(s - m_new)
                l_ref[c, rs, :] = alpha * l_ref[c, rs, :] + jnp.sum(p, axis=-1, keepdims=True)
                acc_ref[c, rs, :] = (alpha * acc_ref[c, rs, :]
                                     + jnp.dot(p.astype(BF16), v, preferred_element_type=F32))
                m_ref[c, rs, :] = m_new

    first_row = qi * tq
    last_col = ki * tk + tk - 1

    @pl.when(last_col <= first_row)
    def _():
        step(False)

    @pl.when(jnp.logical_and(last_col > first_row, ki * tk <= first_row + tq - 1))
    def _():
        step(True)

    @pl.when(ki == nk - 1)
    def _():
        lam = _lambda(lp_ref, lam_init)
        o = acc_ref[0] / l_ref[0] - lam * (acc_ref[1] / l_ref[1])
        o_ref[...] = _sub_ln(o, g_ref[...], lam_init).astype(BF16)


def attn_b_prompt(q, kv, lp, g_sub, lam_init, batch, seq, *, tq=1024, tk=1024, sub=128):
    tq = min(tq, seq)
    tk = min(tk, seq)
    nq, nk = seq // tq, seq // tk
    hd = 2 * B_QK_DIM

    def kmap(b, h, i, j):
        return (b * nk + jnp.minimum(j, (i * tq + tq - 1) // tk), h)

    def vmap_(b, h, i, j):
        return (b * nk + jnp.minimum(j, (i * tq + tq - 1) // tk), B_HEADS + h)

    kern = functools.partial(_attn_b_kernel, lam_init=lam_init, tq=tq, tk=tk, sub=sub)
    return pl.pallas_call(
        kern, grid=(batch, B_HEADS, nq, nk),
        in_specs=[pl.BlockSpec((tq, hd), lambda b, h, i, j: (b * nq + i, h)),
                  pl.BlockSpec((tk, hd), kmap),
                  pl.BlockSpec((tk, B_V_DIM), vmap_),
                  pl.BlockSpec((4, B_QK_DIM), lambda b, h, i, j: (0, 0)),
                  pl.BlockSpec((1, B_V_DIM), lambda b, h, i, j: (0, 0))],
        out_specs=pl.BlockSpec((tq, B_V_DIM), lambda b, h, i, j: (b * nq + i, h)),
        out_shape=jax.ShapeDtypeStruct((batch * seq, B_HEADS * B_V_DIM), BF16),
        scratch_shapes=[pltpu.VMEM((2, tq, 1), F32), pltpu.VMEM((2, tq, 1), F32),
                        pltpu.VMEM((2, tq, B_V_DIM), F32)],
        compiler_params=_params(("parallel", "parallel", "parallel", "arbitrary")),
        name="attn_b_prompt")(q, kv, kv, lp, g_sub.reshape(1, B_V_DIM))


def _attn_b_sample_kernel(*refs, lam_init, t_new, n_par):
    pt_ref, q_ref = refs[0], refs[1]
    page_refs = refs[2:2 + 2 * n_par]
    kvn_ref, lp_ref, g_ref, o_ref, m_ref, l_ref, acc_ref, xs_ref = refs[2 + 2 * n_par:]
    p = pl.program_id(1)
    n_steps = pl.num_programs(1)
    kw = B_HEADS * 2 * B_QK_DIM
    nt = (((1,), (1,)), ((), ()))
    rph = 2 * t_new

    @pl.when(p == 0)
    def _():
        m_ref[...] = jnp.full(m_ref.shape, -jnp.inf, F32)
        l_ref[...] = jnp.zeros(l_ref.shape, F32)
        acc_ref[...] = jnp.zeros(acc_ref.shape, F32)

    def update(state, s, weigh):
        m_old, l_old, acc_old = state
        m_new = jnp.maximum(m_old, jnp.max(s, axis=-1, keepdims=True))
        alpha = jnp.exp2(m_old - m_new)
        e = jnp.exp2(s - m_new)
        return m_new, alpha * l_old + jnp.sum(e, axis=-1, keepdims=True), alpha * acc_old + weigh(e)

    half_heads = B_HEADS // 2
    rows_per_page = 4 * PAGE_SIZE
    for g in range(half_heads):
        for half in range(2):
            for k in range(n_par):
                xs_ref[g, half, k * rows_per_page:(k + 1) * rows_per_page, :] = (
                    page_refs[2 * k + half][0, pl.ds(g, rows_per_page, stride=half_heads), :].astype(BF16))

    def pair_rows(x, g):
        return jnp.concatenate([x[g * rph:(g + 1) * rph], x[(g + half_heads) * rph:(g + half_heads + 1) * rph]],
                               axis=0)

    q_all = q_ref[0]
    lo, hi = [None] * half_heads, [None] * half_heads
    for g in range(half_heads):
        qg = pair_rows(q_all, g)
        sg = (lax.dot_general(qg[:, 0:B_QK_DIM], xs_ref[g, 0], nt, preferred_element_type=F32)
              + lax.dot_general(qg[:, B_QK_DIM:], xs_ref[g, 1], nt, preferred_element_type=F32))
        lo[g], hi[g] = sg[0:rph], sg[rph:2 * rph]
    s = jnp.concatenate(lo + hi, axis=0)
    kind = jnp.bitwise_and(lax.broadcasted_iota(jnp.int32, s.shape, 1), 3)
    own = (lax.broadcasted_iota(jnp.int32, s.shape, 0) >= half_heads * rph).astype(jnp.int32)
    s = jnp.where(kind == own, s, -jnp.inf)

    def weigh(e):
        ev = pltpu.roll(e, 2, 1)
        lo_o, hi_o = [None] * half_heads, [None] * half_heads
        for g in range(half_heads):
            eg = pair_rows(ev, g).astype(BF16)
            og = jnp.concatenate([jnp.dot(eg, xs_ref[g, 0], preferred_element_type=F32),
                                  jnp.dot(eg, xs_ref[g, 1], preferred_element_type=F32)], axis=1)
            lo_o[g], hi_o[g] = og[0:rph], og[rph:2 * rph]
        return jnp.concatenate(lo_o + hi_o, axis=0)

    m_new, l_new, acc_new = update((m_ref[...], l_ref[...], acc_ref[...]), s, weigh)
    m_ref[...] = m_new
    l_ref[...] = l_new
    acc_ref[...] = acc_new

    @pl.when(p == n_steps - 1)
    def _():
        lam = _lambda(lp_ref, lam_init)
        for h in range(B_HEADS):
            rs = slice(h * rph, (h + 1) * rph)
            qh = q_ref[0, rs, :]
            kn = kvn_ref[0, :, h * B_V_DIM:(h + 1) * B_V_DIM].astype(BF16)
            vn = kvn_ref[0, :, kw + h * B_V_DIM:kw + (h + 1) * B_V_DIM].astype(BF16)
            sn = lax.dot_general(qh, kn, nt, preferred_element_type=F32)
            rq = jnp.bitwise_and(lax.broadcasted_iota(jnp.int32, sn.shape, 0), t_new - 1)
            cj = lax.broadcasted_iota(jnp.int32, sn.shape, 1)
            _, l_fin, acc_fin = update(
                (m_ref[rs, :], l_ref[rs, :], acc_ref[rs, :]), jnp.where(cj <= rq, sn, -jnp.inf),
                lambda e, vn=vn: jnp.dot(e.astype(BF16), vn, preferred_element_type=F32))
            on = acc_fin / l_fin
            o = on[0:t_new, :] - lam * on[t_new:2 * t_new, :]
            o_ref[0, :, h * B_V_DIM:(h + 1) * B_V_DIM] = _sub_ln(o, g_ref[...], lam_init)


def attn_b_sample(q_s, kv_s, cache_b_kv, page_table, lp, g_sub, lam_init):
    db, t_new, qw = q_s.shape
    n_pages = page_table.shape[1]
    n_phys = cache_b_kv.shape[0]
    assert t_new & (t_new - 1) == 0
    n_par = next(k for k in (8, 4, 2, 1) if n_pages % k == 0)
    kvw = 2 * B_HEADS * B_V_DIM
    page_rows = PAGE_SIZE * 2 * B_HEADS
    pages = cache_b_kv.reshape(n_phys, page_rows, B_V_DIM)
    q5 = q_s.reshape(db, t_new, B_HEADS, 2, B_QK_DIM).transpose(0, 2, 3, 1, 4)
    eye = jnp.eye(2, dtype=q_s.dtype)
    qbd = q5[:, :, :, :, None, :] * eye[None, None, :, None, :, None]
    rows = B_HEADS * 2 * t_new
    qbd = qbd.reshape(db, rows, 2 * B_QK_DIM).astype(BF16)
    kern = functools.partial(_attn_b_sample_kernel, lam_init=lam_init, t_new=t_new, n_par=n_par)
    page_specs = [pl.BlockSpec((1, page_rows, B_QK_DIM),
                               lambda b, p, pt, k=k, half=half: (pt[b, p * n_par + k], 0, half))
                  for k in range(n_par) for half in range(2)]
    grid_spec = pltpu.PrefetchScalarGridSpec(
        num_scalar_prefetch=1, grid=(db, n_pages // n_par),
        in_specs=[pl.BlockSpec((1, rows, 2 * B_QK_DIM), lambda b, p, pt: (b, 0, 0))] + page_specs + [
                  pl.BlockSpec((1, t_new, kvw), lambda b, p, pt: (b, 0, 0)),
                  pl.BlockSpec((4, B_QK_DIM), lambda b, p, pt: (0, 0)),
                  pl.BlockSpec((1, B_V_DIM), lambda b, p, pt: (0, 0))],
        out_specs=pl.BlockSpec((1, t_new, qw), lambda b, p, pt: (b, 0, 0)),
        scratch_shapes=[pltpu.VMEM((rows, 1), F32), pltpu.VMEM((rows, 1), F32),
                        pltpu.VMEM((rows, B_V_DIM), F32),
                        pltpu.VMEM((B_HEADS // 2, 2, n_par * 4 * PAGE_SIZE, B_QK_DIM), BF16)])
    return pl.pallas_call(
        kern, grid_spec=grid_spec,
        out_shape=jax.ShapeDtypeStruct((db, t_new, qw), F32),
        compiler_params=_params(("parallel", "arbitrary")),
        name="attn_b_sample")(page_table, qbd, *([pages] * (2 * n_par)), kv_s, lp, g_sub.reshape(1, B_V_DIM))


def _top_ranks(s, exact):
    iota = lax.broadcasted_iota(jnp.int32, s.shape, 0).astype(F32)
    rank = jnp.full(s.shape, NOT_SELECTED, F32)
    vals = []
    work = s
    for a in range(PEER_TOPK):
        m = jnp.max(work, axis=0, keepdims=True)
        if exact:
            idx = jnp.min(jnp.where(work == m, iota, float(N_KEYS)), axis=0, keepdims=True)
            hit = iota == idx
        else:
            hit = work == m
        rank = jnp.where(hit, float(a), rank)
        work = jnp.where(hit, -jnp.inf, work)
        vals.append(m)
    taken = jnp.sum(jnp.where(rank < NOT_SELECTED, 1.0, 0.0), axis=0, keepdims=True)
    return vals, rank, taken


def _candidate_rows():
    pieces = [(0, 0, 16, 16)]
    for a in range(1, 8):
        pieces.append((a, 0, 8, PEER_TOPK // (a + 1)))
    return pieces


def _route_kernel(ht_ref, wq_ref, sk_ref, r1_ref, e1_ref, bq_out_ref, c0_out_ref, qt_ref, bq_ref, c0_ref):
    n_chunks = ht_ref.shape[1] // LANES
    qt_ref[...] = jnp.dot(wq_ref[...], ht_ref[...], preferred_element_type=F32)
    neg = -jnp.inf
    t = LANES

    def route_chunk(h, cs, scores, exact):
        vals, ranks = [], []
        off = jnp.zeros((1, t), F32)
        for c in range(2):
            v, r, taken = _top_ranks(scores[c], exact)
            vals.append(v); ranks.append(r)
            off = off + jnp.abs(taken - float(PEER_TOPK))
        v0, v1 = vals
        v1_16 = jnp.concatenate(v1, axis=0)
        v0_hi = jnp.concatenate(v0[8:16], axis=0)
        cands, flats = [], []
        for a, _, rows, nvalid in _candidate_rows():
            b_iota = lax.broadcasted_iota(jnp.int32, (rows, t), 0)
            cs_ab = v0[a] + v1_16[0:rows, :]
            cands.append(jnp.where(b_iota < nvalid, cs_ab, neg))
            flats.append((b_iota + a * PEER_TOPK).astype(F32))
        cands.append(v0_hi + v1[0])
        flats.append(((lax.broadcasted_iota(jnp.int32, (8, t), 0) + 8) * PEER_TOPK).astype(F32))
        cand = jnp.concatenate(cands, axis=0)
        flat = jnp.concatenate(flats, axis=0)
        big = float(PEER_TOPK * PEER_TOPK)
        work = cand
        sel = jnp.zeros(cand.shape, F32)
        for _ in range(PEER_TOPK):
            m = jnp.max(work, axis=0, keepdims=True)
            if exact:
                idx = jnp.min(jnp.where(work == m, flat, big), axis=0, keepdims=True)
                hit = flat == idx
            else:
                hit = work == m
            sel = jnp.where(hit, 1.0, sel)
            work = jnp.where(hit, neg, work)
        off = off + jnp.abs(jnp.sum(sel, axis=0, keepdims=True) - float(PEER_TOPK))
        top = v0[0] + v1[0]
        z = jnp.sum(jnp.where(sel > 0.0, jnp.exp(cand - top), 0.0), axis=0, keepdims=True)
        counts = [jnp.sum(sel[0:16, :], axis=0, keepdims=True)]
        for k in range(1, 8):
            counts.append(jnp.sum(sel[8 + 8 * k:16 + 8 * k, :], axis=0, keepdims=True))
        hi = sel[72:80, :]
        bq = jnp.zeros(ranks[0].shape, F32)
        for a in range(PEER_TOPK):
            cnt = counts[a] if a < 8 else hi[a - 8:a - 7, :]
            bq = jnp.where(ranks[0] == float(a), cnt, bq)
        rs = pl.ds(pl.multiple_of(h * N_KEYS, N_KEYS), N_KEYS)
        bq_ref[rs, cs] = bq
        c0_ref[rs, cs] = jnp.exp(scores[0] - v0[0]) * (0.5 / z)
        r1_ref[rs, cs] = ranks[1].astype(BF16)
        e1_ref[rs, cs] = jnp.exp(scores[1] - v1[0]).astype(BF16)
        return off

    def body(h, carry):
        work = []
        for k in range(n_chunks):
            cs = slice(k * LANES, (k + 1) * LANES)
            scores = []
            for c in range(2):
                hc = h * 2 + c
                qhc = qt_ref[pl.ds(pl.multiple_of(hc * N_KEYS, N_KEYS), N_KEYS), cs].astype(BF16)
                scores.append(jnp.dot(sk_ref[hc].astype(BF16), qhc, preferred_element_type=F32))
            work.append((cs, scores, route_chunk(h, cs, scores, exact=False)))
        for cs, scores, off in work:
            @pl.when(jnp.max(off) > 0.0)
            def _(cs=cs, scores=scores):
                route_chunk(h, cs, scores, exact=True)

        return carry

    lax.fori_loop(0, PEER_HEADS, body, 0)
    for h in range(PEER_HEADS):
        bq_out_ref[:, h, :] = bq_ref[h * N_KEYS:(h + 1) * N_KEYS, :]
        c0_out_ref[:, h, :] = c0_ref[h * N_KEYS:(h + 1) * N_KEYS, :]


def peer_route(ht, wq_t, subkeys, *, tm=512):
    d, n = ht.shape
    rows = PEER_HEADS * N_KEYS
    slab = pl.BlockSpec((rows, tm), lambda i: (0, i))
    sds = jax.ShapeDtypeStruct((rows, n), BF16)
    slab3 = pl.BlockSpec((N_KEYS, PEER_HEADS, tm), lambda i: (0, 0, i))
    sds3 = jax.ShapeDtypeStruct((N_KEYS, PEER_HEADS, n), F32)
    return pl.pallas_call(
        _route_kernel, grid=(n // tm,),
        in_specs=[pl.BlockSpec((d, tm), lambda i: (0, i)),
                  pl.BlockSpec(wq_t.shape, lambda i: (0, 0)),
                  pl.BlockSpec(subkeys.shape, lambda i: (0, 0, 0))],
        out_specs=[slab, slab, slab3, slab3], out_shape=[sds, sds, sds3, sds3],
        scratch_shapes=[pltpu.VMEM((wq_t.shape[0], tm), F32), pltpu.VMEM((rows, tm), F32),
                        pltpu.VMEM((rows, tm), F32)],
        compiler_params=_params(("parallel",)), name="peer_route")(ht, wq_t, subkeys)


def _gelu_x2(a):
    return a * (1.0 + lax.erf(a * math.sqrt(0.5)))


def _expert_kernel(x_ref, ht_ref, u_ref, v_ref, r1_ref, e1_ref, bq_ref, c0_ref, o_ref, *a_refs, te, tm, ge):
    e = pl.program_id(1)
    rows_per_group = ge // N_KEYS
    n_chunks = tm // LANES
    packed = 2 * SUBLANES
    tiles = N_KEYS // packed
    nn = (((1,), (0,)), ((), ()))

    @pl.when(e == 0)
    def _():
        o_ref[...] = x_ref[...]

    n_groups = te // ge
    for k in range(n_groups):
        a_refs[k][...] = lax.dot_general(u_ref[k * ge:(k + 1) * ge, :], ht_ref[...], nn,
                                         preferred_element_type=F32)
    groups = []
    for k in range(n_groups):
        rows = []
        for r in range(rows_per_group):
            i = e * (te // N_KEYS) + k * rows_per_group + r
            cols = []
            for c in range(n_chunks):
                cs = slice(c * LANES, (c + 1) * LANES)
                bq_all = bq_ref[i, :, cs]
                c0_all = c0_ref[i, :, cs]
                w = [jnp.zeros((packed, LANES), BF16)] * tiles
                for h in range(PEER_HEADS):
                    bq = jnp.broadcast_to(bq_all[h:h + 1, :], (packed, LANES)).astype(BF16)
                    c0 = jnp.broadcast_to(c0_all[h:h + 1, :], (packed, LANES)).astype(BF16)
                    for t in range(tiles):
                        js = slice(h * N_KEYS + t * packed, h * N_KEYS + (t + 1) * packed)
                        e1 = e1_ref[js, cs]
                        w[t] = w[t] + jnp.where(r1_ref[js, cs] < bq, e1 * c0, jnp.zeros_like(e1))
                gate = jnp.concatenate(w, axis=0).astype(F32)
                cols.append(_gelu_x2(a_refs[k][r * N_KEYS:(r + 1) * N_KEYS, cs]) * gate)
            rows.append(jnp.concatenate(cols, axis=1))
        groups.append(jnp.concatenate(rows, axis=0).T)
    g = jnp.concatenate(groups, axis=1)
    o_ref[...] += lax.dot_general(g, v_ref[...], nn, preferred_element_type=F32)


def peer_experts(x, ht, u, v_bf, layer, slabs, *, tm=512, te=1024, ge=256):
    d, n = ht.shape
    n_exp = u.shape[1]
    while n % tm:
        tm //= 2
    te = min(te, n_exp)
    rows = PEER_HEADS * N_KEYS
    once = pl.Buffered(1)
    slab = pl.BlockSpec((rows, tm), lambda i, e: (0, i), pipeline_mode=once)
    slab3 = pl.BlockSpec((N_KEYS, PEER_HEADS, tm), lambda i, e: (0, 0, i), pipeline_mode=once)
    kern = functools.partial(_expert_kernel, te=te, tm=tm, ge=ge)
    return pl.pallas_call(
        kern, grid=(n // tm, n_exp // te),
        in_specs=[pl.BlockSpec((tm, d), lambda i, e: (i, 0), pipeline_mode=once),
                  pl.BlockSpec((d, tm), lambda i, e: (0, i), pipeline_mode=once),
                  pl.BlockSpec((None, te, d), lambda i, e: (layer, e, 0)),
                  pl.BlockSpec((None, te, d), lambda i, e: (layer, e, 0)),
                  slab, slab, slab3, slab3],
        out_specs=pl.BlockSpec((tm, d), lambda i, e: (i, 0)),
        out_shape=jax.ShapeDtypeStruct((n, d), F32),
        scratch_shapes=[pltpu.VMEM((ge, tm), F32) for _ in range(te // ge)],
        compiler_params=_params(("parallel", "arbitrary")),
        name="peer_experts")(x, ht, u, v_bf, *slabs)


def _rope_table(pos):
    rot = A_HEAD_DIM // 4
    half = rot // 2
    inv_freq = ROPE_THETA ** (-jnp.arange(half, dtype=F32) / half)
    ang = pos.astype(F32)[:, None] * inv_freq[None, :]
    cos, sin = jnp.cos(ang), jnp.sin(ang)
    n = pos.shape[0]
    ones = jnp.ones((n, A_HEAD_DIM - rot), F32)
    zeros = jnp.zeros((n, A_HEAD_DIM - rot), F32)
    zh = jnp.zeros((n, half), F32)
    return jnp.concatenate([cos, cos, ones, -sin, zh, zeros, zh, sin, zeros], axis=1)


def kernel(x_prompt, x_sample, cache_a_w128, cache_a_w512, cache_a_w2048, cache_b_kv, page_table,
           p_prompt, p_sample, norm_mix, norm_ffn, norm_ple, norm_kv, norm_final,
           w_qkv_a, w_o_a, w_kv_b, w_q_b, diff_lambda, norm_sub_b, w_o_b,
           peer_wq, peer_subkeys, peer_u, peer_v, w_ple, w_ple_gate):
    batch, seq, d = x_prompt.shape
    db, t_new, _ = x_sample.shape
    depth = norm_mix.shape[0]
    n_a = w_qkv_a.shape[0]
    past_len = page_table.shape[1] * PAGE_SIZE
    n_p, n_s = batch * seq, db * t_new
    n_tot = -(-(n_p + n_s) // TOKEN_PAD) * TOKEN_PAD
    pad = n_tot - n_p - n_s
    a_caches = (cache_a_w128, cache_a_w512, cache_a_w2048)
    hw = A_HEADS * A_HEAD_DIM

    def tokens(prompt_part, sample_part):
        w = prompt_part.shape[-1]
        return jnp.concatenate([prompt_part.reshape(n_p, w), sample_part.reshape(n_s, w),
                                jnp.zeros((pad, w), prompt_part.dtype)], axis=0)

    x = tokens(x_prompt, x_sample)
    pos = jnp.concatenate([jnp.tile(jnp.arange(seq, dtype=jnp.int32), batch),
                           jnp.tile(past_len + jnp.arange(t_new, dtype=jnp.int32), db),
                           jnp.zeros((pad,), jnp.int32)])
    tab = _rope_table(pos)
    v_bf = peer_v.astype(BF16)

    a_rows_p = [[] for _ in A_GROUPS]
    a_rows_s = [[] for _ in A_GROUPS]
    new_b_kv_prompt = new_b_kv_sample = kv = None
    for i in range(depth):
        h = rms_norm(x, norm_mix[i], out_h=True)[0]
        if i < n_a:
            qkv = matmul(h, w_qkv_a, layer=i, mode="rope", tab=tab, tn=hw,
                         rope_fn=lambda col: (col // hw) % 3 != 2)
            o_p = attn_a_prompt(qkv, batch, seq)
            qkv_s = qkv[n_p:n_p + n_s].reshape(db, t_new, qkv.shape[1])
            o_s = attn_a_sample(qkv_s, a_caches, i).reshape(n_s, hw).astype(BF16)
            o_all = jnp.concatenate([o_p, o_s, jnp.zeros((pad, hw), BF16)], axis=0)
            x = matmul(o_all, w_o_a, layer=i, mode="res", res=x)
            for g, (win, dil) in enumerate(A_GROUPS):
                wb = min(win, seq)
                c0, c1 = (g * 3 + 1) * hw, (g * 3 + 3) * hw
                rows = jnp.stack([qkv[(b + 1) * seq - wb:(b + 1) * seq, c0:c1] for b in range(batch)], axis=0)
                a_rows_p[g].append(rows.reshape(batch, wb, 2, A_HEADS, A_HEAD_DIM))
                a_rows_s[g].append(qkv[n_p:n_p + n_s, c0:c1].reshape(db, t_new, 2, A_HEADS, A_HEAD_DIM))
        else:
            j = i - n_a
            if j == 0:
                hkv = rms_norm(x, norm_kv, out_h=True)[0]
                kw = B_HEADS * 2 * B_QK_DIM
                kv = matmul(hkv, w_kv_b, mode="rope", tab=tab, tn=1024, rope_fn=lambda col: col < kw)
                new_b_kv_prompt = kv[:n_p].reshape(batch, seq, 2, B_HEADS, B_V_DIM)
                new_b_kv_sample = kv[n_p:n_p + n_s].reshape(db, t_new, 2, B_HEADS, B_V_DIM)
            lam_init = 0.8 - 0.6 * math.exp(-0.3 * i)
            q = matmul(h, w_q_b, layer=j, mode="rope", tab=tab, tn=1024, rope_fn=lambda col: col >= 0,
                       out_scale=math.log2(math.e) / math.sqrt(B_QK_DIM), out_dtype=BF16)
            o_p = attn_b_prompt(q, kv, diff_lambda[j], norm_sub_b[j], lam_init, batch, seq)
            q_s = q[n_p:n_p + n_s].reshape(db, t_new, q.shape[1])
            kv_s = kv[n_p:n_p + n_s].reshape(db, t_new, kv.shape[1])
            o_s = attn_b_sample(q_s, kv_s, cache_b_kv, page_table, diff_lambda[j], norm_sub_b[j], lam_init)
            o_all = jnp.concatenate([o_p, o_s.reshape(n_s, -1).astype(BF16),
                                     jnp.zeros((pad, o_p.shape[1]), BF16)], axis=0)
            x = matmul(o_all, w_o_b, layer=j, mode="res", res=x)
        ht = rms_norm(x, norm_ffn[i], out_ht=True)[0]
        wq_t = peer_wq[i].T.astype(BF16)
        sk = peer_subkeys[i].reshape(PEER_HEADS * 2, N_KEYS, -1)
        slabs = peer_route(ht, wq_t, sk)
        x = peer_experts(x, ht, peer_u, v_bf, i, slabs)
        hn = rms_norm(x, norm_ple[i], out_h=True)[0]
        x = ple(x, tokens(p_prompt[i], p_sample[i]), hn, w_ple, w_ple_gate, i)

    y = rms_norm(x, norm_final, out_y=True)[0]
    y_prompt = y[:n_p].reshape(batch, seq, d)
    y_sample = y[n_p:n_p + n_s].reshape(db, t_new, d)
    outs_p = [jnp.stack(r, axis=0) for r in a_rows_p]
    outs_s = [jnp.stack(r, axis=0) for r in a_rows_s]
    return (y_prompt, y_sample, *outs_p, *outs_s, new_b_kv_prompt, new_b_kv_sample)
```

```python
import functools
import math

import jax
import jax.numpy as jnp
from jax import lax
from jax.experimental import pallas as pl
from jax.experimental.pallas import tpu as pltpu

BF16 = jnp.bfloat16
F32 = jnp.float32

ROPE_THETA = 500000.0
NORM_EPS = 1e-6
SUBLN_EPS = 1e-5
A_GROUPS = ((128, 1), (512, 4), (2048, 16))
A_HEADS = 8
A_HEAD_DIM = 128
A_BLOCK = 128
B_HEADS = 8
B_QK_DIM = 128
B_V_DIM = 256
PEER_HEADS = 8
N_KEYS = 128
PEER_TOPK = 16
PAGE_SIZE = 128

LANES = 128
SUBLANES = 8
VMEM_LIMIT = 56 * 1024 * 1024
TOKEN_PAD = 512
MATMUL_ROW_CAP = 1100
NOT_SELECTED = 99.0


def _params(sem):
    return pltpu.CompilerParams(dimension_semantics=sem, vmem_limit_bytes=VMEM_LIMIT)


def _token_tile(n, cap):
    packed = 2 * SUBLANES
    for parts in range(1, n // packed + 1):
        if n % parts == 0 and (n // parts) % packed == 0 and n // parts <= cap:
            return n // parts
    raise ValueError(f"no token tile for {n} rows under {cap}")


def _rms_kernel(x_ref, g_ref, *out_refs, eps, out_h, out_ht, out_y):
    it = iter(out_refs)
    x = x_ref[...]
    y = x * lax.rsqrt(jnp.mean(x * x, axis=-1, keepdims=True) + eps) * g_ref[...]
    if out_h:
        next(it)[...] = y.astype(BF16)
    if out_ht:
        next(it)[...] = y.T.astype(BF16)
    if out_y:
        next(it)[...] = y


def rms_norm(x, g, *, out_h=False, out_ht=False, out_y=False, eps=NORM_EPS, tm=512):
    n, d = x.shape
    grid = (n // tm,)
    row = pl.BlockSpec((tm, d), lambda i: (i, 0))
    col = pl.BlockSpec((d, tm), lambda i: (0, i))
    in_specs = [row, pl.BlockSpec((1, d), lambda i: (0, 0))]
    args = [x, g.reshape(1, d)]
    out_shape, out_specs = [], []
    if out_h:
        out_shape.append(jax.ShapeDtypeStruct((n, d), BF16)); out_specs.append(row)
    if out_ht:
        out_shape.append(jax.ShapeDtypeStruct((d, n), BF16)); out_specs.append(col)
    if out_y:
        out_shape.append(jax.ShapeDtypeStruct((n, d), F32)); out_specs.append(row)
    kern = functools.partial(_rms_kernel, eps=eps, out_h=out_h, out_ht=out_ht, out_y=out_y)
    return pl.pallas_call(kern, grid=grid, in_specs=in_specs, out_specs=out_specs,
                          out_shape=out_shape, compiler_params=_params(("parallel",)),
                          name="rms_norm")(*args)


def _rope_tile(y, tab):
    c = tab[:, 0:LANES]
    s1 = tab[:, LANES:2 * LANES]
    s2 = tab[:, 2 * LANES:3 * LANES]
    outs = []
    for g in range(y.shape[1] // LANES):
        yg = y[:, g * LANES:(g + 1) * LANES]
        outs.append(yg * c + pltpu.roll(yg, LANES - 16, 1) * s1 + pltpu.roll(yg, 16, 1) * s2)
    return jnp.concatenate(outs, axis=1) if len(outs) > 1 else outs[0]


def _mm_kernel(*refs, mode, rope_fn, tn, out_scale):
    if mode == "rope":
        x_ref, w_ref, tab_ref, o_ref, wb_ref = refs
    elif mode == "res":
        x_ref, w_ref, r_ref, o_ref, wb_ref = refs
    else:
        x_ref, w_ref, o_ref, wb_ref = refs
    j = pl.program_id(0)

    @pl.when(pl.program_id(1) == 0)
    def _():
        wb_ref[...] = w_ref[...].astype(BF16)

    y = jnp.dot(x_ref[...], wb_ref[...], preferred_element_type=F32)
    if mode == "rope":
        roped = rope_fn(j * tn)

        def finish(v):
            return (v if out_scale == 1.0 else v * out_scale).astype(o_ref.dtype)

        @pl.when(roped)
        def _():
            o_ref[...] = finish(_rope_tile(y, tab_ref[...]))

        @pl.when(jnp.logical_not(roped))
        def _():
            o_ref[...] = finish(y)
    elif mode == "res":
        o_ref[...] = r_ref[...] + y
    else:
        o_ref[...] = y


def matmul(x, w, *, mode="plain", tab=None, res=None, rope_fn=None, tm=MATMUL_ROW_CAP, tn=512,
           out_scale=1.0, out_dtype=F32, layer=None):
    n, k = x.shape
    m = w.shape[-1]
    tm = _token_tile(n, tm)
    tn = min(tn, m)
    grid = (m // tn, n // tm)
    if layer is None:
        w_spec = pl.BlockSpec((k, tn), lambda j, i: (0, j))
    else:
        w_spec = pl.BlockSpec((None, k, tn), lambda j, i: (layer, 0, j))
    in_specs = [pl.BlockSpec((tm, k), lambda j, i: (i, 0)), w_spec]
    args = [x, w]
    if mode == "rope":
        in_specs.append(pl.BlockSpec((tm, 3 * LANES), lambda j, i: (i, 0)))
        args.append(tab)
    elif mode == "res":
        in_specs.append(pl.BlockSpec((tm, tn), lambda j, i: (i, j)))
        args.append(res)
    assert mode == "rope" or (out_scale == 1.0 and out_dtype == F32)
    kern = functools.partial(_mm_kernel, mode=mode, rope_fn=rope_fn, tn=tn, out_scale=out_scale)
    return pl.pallas_call(
        kern, grid=grid, in_specs=in_specs,
        out_specs=pl.BlockSpec((tm, tn), lambda j, i: (i, j)),
        out_shape=jax.ShapeDtypeStruct((n, m), out_dtype),
        scratch_shapes=[pltpu.VMEM((k, tn), BF16)],
        compiler_params=_params(("arbitrary", "arbitrary")),
        name="matmul_" + mode)(*args)


def _ple_kernel(x_ref, p_ref, hn_ref, wp_ref, wg_ref, o_ref, wgb_ref):
    @pl.when(pl.program_id(1) == 0)
    def _():
        wgb_ref[...] = wg_ref[...].astype(BF16)

    gate = jax.nn.sigmoid(jnp.dot(hn_ref[...], wgb_ref[...], preferred_element_type=F32))
    up = jnp.dot(p_ref[...].astype(BF16), wp_ref[...].astype(BF16), preferred_element_type=F32)
    o_ref[...] = x_ref[...] + up * gate


def ple(x, p, hn, w_p, w_gate, layer, *, tm=MATMUL_ROW_CAP, tn=512):
    n, d = x.shape
    kp = p.shape[1]
    tn = min(tn, d)
    tm = _token_tile(n, tm)
    grid = (d // tn, n // tm)
    return pl.pallas_call(
        _ple_kernel, grid=grid,
        in_specs=[pl.BlockSpec((tm, tn), lambda j, i: (i, j)),
                  pl.BlockSpec((tm, kp), lambda j, i: (i, 0)),
                  pl.BlockSpec((tm, d), lambda j, i: (i, 0)),
                  pl.BlockSpec((None, kp, tn), lambda j, i: (layer, 0, j)),
                  pl.BlockSpec((None, d, tn), lambda j, i: (layer, 0, j))],
        out_specs=pl.BlockSpec((tm, tn), lambda j, i: (i, j)),
        out_shape=jax.ShapeDtypeStruct((n, d), F32),
        scratch_shapes=[pltpu.VMEM((d, tn), BF16)],
        compiler_params=_params(("arbitrary", "arbitrary")),
        name="ple")(x, p, hn, w_p, w_gate)


def _attn_a_kernel(*refs):
    n_in = 5 * len(A_GROUPS)
    o_ref, og_ref, lg_ref = refs[n_in:n_in + 3]
    blk = pl.program_id(1)
    t_blk = o_ref.shape[0]
    qi = lax.broadcasted_iota(jnp.int32, (A_BLOCK, A_BLOCK), 0)
    kj = lax.broadcasted_iota(jnp.int32, (A_BLOCK, A_BLOCK), 1)
    far = kj >= qi
    near = kj <= qi
    scale = math.log2(math.e) / math.sqrt(A_HEAD_DIM)
    nt = (((1,), (1,)), ((), ()))
    for g, (win, dil) in enumerate(A_GROUPS):
        q_ref, kc_ref, vc_ref, kp_ref, vp_ref = refs[5 * g:5 * g + 5]
        span = A_BLOCK * dil

        def body(it, carry, g=g, dil=dil, span=span, q_ref=q_ref, kc_ref=kc_ref, vc_ref=vc_ref,
                 kp_ref=kp_ref, vp_ref=vp_ref):
            n = it // dil
            r = it % dil
            start = n * span + r
            rows = pl.ds(start, A_BLOCK, stride=dil)
            before = pl.ds(jnp.maximum(start - span, 0), A_BLOCK, stride=dil)
            outside = pl.ds(r, A_BLOCK, stride=dil)
            first = n == 0
            q = q_ref[rows, :].astype(BF16)
            kc = kc_ref[rows, :].astype(BF16)
            vc = vc_ref[rows, :].astype(BF16)
            if span == t_blk:
                kp = kp_ref[outside, :].astype(BF16)
                vp = vp_ref[outside, :].astype(BF16)
            else:
                kp = jnp.where(first, kp_ref[outside, :], kc_ref[before, :]).astype(BF16)
                vp = jnp.where(first, vp_ref[outside, :], vc_ref[before, :]).astype(BF16)
            has_prev = jnp.logical_or(n > 0, blk > 0)
            sp = lax.dot_general(q, kp, nt, preferred_element_type=F32) * scale
            sc = lax.dot_general(q, kc, nt, preferred_element_type=F32) * scale
            sp = jnp.where(jnp.logical_and(far, has_prev), sp, -jnp.inf)
            sc = jnp.where(near, sc, -jnp.inf)
            m = jnp.maximum(jnp.max(sp, axis=-1, keepdims=True), jnp.max(sc, axis=-1, keepdims=True))
            ep = jnp.exp2(sp - m)
            ec = jnp.exp2(sc - m)
            den = jnp.sum(ep, axis=-1, keepdims=True) + jnp.sum(ec, axis=-1, keepdims=True)
            o = (jnp.dot(ep.astype(BF16), vp, preferred_element_type=F32)
                 + jnp.dot(ec.astype(BF16), vc, preferred_element_type=F32))
            og_ref[g, rows, :] = o / den
            lg_ref[g, rows, :] = jnp.broadcast_to(m + jnp.log2(den), (A_BLOCK, A_HEAD_DIM))
            return carry

        lax.fori_loop(0, t_blk // A_BLOCK, body, 0, unroll=8)
    l0, l1, l2 = lg_ref[0], lg_ref[1], lg_ref[2]
    m = jnp.maximum(jnp.maximum(l0, l1), l2)
    e0, e1, e2 = jnp.exp2(l0 - m), jnp.exp2(l1 - m), jnp.exp2(l2 - m)
    out = (e0 * og_ref[0] + e1 * og_ref[1] + e2 * og_ref[2]) / (e0 + e1 + e2)
    o_ref[...] = out.astype(BF16)


def attn_a_prompt(qkv, batch, seq):
    n_tot, width = qkv.shape
    hw = A_HEADS * A_HEAD_DIM
    t_blk = A_BLOCK * max(d for _, d in A_GROUPS)
    assert seq % t_blk == 0
    nblk = seq // t_blk
    in_specs, args = [], []
    for g, (win, dil) in enumerate(A_GROUPS):
        span = A_BLOCK * dil
        per_blk = t_blk // span
        for which in range(3):
            col = (g * 3 + which) * A_HEADS
            in_specs.append(pl.BlockSpec((t_blk, A_HEAD_DIM),
                                         lambda b, k, h, col=col: (b * nblk + k, col + h)))
            args.append(qkv)
        for which in (1, 2):
            col = (g * 3 + which) * A_HEADS
            in_specs.append(pl.BlockSpec(
                (span, A_HEAD_DIM),
                lambda b, k, h, col=col, per_blk=per_blk: (jnp.maximum((b * nblk + k) * per_blk - 1, 0), col + h)))
            args.append(qkv)
    scratch = pltpu.VMEM((len(A_GROUPS), t_blk, A_HEAD_DIM), F32)
    return pl.pallas_call(
        _attn_a_kernel, grid=(batch, nblk, A_HEADS), in_specs=in_specs,
        out_specs=pl.BlockSpec((t_blk, A_HEAD_DIM), lambda b, k, h: (b * nblk + k, h)),
        out_shape=jax.ShapeDtypeStruct((batch * seq, hw), BF16),
        scratch_shapes=[scratch, scratch],
        compiler_params=_params(("parallel", "parallel", "parallel")),
        name="attn_a_prompt")(*args)


def _attn_a_sample_kernel(qkv_ref, c0_ref, c1_ref, c2_ref, o_ref):
    t_new = qkv_ref.shape[1]
    hw = A_HEADS * A_HEAD_DIM
    scale = 1.0 / math.sqrt(A_HEAD_DIM)
    nt = (((1,), (1,)), ((), ()))
    caches = (c0_ref, c1_ref, c2_ref)
    ti = lax.broadcasted_iota(jnp.int32, (t_new, A_BLOCK), 0)
    ni = lax.broadcasted_iota(jnp.int32, (t_new, A_BLOCK), 1)
    tq = lax.broadcasted_iota(jnp.int32, (t_new, t_new), 0)
    tj = lax.broadcasted_iota(jnp.int32, (t_new, t_new), 1)
    for h in range(A_HEADS):
        outs, lses = [], []
        for g, (win, dil) in enumerate(A_GROUPS):
            base = g * 3 * hw + h * A_HEAD_DIM
            q = qkv_ref[0, :, base:base + A_HEAD_DIM].astype(BF16)
            kn = qkv_ref[0, :, base + hw:base + hw + A_HEAD_DIM].astype(BF16)
            vn = qkv_ref[0, :, base + 2 * hw:base + 2 * hw + A_HEAD_DIM].astype(BF16)
            n_sub = min(dil, t_new)
            s_new = lax.dot_general(q, kn, nt, preferred_element_type=F32) * scale
            ok_new = jnp.logical_and(tj <= tq, jnp.bitwise_and(tq - tj, dil - 1) == 0)
            s_new = jnp.where(ok_new, s_new, -jnp.inf)
            ss, vs = [], []
            for r in range(n_sub):
                kb = caches[g][0, :, r * 2 * hw + h * A_HEAD_DIM:r * 2 * hw + (h + 1) * A_HEAD_DIM]
                vb = caches[g][0, :, r * 2 * hw + hw + h * A_HEAD_DIM:r * 2 * hw + hw + (h + 1) * A_HEAD_DIM]
                s = lax.dot_general(q, kb.astype(BF16), nt, preferred_element_type=F32) * scale
                ok = jnp.logical_and(jnp.bitwise_and(ti, dil - 1) == r, ni * dil + r >= ti)
                ss.append(jnp.where(ok, s, -jnp.inf))
                vs.append(vb.astype(BF16))
            m = jnp.max(s_new, axis=-1, keepdims=True)
            for s in ss:
                m = jnp.maximum(m, jnp.max(s, axis=-1, keepdims=True))
            e_new = jnp.exp(s_new - m)
            den = jnp.sum(e_new, axis=-1, keepdims=True)
            o = jnp.dot(e_new.astype(BF16), vn, preferred_element_type=F32)
            for s, vb in zip(ss, vs):
                e = jnp.exp(s - m)
                den = den + jnp.sum(e, axis=-1, keepdims=True)
                o = o + jnp.dot(e.astype(BF16), vb, preferred_element_type=F32)
            outs.append(o / den)
            lses.append(m + jnp.log(den))
        lm = jnp.maximum(jnp.maximum(lses[0], lses[1]), lses[2])
        ws = [jnp.exp(l - lm) for l in lses]
        tot = ws[0] + ws[1] + ws[2]
        comb = (ws[0] * outs[0] + ws[1] * outs[1] + ws[2] * outs[2]) / tot
        o_ref[0, :, h * A_HEAD_DIM:(h + 1) * A_HEAD_DIM] = comb


def attn_a_sample(qkv_s, caches, layer):
    db, t_new, width = qkv_s.shape
    hw = A_HEADS * A_HEAD_DIM
    in_specs = [pl.BlockSpec((1, t_new, width), lambda b: (b, 0, 0))]
    args = [qkv_s]
    for (win, dil), c in zip(A_GROUPS, caches):
        nl = c.shape[0]
        view = c.reshape(nl * db, win // dil, dil * 2 * hw)
        n_sub = min(dil, t_new)
        in_specs.append(pl.BlockSpec((1, win // dil, n_sub * 2 * hw),
                                     lambda b, layer=layer: (layer * db + b, 0, 0)))
        args.append(view)
    return pl.pallas_call(
        _attn_a_sample_kernel, grid=(db,), in_specs=in_specs,
        out_specs=pl.BlockSpec((1, t_new, hw), lambda b: (b, 0, 0)),
        out_shape=jax.ShapeDtypeStruct((db, t_new, hw), F32),
        compiler_params=_params(("parallel",)), name="attn_a_sample")(*args)


def _lambda(lp_ref, lam_init):
    lp = lp_ref[...]
    a = jnp.sum(lp[0:1, :] * lp[1:2, :], axis=-1, keepdims=True)
    b = jnp.sum(lp[2:3, :] * lp[3:4, :], axis=-1, keepdims=True)
    return jnp.exp(a) - jnp.exp(b) + lam_init


def _sub_ln(o, g, lam_init):
    on = o * lax.rsqrt(jnp.mean(o * o, axis=-1, keepdims=True) + SUBLN_EPS) * g
    return on * (1.0 - lam_init)


def _attn_b_kernel(q_ref, k_ref, v_ref, lp_ref, g_ref, o_ref, m_ref, l_ref, acc_ref, *, lam_init, tq, tk, sub):
    qi = pl.program_id(2)
    ki = pl.program_id(3)
    nk = pl.num_programs(3)
    nt = (((1,), (1,)), ((), ()))
    sub = min(tq, sub)

    @pl.when(ki == 0)
    def _():
        m_ref[...] = jnp.full(m_ref.shape, -jnp.inf, F32)
        l_ref[...] = jnp.zeros(l_ref.shape, F32)
        acc_ref[...] = jnp.zeros(acc_ref.shape, F32)

    def step(masked):
        v = v_ref[...].astype(BF16)
        if masked:
            rows = qi * tq + lax.broadcasted_iota(jnp.int32, (tq, tk), 0)
            cols = ki * tk + lax.broadcasted_iota(jnp.int32, (tq, tk), 1)
            mask = cols <= rows
        for c in range(2):
            sl = slice(c * B_QK_DIM, (c + 1) * B_QK_DIM)
            kc = k_ref[:, sl].astype(BF16)
            for r in range(tq // sub):
                rs = slice(r * sub, (r + 1) * sub)
                s = lax.dot_general(q_ref[rs, sl], kc, nt, preferred_element_type=F32)
                if masked:
                    s = jnp.where(mask[rs, :], s, -jnp.inf)
                m_old = m_ref[c, rs, :]
                m_new = jnp.maximum(m_old, jnp.max(s, axis=-1, keepdims=True))
                alpha = jnp.exp2(m_old - m_new)
                p = jnp.exp2(s - m_new)
                l_ref[c, rs, :] = alpha * l_ref[c, rs, :] + jnp.sum(p, axis=-1, keepdims=True)
                acc_ref[c, rs, :] = (alpha * acc_ref[c, rs, :]
                                     + jnp.dot(p.astype(BF16), v, preferred_element_type=F32))
                m_ref[c, rs, :] = m_new

    first_row = qi * tq
    last_col = ki * tk + tk - 1

    @pl.when(last_col <= first_row)
    def _():
        step(False)

    @pl.when(jnp.logical_and(last_col > first_row, ki * tk <= first_row + tq - 1))
    def _():
        step(True)

    @pl.when(ki == nk - 1)
    def _():
        lam = _lambda(lp_ref, lam_init)
        o = acc_ref[0] / l_ref[0] - lam * (acc_ref[1] / l_ref[1])
        o_ref[...] = _sub_ln(o, g_ref[...], lam_init).astype(BF16)


def attn_b_prompt(q, kv, lp, g_sub, lam_init, batch, seq, *, tq=1024, tk=1024, sub=128):
    tq = min(tq, seq)
    tk = min(tk, seq)
    nq, nk = seq // tq, seq // tk
    hd = 2 * B_QK_DIM

    def kmap(b, h, i, j):
        return (b * nk + jnp.minimum(j, (i * tq + tq - 1) // tk), h)

    def vmap_(b, h, i, j):
        return (b * nk + jnp.minimum(j, (i * tq + tq - 1) // tk), B_HEADS + h)

    kern = functools.partial(_attn_b_kernel, lam_init=lam_init, tq=tq, tk=tk, sub=sub)
    return pl.pallas_call(
        kern, grid=(batch, B_HEADS, nq, nk),
        in_specs=[pl.BlockSpec((tq, hd), lambda b, h, i, j: (b * nq + i, h)),
                  pl.BlockSpec((tk, hd), kmap),
                  pl.BlockSpec((tk, B_V_DIM), vmap_),
                  pl.BlockSpec((4, B_QK_DIM), lambda b, h, i, j: (0, 0)),
                  pl.BlockSpec((1, B_V_DIM), lambda b, h, i, j: (0, 0))],
        out_specs=pl.BlockSpec((tq, B_V_DIM), lambda b, h, i, j: (b * nq + i, h)),
        out_shape=jax.ShapeDtypeStruct((batch * seq, B_HEADS * B_V_DIM), BF16),
        scratch_shapes=[pltpu.VMEM((2, tq, 1), F32), pltpu.VMEM((2, tq, 1), F32),
                        pltpu.VMEM((2, tq, B_V_DIM), F32)],
        compiler_params=_params(("parallel", "parallel", "parallel", "arbitrary")),
        name="attn_b_prompt")(q, kv, kv, lp, g_sub.reshape(1, B_V_DIM))


def _attn_b_sample_kernel(*refs, lam_init, t_new, n_par):
    pt_ref, q_ref = refs[0], refs[1]
    page_refs = refs[2:2 + 2 * n_par]
    kvn_ref, lp_ref, g_ref, o_ref, m_ref, l_ref, acc_ref, xs_ref = refs[2 + 2 * n_par:]
    p = pl.program_id(1)
    n_steps = pl.num_programs(1)
    kw = B_HEADS * 2 * B_QK_DIM
    nt = (((1,), (1,)), ((), ()))
    rph = 2 * t_new

    @pl.when(p == 0)
    def _():
        m_ref[...] = jnp.full(m_ref.shape, -jnp.inf, F32)
        l_ref[...] = jnp.zeros(l_ref.shape, F32)
        acc_ref[...] = jnp.zeros(acc_ref.shape, F32)

    def update(state, s, weigh):
        m_old, l_old, acc_old = state
        m_new = jnp.maximum(m_old, jnp.max(s, axis=-1, keepdims=True))
        alpha = jnp.exp2(m_old - m_new)
        e = jnp.exp2(s - m_new)
        return m_new, alpha * l_old + jnp.sum(e, axis=-1, keepdims=True), alpha * acc_old + weigh(e)

    half_heads = B_HEADS // 2
    rows_per_page = 4 * PAGE_SIZE
    for g in range(half_heads):
        for half in range(2):
            for k in range(n_par):
                xs_ref[g, half, k * rows_per_page:(k + 1) * rows_per_page, :] = (
                    page_refs[2 * k + half][0, pl.ds(g, rows_per_page, stride=half_heads), :].astype(BF16))

    def pair_rows(x, g):
        return jnp.concatenate([x[g * rph:(g + 1) * rph], x[(g + half_heads) * rph:(g + half_heads + 1) * rph]],
                               axis=0)

    q_all = q_ref[0]
    lo, hi = [None] * half_heads, [None] * half_heads
    for g in range(half_heads):
        qg = pair_rows(q_all, g)
        sg = (lax.dot_general(qg[:, 0:B_QK_DIM], xs_ref[g, 0], nt, preferred_element_type=F32)
              + lax.dot_general(qg[:, B_QK_DIM:], xs_ref[g, 1], nt, preferred_element_type=F32))
        lo[g], hi[g] = sg[0:rph], sg[rph:2 * rph]
    s = jnp.concatenate(lo + hi, axis=0)
    kind = jnp.bitwise_and(lax.broadcasted_iota(jnp.int32, s.shape, 1), 3)
    own = (lax.broadcasted_iota(jnp.int32, s.shape, 0) >= half_heads * rph).astype(jnp.int32)
    s = jnp.where(kind == own, s, -jnp.inf)

    def weigh(e):
        ev = pltpu.roll(e, 2, 1)
        lo_o, hi_o = [None] * half_heads, [None] * half_heads
        for g in range(half_heads):
            eg = pair_rows(ev, g).astype(BF16)
            og = jnp.concatenate([jnp.dot(eg, xs_ref[g, 0], preferred_element_type=F32),
                                  jnp.dot(eg, xs_ref[g, 1], preferred_element_type=F32)], axis=1)
            lo_o[g], hi_o[g] = og[0:rph], og[rph:2 * rph]
        return jnp.concatenate(lo_o + hi_o, axis=0)

    m_new, l_new, acc_new = update((m_ref[...], l_ref[...], acc_ref[...]), s, weigh)
    m_ref[...] = m_new
    l_ref[...] = l_new
    acc_ref[...] = acc_new

    @pl.when(p == n_steps - 1)
    def _():
        lam = _lambda(lp_ref, lam_init)
        for h in range(B_HEADS):
            rs = slice(h * rph, (h + 1) * rph)
            qh = q_ref[0, rs, :]
            kn = kvn_ref[0, :, h * B_V_DIM:(h + 1) * B_V_DIM].astype(BF16)
            vn = kvn_ref[0, :, kw + h * B_V_DIM:kw + (h + 1) * B_V_DIM].astype(BF16)
            sn = lax.dot_general(qh, kn, nt, preferred_element_type=F32)
            rq = jnp.bitwise_and(lax.broadcasted_iota(jnp.int32, sn.shape, 0), t_new - 1)
            cj = lax.broadcasted_iota(jnp.int32, sn.shape, 1)
            _, l_fin, acc_fin = update(
                (m_ref[rs, :], l_ref[rs, :], acc_ref[rs, :]), jnp.where(cj <= rq, sn, -jnp.inf),
                lambda e, vn=vn: jnp.dot(e.astype(BF16), vn, preferred_element_type=F32))
            on = acc_fin / l_fin
            o = on[0:t_new, :] - lam * on[t_new:2 * t_new, :]
            o_ref[0, :, h * B_V_DIM:(h + 1) * B_V_DIM] = _sub_ln(o, g_ref[...], lam_init)


def attn_b_sample(q_s, kv_s, cache_b_kv, page_table, lp, g_sub, lam_init):
    db, t_new, qw = q_s.shape
    n_pages = page_table.shape[1]
    n_phys = cache_b_kv.shape[0]
    assert t_new & (t_new - 1) == 0
    n_par = next(k for k in (8, 4, 2, 1) if n_pages % k == 0)
    kvw = 2 * B_HEADS * B_V_DIM
    page_rows = PAGE_SIZE * 2 * B_HEADS
    pages = cache_b_kv.reshape(n_phys, page_rows, B_V_DIM)
    q5 = q_s.reshape(db, t_new, B_HEADS, 2, B_QK_DIM).transpose(0, 2, 3, 1, 4)
    eye = jnp.eye(2, dtype=q_s.dtype)
    qbd = q5[:, :, :, :, None, :] * eye[None, None, :, None, :, None]
    rows = B_HEADS * 2 * t_new
    qbd = qbd.reshape(db, rows, 2 * B_QK_DIM).astype(BF16)
    kern = functools.partial(_attn_b_sample_kernel, lam_init=lam_init, t_new=t_new, n_par=n_par)
    page_specs = [pl.BlockSpec((1, page_rows, B_QK_DIM),
                               lambda b, p, pt, k=k, half=half: (pt[b, p * n_par + k], 0, half))
                  for k in range(n_par) for half in range(2)]
    grid_spec = pltpu.PrefetchScalarGridSpec(
        num_scalar_prefetch=1, grid=(db, n_pages // n_par),
        in_specs=[pl.BlockSpec((1, rows, 2 * B_QK_DIM), lambda b, p, pt: (b, 0, 0))] + page_specs + [
                  pl.BlockSpec((1, t_new, kvw), lambda b, p, pt: (b, 0, 0)),
                  pl.BlockSpec((4, B_QK_DIM), lambda b, p, pt: (0, 0)),
                  pl.BlockSpec((1, B_V_DIM), lambda b, p, pt: (0, 0))],
        out_specs=pl.BlockSpec((1, t_new, qw), lambda b, p, pt: (b, 0, 0)),
        scratch_shapes=[pltpu.VMEM((rows, 1), F32), pltpu.VMEM((rows, 1), F32),
                        pltpu.VMEM((rows, B_V_DIM), F32),
                        pltpu.VMEM((B_HEADS // 2, 2, n_par * 4 * PAGE_SIZE, B_QK_DIM), BF16)])
    return pl.pallas_call(
        kern, grid_spec=grid_spec,
        out_shape=jax.ShapeDtypeStruct((db, t_new, qw), F32),
        compiler_params=_params(("parallel", "arbitrary")),
        name="attn_b_sample")(page_table, qbd, *([pages] * (2 * n_par)), kv_s, lp, g_sub.reshape(1, B_V_DIM))


def _top_ranks(s, exact):
    iota = lax.broadcasted_iota(jnp.int32, s.shape, 0).astype(F32)
    rank = jnp.full(s.shape, NOT_SELECTED, F32)
    vals = []
    work = s
    for a in range(PEER_TOPK):
        m = jnp.max(work, axis=0, keepdims=True)
        if exact:
            idx = jnp.min(jnp.where(work == m, iota, float(N_KEYS)), axis=0, keepdims=True)
            hit = iota == idx
        else:
            hit = work == m
        rank = jnp.where(hit, float(a), rank)
        work = jnp.where(hit, -jnp.inf, work)
        vals.append(m)
    taken = jnp.sum(jnp.where(rank < NOT_SELECTED, 1.0, 0.0), axis=0, keepdims=True)
    return vals, rank, taken


def _candidate_rows():
    pieces = [(0, 0, 16, 16)]
    for a in range(1, 8):
        pieces.append((a, 0, 8, PEER_TOPK // (a + 1)))
    return pieces


def _route_kernel(ht_ref, wq_ref, sk_ref, r1_ref, e1_ref, bq_out_ref, c0_out_ref, qt_ref, bq_ref, c0_ref):
    n_chunks = ht_ref.shape[1] // LANES
    qt_ref[...] = jnp.dot(wq_ref[...], ht_ref[...], preferred_element_type=F32)
    neg = -jnp.inf
    t = LANES

    def route_chunk(h, cs, scores, exact):
        vals, ranks = [], []
        off = jnp.zeros((1, t), F32)
        for c in range(2):
            v, r, taken = _top_ranks(scores[c], exact)
            vals.append(v); ranks.append(r)
            off = off + jnp.abs(taken - float(PEER_TOPK))
        v0, v1 = vals
        v1_16 = jnp.concatenate(v1, axis=0)
        v0_hi = jnp.concatenate(v0[8:16], axis=0)
        cands, flats = [], []
        for a, _, rows, nvalid in _candidate_rows():
            b_iota = lax.broadcasted_iota(jnp.int32, (rows, t), 0)
            cs_ab = v0[a] + v1_16[0:rows, :]
            cands.append(jnp.where(b_iota < nvalid, cs_ab, neg))
            flats.append((b_iota + a * PEER_TOPK).astype(F32))
        cands.append(v0_hi + v1[0])
        flats.append(((lax.broadcasted_iota(jnp.int32, (8, t), 0) + 8) * PEER_TOPK).astype(F32))
        cand = jnp.concatenate(cands, axis=0)
        flat = jnp.concatenate(flats, axis=0)
        big = float(PEER_TOPK * PEER_TOPK)
        work = cand
        sel = jnp.zeros(cand.shape, F32)
        for _ in range(PEER_TOPK):
            m = jnp.max(work, axis=0, keepdims=True)
            if exact:
                idx = jnp.min(jnp.where(work == m, flat, big), axis=0, keepdims=True)
                hit = flat == idx
            else:
                hit = work == m
            sel = jnp.where(hit, 1.0, sel)
            work = jnp.where(hit, neg, work)
        off = off + jnp.abs(jnp.sum(sel, axis=0, keepdims=True) - float(PEER_TOPK))
        top = v0[0] + v1[0]
        z = jnp.sum(jnp.where(sel > 0.0, jnp.exp(cand - top), 0.0), axis=0, keepdims=True)
        counts = [jnp.sum(sel[0:16, :], axis=0, keepdims=True)]
        for k in range(1, 8):
            counts.append(jnp.sum(sel[8 + 8 * k:16 + 8 * k, :], axis=0, keepdims=True))
        hi = sel[72:80, :]
        bq = jnp.zeros(ranks[0].shape, F32)
        for a in range(PEER_TOPK):
            cnt = counts[a] if a < 8 else hi[a - 8:a - 7, :]
            bq = jnp.where(ranks[0] == float(a), cnt, bq)
        rs = pl.ds(pl.multiple_of(h * N_KEYS, N_KEYS), N_KEYS)
        bq_ref[rs, cs] = bq
        c0_ref[rs, cs] = jnp.exp(scores[0] - v0[0]) * (0.5 / z)
        r1_ref[rs, cs] = ranks[1].astype(BF16)
        e1_ref[rs, cs] = jnp.exp(scores[1] - v1[0]).astype(BF16)
        return off

    def body(h, carry):
        work = []
        for k in range(n_chunks):
            cs = slice(k * LANES, (k + 1) * LANES)
            scores = []
            for c in range(2):
                hc = h * 2 + c
                qhc = qt_ref[pl.ds(pl.multiple_of(hc * N_KEYS, N_KEYS), N_KEYS), cs].astype(BF16)
                scores.append(jnp.dot(sk_ref[hc].astype(BF16), qhc, preferred_element_type=F32))
            work.append((cs, scores, route_chunk(h, cs, scores, exact=False)))
        for cs, scores, off in work:
            @pl.when(jnp.max(off) > 0.0)
            def _(cs=cs, scores=scores):
                route_chunk(h, cs, scores, exact=True)

        return carry

    lax.fori_loop(0, PEER_HEADS, body, 0)
    for h in range(PEER_HEADS):
        bq_out_ref[:, h, :] = bq_ref[h * N_KEYS:(h + 1) * N_KEYS, :]
        c0_out_ref[:, h, :] = c0_ref[h * N_KEYS:(h + 1) * N_KEYS, :]


def peer_route(ht, wq_t, subkeys, *, tm=512):
    d, n = ht.shape
    rows = PEER_HEADS * N_KEYS
    slab = pl.BlockSpec((rows, tm), lambda i: (0, i))
    sds = jax.ShapeDtypeStruct((rows, n), BF16)
    slab3 = pl.BlockSpec((N_KEYS, PEER_HEADS, tm), lambda i: (0, 0, i))
    sds3 = jax.ShapeDtypeStruct((N_KEYS, PEER_HEADS, n), F32)
    return pl.pallas_call(
        _route_kernel, grid=(n // tm,),
        in_specs=[pl.BlockSpec((d, tm), lambda i: (0, i)),
                  pl.BlockSpec(wq_t.shape, lambda i: (0, 0)),
                  pl.BlockSpec(subkeys.shape, lambda i: (0, 0, 0))],
        out_specs=[slab, slab, slab3, slab3], out_shape=[sds, sds, sds3, sds3],
        scratch_shapes=[pltpu.VMEM((wq_t.shape[0], tm), F32), pltpu.VMEM((rows, tm), F32),
                        pltpu.VMEM((rows, tm), F32)],
        compiler_params=_params(("parallel",)), name="peer_route")(ht, wq_t, subkeys)


def _gelu_x2(a):
    return a * (1.0 + lax.erf(a * math.sqrt(0.5)))


def _expert_kernel(x_ref, ht_ref, u_ref, v_ref, r1_ref, e1_ref, bq_ref, c0_ref, o_ref, *a_refs, te, tm, ge):
    e = pl.program_id(1)
    rows_per_group = ge // N_KEYS
    n_chunks = tm // LANES
    packed = 2 * SUBLANES
    tiles = N_KEYS // packed
    nn = (((1,), (0,)), ((), ()))

    @pl.when(e == 0)
    def _():
        o_ref[...] = x_ref[...]

    n_groups = te // ge
    for k in range(n_groups):
        a_refs[k][...] = lax.dot_general(u_ref[k * ge:(k + 1) * ge, :], ht_ref[...], nn,
                                         preferred_element_type=F32)
    groups = []
    for k in range(n_groups):
        rows = []
        for r in range(rows_per_group):
            i = e * (te // N_KEYS) + k * rows_per_group + r
            cols = []
            for c in range(n_chunks):
                cs = slice(c * LANES, (c + 1) * LANES)
                bq_all = bq_ref[i, :, cs]
                c0_all = c0_ref[i, :, cs]
                w = [jnp.zeros((packed, LANES), BF16)] * tiles
                for h in range(PEER_HEADS):
                    bq = jnp.broadcast_to(bq_all[h:h + 1, :], (packed, LANES)).astype(BF16)
                    c0 = jnp.broadcast_to(c0_all[h:h + 1, :], (packed, LANES)).astype(BF16)
                    for t in range(tiles):
                        js = slice(h * N_KEYS + t * packed, h * N_KEYS + (t + 1) * packed)
                        e1 = e1_ref[js, cs]
                        w[t] = w[t] + jnp.where(r1_ref[js, cs] < bq, e1 * c0, jnp.zeros_like(e1))
                gate = jnp.concatenate(w, axis=0).astype(F32)
                cols.append(_gelu_x2(a_refs[k][r * N_KEYS:(r + 1) * N_KEYS, cs]) * gate)
            rows.append(jnp.concatenate(cols, axis=1))
        groups.append(jnp.concatenate(rows, axis=0).T)
    g = jnp.concatenate(groups, axis=1)
    o_ref[...] += lax.dot_general(g, v_ref[...], nn, preferred_element_type=F32)


def peer_experts(x, ht, u, v_bf, layer, slabs, *, tm=512, te=1024, ge=256):
    d, n = ht.shape
    n_exp = u.shape[1]
    while n % tm:
        tm //= 2
    te = min(te, n_exp)
    rows = PEER_HEADS * N_KEYS
    once = pl.Buffered(1)
    slab = pl.BlockSpec((rows, tm), lambda i, e: (0, i), pipeline_mode=once)
    slab3 = pl.BlockSpec((N_KEYS, PEER_HEADS, tm), lambda i, e: (0, 0, i), pipeline_mode=once)
    kern = functools.partial(_expert_kernel, te=te, tm=tm, ge=ge)
    return pl.pallas_call(
        kern, grid=(n // tm, n_exp // te),
        in_specs=[pl.BlockSpec((tm, d), lambda i, e: (i, 0), pipeline_mode=once),
                  pl.BlockSpec((d, tm), lambda i, e: (0, i), pipeline_mode=once),
                  pl.BlockSpec((None, te, d), lambda i, e: (layer, e, 0)),
                  pl.BlockSpec((None, te, d), lambda i, e: (layer, e, 0)),
                  slab, slab, slab3, slab3],
        out_specs=pl.BlockSpec((tm, d), lambda i, e: (i, 0)),
        out_shape=jax.ShapeDtypeStruct((n, d), F32),
        scratch_shapes=[pltpu.VMEM((ge, tm), F32) for _ in range(te // ge)],
        compiler_params=_params(("parallel", "arbitrary")),
        name="peer_experts")(x, ht, u, v_bf, *slabs)


def _rope_table(pos):
    rot = A_HEAD_DIM // 4
    half = rot // 2
    inv_freq = ROPE_THETA ** (-jnp.arange(half, dtype=F32) / half)
    ang = pos.astype(F32)[:, None] * inv_freq[None, :]
    cos, sin = jnp.cos(ang), jnp.sin(ang)
    n = pos.shape[0]
    ones = jnp.ones((n, A_HEAD_DIM - rot), F32)
    zeros = jnp.zeros((n, A_HEAD_DIM - rot), F32)
    zh = jnp.zeros((n, half), F32)
    return jnp.concatenate([cos, cos, ones, -sin, zh, zeros, zh, sin, zeros], axis=1)


def kernel(x_prompt, x_sample, cache_a_w128, cache_a_w512, cache_a_w2048, cache_b_kv, page_table,
           p_prompt, p_sample, norm_mix, norm_ffn, norm_ple, norm_kv, norm_final,
           w_qkv_a, w_o_a, w_kv_b, w_q_b, diff_lambda, norm_sub_b, w_o_b,
           peer_wq, peer_subkeys, peer_u, peer_v, w_ple, w_ple_gate):
    batch, seq, d = x_prompt.shape
    db, t_new, _ = x_sample.shape
    depth = norm_mix.shape[0]
    n_a = w_qkv_a.shape[0]
    past_len = page_table.shape[1] * PAGE_SIZE
    n_p, n_s = batch * seq, db * t_new
    n_tot = -(-(n_p + n_s) // TOKEN_PAD) * TOKEN_PAD
    pad = n_tot - n_p - n_s
    a_caches = (cache_a_w128, cache_a_w512, cache_a_w2048)
    hw = A_HEADS * A_HEAD_DIM

    def tokens(prompt_part, sample_part):
        w = prompt_part.shape[-1]
        return jnp.concatenate([prompt_part.reshape(n_p, w), sample_part.reshape(n_s, w),
                                jnp.zeros((pad, w), prompt_part.dtype)], axis=0)

    x = tokens(x_prompt, x_sample)
    pos = jnp.concatenate([jnp.tile(jnp.arange(seq, dtype=jnp.int32), batch),
                           jnp.tile(past_len + jnp.arange(t_new, dtype=jnp.int32), db),
                           jnp.zeros((pad,), jnp.int32)])
    tab = _rope_table(pos)
    v_bf = peer_v.astype(BF16)

    a_rows_p = [[] for _ in A_GROUPS]
    a_rows_s = [[] for _ in A_GROUPS]
    new_b_kv_prompt = new_b_kv_sample = kv = None
    for i in range(depth):
        h = rms_norm(x, norm_mix[i], out_h=True)[0]
        if i < n_a:
            qkv = matmul(h, w_qkv_a, layer=i, mode="rope", tab=tab, tn=hw,
                         rope_fn=lambda col: (col // hw) % 3 != 2)
            o_p = attn_a_prompt(qkv, batch, seq)
            qkv_s = qkv[n_p:n_p + n_s].reshape(db, t_new, qkv.shape[1])
            o_s = attn_a_sample(qkv_s, a_caches, i).reshape(n_s, hw).astype(BF16)
            o_all = jnp.concatenate([o_p, o_s, jnp.zeros((pad, hw), BF16)], axis=0)
            x = matmul(o_all, w_o_a, layer=i, mode="res", res=x, tn=1024)
            for g, (win, dil) in enumerate(A_GROUPS):
                wb = min(win, seq)
                c0, c1 = (g * 3 + 1) * hw, (g * 3 + 3) * hw
                rows = jnp.stack([qkv[(b + 1) * seq - wb:(b + 1) * seq, c0:c1] for b in range(batch)], axis=0)
                a_rows_p[g].append(rows.reshape(batch, wb, 2, A_HEADS, A_HEAD_DIM))
                a_rows_s[g].append(qkv[n_p:n_p + n_s, c0:c1].reshape(db, t_new, 2, A_HEADS, A_HEAD_DIM))
        else:
            j = i - n_a
            if j == 0:
                hkv = rms_norm(x, norm_kv, out_h=True)[0]
                kw = B_HEADS * 2 * B_QK_DIM
                kv = matmul(hkv, w_kv_b, mode="rope", tab=tab, tn=1024, rope_fn=lambda col: col < kw)
                new_b_kv_prompt = kv[:n_p].reshape(batch, seq, 2, B_HEADS, B_V_DIM)
                new_b_kv_sample = kv[n_p:n_p + n_s].reshape(db, t_new, 2, B_HEADS, B_V_DIM)
            lam_init = 0.8 - 0.6 * math.exp(-0.3 * i)
            q = matmul(h, w_q_b, layer=j, mode="rope", tab=tab, tn=1024, rope_fn=lambda col: col >= 0,
                       out_scale=math.log2(math.e) / math.sqrt(B_QK_DIM), out_dtype=BF16)
            o_p = attn_b_prompt(q, kv, diff_lambda[j], norm_sub_b[j], lam_init, batch, seq)
            q_s = q[n_p:n_p + n_s].reshape(db, t_new, q.shape[1])
            kv_s = kv[n_p:n_p + n_s].reshape(db, t_new, kv.shape[1])
            o_s = attn_b_sample(q_s, kv_s, cache_b_kv, page_table, diff_lambda[j], norm_sub_b[j], lam_init)
            o_all = jnp.concatenate([o_p, o_s.reshape(n_s, -1).astype(BF16),
                                     jnp.zeros((pad, o_p.shape[1]), BF16)], axis=0)
            x = matmul(o_all, w_o_b, layer=j, mode="res", res=x, tn=1024)
        ht = rms_norm(x, norm_ffn[i], out_ht=True)[0]
        wq_t = peer_wq[i].T.astype(BF16)
        sk = peer_subkeys[i].reshape(PEER_HEADS * 2, N_KEYS, -1)
        slabs = peer_route(ht, wq_t, sk)
        x = peer_experts(x, ht, peer_u, v_bf, i, slabs)
        hn = rms_norm(x, norm_ple[i], out_h=True)[0]
        x = ple(x, tokens(p_prompt[i], p_sample[i]), hn, w_ple, w_ple_gate, i)

    y = rms_norm(x, norm_final, out_y=True)[0]
    y_prompt = y[:n_p].reshape(batch, seq, d)
    y_sample = y[n_p:n_p + n_s].reshape(db, t_new, d)
    outs_p = [jnp.stack(r, axis=0) for r in a_rows_p]
    outs_s = [jnp.stack(r, axis=0) for r in a_rows_s]
    return (y_prompt, y_sample, *outs_p, *outs_s, new_b_kv_prompt, new_b_kv_sample)
```

```python
import functools
import math

import jax
import jax.numpy as jnp
from jax import lax
from jax.experimental import pallas as pl
from jax.experimental.pallas import tpu as pltpu

BF16 = jnp.bfloat16
F32 = jnp.float32

ROPE_THETA = 500000.0
NORM_EPS = 1e-6
SUBLN_EPS = 1e-5
A_GROUPS = ((128, 1), (512, 4), (2048, 16))
A_HEADS = 8
A_HEAD_DIM = 128
A_BLOCK = 128
B_HEADS = 8
B_QK_DIM = 128
B_V_DIM = 256
PEER_HEADS = 8
N_KEYS = 128
PEER_TOPK = 16
PAGE_SIZE = 128

LANES = 128
SUBLANES = 8
VMEM_LIMIT = 56 * 1024 * 1024
TOKEN_PAD = 512
MATMUL_ROW_CAP = 1100
NOT_SELECTED = 99.0


def _params(sem):
    return pltpu.CompilerParams(dimension_semantics=sem, vmem_limit_bytes=VMEM_LIMIT)


def _token_tile(n, cap):
    packed = 2 * SUBLANES
    for parts in range(1, n // packed + 1):
        if n % parts == 0 and (n // parts) % packed == 0 and n // parts <= cap:
            return n // parts
    raise ValueError(f"no token tile for {n} rows under {cap}")


def _rms_kernel(x_ref, g_ref, *out_refs, eps, out_h, out_ht, out_y):
    it = iter(out_refs)
    x = x_ref[...]
    y = x * lax.rsqrt(jnp.mean(x * x, axis=-1, keepdims=True) + eps) * g_ref[...]
    if out_h:
        next(it)[...] = y.astype(BF16)
    if out_ht:
        next(it)[...] = y.T.astype(BF16)
    if out_y:
        next(it)[...] = y


def rms_norm(x, g, *, out_h=False, out_ht=False, out_y=False, eps=NORM_EPS, tm=512):
    n, d = x.shape
    grid = (n // tm,)
    row = pl.BlockSpec((tm, d), lambda i: (i, 0))
    col = pl.BlockSpec((d, tm), lambda i: (0, i))
    in_specs = [row, pl.BlockSpec((1, d), lambda i: (0, 0))]
    args = [x, g.reshape(1, d)]
    out_shape, out_specs = [], []
    if out_h:
        out_shape.append(jax.ShapeDtypeStruct((n, d), BF16)); out_specs.append(row)
    if out_ht:
        out_shape.append(jax.ShapeDtypeStruct((d, n), BF16)); out_specs.append(col)
    if out_y:
        out_shape.append(jax.ShapeDtypeStruct((n, d), F32)); out_specs.append(row)
    kern = functools.partial(_rms_kernel, eps=eps, out_h=out_h, out_ht=out_ht, out_y=out_y)
    return pl.pallas_call(kern, grid=grid, in_specs=in_specs, out_specs=out_specs,
                          out_shape=out_shape, compiler_params=_params(("parallel",)),
                          name="rms_norm")(*args)


def _rope_tile(y, tab):
    c = tab[:, 0:LANES]
    s1 = tab[:, LANES:2 * LANES]
    s2 = tab[:, 2 * LANES:3 * LANES]
    outs = []
    for g in range(y.shape[1] // LANES):
        yg = y[:, g * LANES:(g + 1) * LANES]
        outs.append(yg * c + pltpu.roll(yg, LANES - 16, 1) * s1 + pltpu.roll(yg, 16, 1) * s2)
    return jnp.concatenate(outs, axis=1) if len(outs) > 1 else outs[0]


def _mm_kernel(*refs, mode, rope_fn, tn, out_scale):
    if mode == "rope":
        x_ref, w_ref, tab_ref, o_ref, wb_ref = refs
    elif mode == "res":
        x_ref, w_ref, r_ref, o_ref, wb_ref = refs
    else:
        x_ref, w_ref, o_ref, wb_ref = refs
    j = pl.program_id(0)

    @pl.when(pl.program_id(1) == 0)
    def _():
        wb_ref[...] = w_ref[...].astype(BF16)

    y = jnp.dot(x_ref[...], wb_ref[...], preferred_element_type=F32)
    if mode == "rope":
        roped = rope_fn(j * tn)

        def finish(v):
            return (v if out_scale == 1.0 else v * out_scale).astype(o_ref.dtype)

        @pl.when(roped)
        def _():
            o_ref[...] = finish(_rope_tile(y, tab_ref[...]))

        @pl.when(jnp.logical_not(roped))
        def _():
            o_ref[...] = finish(y)
    elif mode == "res":
        o_ref[...] = r_ref[...] + y
    else:
        o_ref[...] = y


def matmul(x, w, *, mode="plain", tab=None, res=None, rope_fn=None, tm=MATMUL_ROW_CAP, tn=512,
           out_scale=1.0, out_dtype=F32, layer=None):
    n, k = x.shape
    m = w.shape[-1]
    tm = _token_tile(n, tm)
    tn = min(tn, m)
    grid = (m // tn, n // tm)
    if layer is None:
        w_spec = pl.BlockSpec((k, tn), lambda j, i: (0, j))
    else:
        w_spec = pl.BlockSpec((None, k, tn), lambda j, i: (layer, 0, j))
    in_specs = [pl.BlockSpec((tm, k), lambda j, i: (i, 0)), w_spec]
    args = [x, w]
    if mode == "rope":
        in_specs.append(pl.BlockSpec((tm, 3 * LANES), lambda j, i: (i, 0)))
        args.append(tab)
    elif mode == "res":
        in_specs.append(pl.BlockSpec((tm, tn), lambda j, i: (i, j)))
        args.append(res)
    assert mode == "rope" or (out_scale == 1.0 and out_dtype == F32)
    kern = functools.partial(_mm_kernel, mode=mode, rope_fn=rope_fn, tn=tn, out_scale=out_scale)
    return pl.pallas_call(
        kern, grid=grid, in_specs=in_specs,
        out_specs=pl.BlockSpec((tm, tn), lambda j, i: (i, j)),
        out_shape=jax.ShapeDtypeStruct((n, m), out_dtype),
        scratch_shapes=[pltpu.VMEM((k, tn), BF16)],
        compiler_params=_params(("arbitrary", "arbitrary")),
        name="matmul_" + mode)(*args)


def _ple_kernel(x_ref, p_ref, hn_ref, wp_ref, wg_ref, o_ref, wgb_ref):
    @pl.when(pl.program_id(1) == 0)
    def _():
        wgb_ref[...] = wg_ref[...].astype(BF16)

    gate = jax.nn.sigmoid(jnp.dot(hn_ref[...], wgb_ref[...], preferred_element_type=F32))
    up = jnp.dot(p_ref[...].astype(BF16), wp_ref[...].astype(BF16), preferred_element_type=F32)
    o_ref[...] = x_ref[...] + up * gate


def ple(x, p, hn, w_p, w_gate, layer, *, tm=MATMUL_ROW_CAP, tn=512):
    n, d = x.shape
    kp = p.shape[1]
    tn = min(tn, d)
    tm = _token_tile(n, tm)
    grid = (d // tn, n // tm)
    return pl.pallas_call(
        _ple_kernel, grid=grid,
        in_specs=[pl.BlockSpec((tm, tn), lambda j, i: (i, j)),
                  pl.BlockSpec((tm, kp), lambda j, i: (i, 0)),
                  pl.BlockSpec((tm, d), lambda j, i: (i, 0)),
                  pl.BlockSpec((None, kp, tn), lambda j, i: (layer, 0, j)),
                  pl.BlockSpec((None, d, tn), lambda j, i: (layer, 0, j))],
        out_specs=pl.BlockSpec((tm, tn), lambda j, i: (i, j)),
        out_shape=jax.ShapeDtypeStruct((n, d), F32),
        scratch_shapes=[pltpu.VMEM((d, tn), BF16)],
        compiler_params=_params(("arbitrary", "arbitrary")),
        name="ple")(x, p, hn, w_p, w_gate)


def _attn_a_kernel(*refs):
    n_in = 5 * len(A_GROUPS)
    o_ref, og_ref, lg_ref = refs[n_in:n_in + 3]
    blk = pl.program_id(1)
    t_blk = o_ref.shape[0]
    qi = lax.broadcasted_iota(jnp.int32, (A_BLOCK, A_BLOCK), 0)
    kj = lax.broadcasted_iota(jnp.int32, (A_BLOCK, A_BLOCK), 1)
    far = kj >= qi
    near = kj <= qi
    scale = math.log2(math.e) / math.sqrt(A_HEAD_DIM)
    nt = (((1,), (1,)), ((), ()))
    for g, (win, dil) in enumerate(A_GROUPS):
        q_ref, kc_ref, vc_ref, kp_ref, vp_ref = refs[5 * g:5 * g + 5]
        span = A_BLOCK * dil

        def body(it, carry, g=g, dil=dil, span=span, q_ref=q_ref, kc_ref=kc_ref, vc_ref=vc_ref,
                 kp_ref=kp_ref, vp_ref=vp_ref):
            n = it // dil
            r = it % dil
            start = n * span + r
            rows = pl.ds(start, A_BLOCK, stride=dil)
            before = pl.ds(jnp.maximum(start - span, 0), A_BLOCK, stride=dil)
            outside = pl.ds(r, A_BLOCK, stride=dil)
            first = n == 0
            q = q_ref[rows, :].astype(BF16)
            kc = kc_ref[rows, :].astype(BF16)
            vc = vc_ref[rows, :].astype(BF16)
            if span == t_blk:
                kp = kp_ref[outside, :].astype(BF16)
                vp = vp_ref[outside, :].astype(BF16)
            else:
                kp = jnp.where(first, kp_ref[outside, :], kc_ref[before, :]).astype(BF16)
                vp = jnp.where(first, vp_ref[outside, :], vc_ref[before, :]).astype(BF16)
            has_prev = jnp.logical_or(n > 0, blk > 0)
            sp = lax.dot_general(q, kp, nt, preferred_element_type=F32) * scale
            sc = lax.dot_general(q, kc, nt, preferred_element_type=F32) * scale
            sp = jnp.where(jnp.logical_and(far, has_prev), sp, -jnp.inf)
            sc = jnp.where(near, sc, -jnp.inf)
            m = jnp.maximum(jnp.max(sp, axis=-1, keepdims=True), jnp.max(sc, axis=-1, keepdims=True))
            ep = jnp.exp2(sp - m)
            ec = jnp.exp2(sc - m)
            den = jnp.sum(ep, axis=-1, keepdims=True) + jnp.sum(ec, axis=-1, keepdims=True)
            o = (jnp.dot(ep.astype(BF16), vp, preferred_element_type=F32)
                 + jnp.dot(ec.astype(BF16), vc, preferred_element_type=F32))
            og_ref[g, rows, :] = o / den
            lg_ref[g, rows, :] = jnp.broadcast_to(m + jnp.log2(den), (A_BLOCK, A_HEAD_DIM))
            return carry

        lax.fori_loop(0, t_blk // A_BLOCK, body, 0, unroll=8)
    l0, l1, l2 = lg_ref[0], lg_ref[1], lg_ref[2]
    m = jnp.maximum(jnp.maximum(l0, l1), l2)
    e0, e1, e2 = jnp.exp2(l0 - m), jnp.exp2(l1 - m), jnp.exp2(l2 - m)
    out = (e0 * og_ref[0] + e1 * og_ref[1] + e2 * og_ref[2]) / (e0 + e1 + e2)
    o_ref[...] = out.astype(BF16)


def attn_a_prompt(qkv, batch, seq):
    n_tot, width = qkv.shape
    hw = A_HEADS * A_HEAD_DIM
    t_blk = A_BLOCK * max(d for _, d in A_GROUPS)
    assert seq % t_blk == 0
    nblk = seq // t_blk
    in_specs, args = [], []
    for g, (win, dil) in enumerate(A_GROUPS):
        span = A_BLOCK * dil
        per_blk = t_blk // span
        for which in range(3):
            col = (g * 3 + which) * A_HEADS
            in_specs.append(pl.BlockSpec((t_blk, A_HEAD_DIM),
                                         lambda b, k, h, col=col: (b * nblk + k, col + h)))
            args.append(qkv)
        for which in (1, 2):
            col = (g * 3 + which) * A_HEADS
            in_specs.append(pl.BlockSpec(
                (span, A_HEAD_DIM),
                lambda b, k, h, col=col, per_blk=per_blk: (jnp.maximum((b * nblk + k) * per_blk - 1, 0), col + h)))
            args.append(qkv)
    scratch = pltpu.VMEM((len(A_GROUPS), t_blk, A_HEAD_DIM), F32)
    return pl.pallas_call(
        _attn_a_kernel, grid=(batch, nblk, A_HEADS), in_specs=in_specs,
        out_specs=pl.BlockSpec((t_blk, A_HEAD_DIM), lambda b, k, h: (b * nblk + k, h)),
        out_shape=jax.ShapeDtypeStruct((batch * seq, hw), BF16),
        scratch_shapes=[scratch, scratch],
        compiler_params=_params(("parallel", "parallel", "parallel")),
        name="attn_a_prompt")(*args)


def _attn_a_sample_kernel(qkv_ref, c0_ref, c1_ref, c2_ref, o_ref):
    t_new = qkv_ref.shape[1]
    hw = A_HEADS * A_HEAD_DIM
    scale = 1.0 / math.sqrt(A_HEAD_DIM)
    nt = (((1,), (1,)), ((), ()))
    caches = (c0_ref, c1_ref, c2_ref)
    ti = lax.broadcasted_iota(jnp.int32, (t_new, A_BLOCK), 0)
    ni = lax.broadcasted_iota(jnp.int32, (t_new, A_BLOCK), 1)
    tq = lax.broadcasted_iota(jnp.int32, (t_new, t_new), 0)
    tj = lax.broadcasted_iota(jnp.int32, (t_new, t_new), 1)
    for h in range(A_HEADS):
        outs, lses = [], []
        for g, (win, dil) in enumerate(A_GROUPS):
            base = g * 3 * hw + h * A_HEAD_DIM
            q = qkv_ref[0, :, base:base + A_HEAD_DIM].astype(BF16)
            kn = qkv_ref[0, :, base + hw:base + hw + A_HEAD_DIM].astype(BF16)
            vn = qkv_ref[0, :, base + 2 * hw:base + 2 * hw + A_HEAD_DIM].astype(BF16)
            n_sub = min(dil, t_new)
            s_new = lax.dot_general(q, kn, nt, preferred_element_type=F32) * scale
            ok_new = jnp.logical_and(tj <= tq, jnp.bitwise_and(tq - tj, dil - 1) == 0)
            s_new = jnp.where(ok_new, s_new, -jnp.inf)
            ss, vs = [], []
            for r in range(n_sub):
                kb = caches[g][0, :, r * 2 * hw + h * A_HEAD_DIM:r * 2 * hw + (h + 1) * A_HEAD_DIM]
                vb = caches[g][0, :, r * 2 * hw + hw + h * A_HEAD_DIM:r * 2 * hw + hw + (h + 1) * A_HEAD_DIM]
                s = lax.dot_general(q, kb.astype(BF16), nt, preferred_element_type=F32) * scale
                ok = jnp.logical_and(jnp.bitwise_and(ti, dil - 1) == r, ni * dil + r >= ti)
                ss.append(jnp.where(ok, s, -jnp.inf))
                vs.append(vb.astype(BF16))
            m = jnp.max(s_new, axis=-1, keepdims=True)
            for s in ss:
                m = jnp.maximum(m, jnp.max(s, axis=-1, keepdims=True))
            e_new = jnp.exp(s_new - m)
            den = jnp.sum(e_new, axis=-1, keepdims=True)
            o = jnp.dot(e_new.astype(BF16), vn, preferred_element_type=F32)
            for s, vb in zip(ss, vs):
                e = jnp.exp(s - m)
                den = den + jnp.sum(e, axis=-1, keepdims=True)
                o = o + jnp.dot(e.astype(BF16), vb, preferred_element_type=F32)
            outs.append(o / den)
            lses.append(m + jnp.log(den))
        lm = jnp.maximum(jnp.maximum(lses[0], lses[1]), lses[2])
        ws = [jnp.exp(l - lm) for l in lses]
        tot = ws[0] + ws[1] + ws[2]
        comb = (ws[0] * outs[0] + ws[1] * outs[1] + ws[2] * outs[2]) / tot
        o_ref[0, :, h * A_HEAD_DIM:(h + 1) * A_HEAD_DIM] = comb


def attn_a_sample(qkv_s, caches, layer):
    db, t_new, width = qkv_s.shape
    hw = A_HEADS * A_HEAD_DIM
    in_specs = [pl.BlockSpec((1, t_new, width), lambda b: (b, 0, 0))]
    args = [qkv_s]
    for (win, dil), c in zip(A_GROUPS, caches):
        nl = c.shape[0]
        view = c.reshape(nl * db, win // dil, dil * 2 * hw)
        n_sub = min(dil, t_new)
        in_specs.append(pl.BlockSpec((1, win // dil, n_sub * 2 * hw),
                                     lambda b, layer=layer: (layer * db + b, 0, 0)))
        args.append(view)
    return pl.pallas_call(
        _attn_a_sample_kernel, grid=(db,), in_specs=in_specs,
        out_specs=pl.BlockSpec((1, t_new, hw), lambda b: (b, 0, 0)),
        out_shape=jax.ShapeDtypeStruct((db, t_new, hw), F32),
        compiler_params=_params(("parallel",)), name="attn_a_sample")(*args)


def _lambda(lp_ref, lam_init):
    lp = lp_ref[...]
    a = jnp.sum(lp[0:1, :] * lp[1:2, :], axis=-1, keepdims=True)
    b = jnp.sum(lp[2:3, :] * lp[3:4, :], axis=-1, keepdims=True)
    return jnp.exp(a) - jnp.exp(b) + lam_init


def _sub_ln(o, g, lam_init):
    on = o * lax.rsqrt(jnp.mean(o * o, axis=-1, keepdims=True) + SUBLN_EPS) * g
    return on * (1.0 - lam_init)


def _attn_b_kernel(q_ref, k_ref, v_ref, lp_ref, g_ref, o_ref, m_ref, l_ref, acc_ref, *, lam_init, tq, tk, sub):
    qi = pl.program_id(2)
    ki = pl.program_id(3)
    nk = pl.num_programs(3)
    nt = (((1,), (1,)), ((), ()))
    sub = min(tq, sub)

    @pl.when(ki == 0)
    def _():
        m_ref[...] = jnp.full(m_ref.shape, -jnp.inf, F32)
        l_ref[...] = jnp.zeros(l_ref.shape, F32)
        acc_ref[...] = jnp.zeros(acc_ref.shape, F32)

    def step(masked):
        v = v_ref[...].astype(BF16)
        if masked:
            rows = qi * tq + lax.broadcasted_iota(jnp.int32, (tq, tk), 0)
            cols = ki * tk + lax.broadcasted_iota(jnp.int32, (tq, tk), 1)
            mask = cols <= rows
        for c in range(2):
            sl = slice(c * B_QK_DIM, (c + 1) * B_QK_DIM)
            kc = k_ref[:, sl].astype(BF16)
            for r in range(tq // sub):
                rs = slice(r * sub, (r + 1) * sub)
                s = lax.dot_general(q_ref[rs, sl], kc, nt, preferred_element_type=F32)
                if masked:
                    s = jnp.where(mask[rs, :], s, -jnp.inf)
                m_old = m_ref[c, rs, :]
                m_new = jnp.maximum(m_old, jnp.max(s, axis=-1, keepdims=True))
                alpha = jnp.exp2(m_old - m_new)
                p = jnp.exp2(s - m_new)
                l_ref[c, rs, :] = alpha * l_ref[c, rs, :] + jnp.sum(p, axis=-1, keepdims=True)
                acc_ref[c, rs, :] = (alpha * acc_ref[c, rs, :]
                                     + jnp.dot(p.astype(BF16), v, preferred_element_type=F32))
                m_ref[c, rs, :] = m_new

    first_row = qi * tq
    last_col = ki * tk + tk - 1

    @pl.when(last_col <= first_row)
    def _():
        step(False)

    @pl.when(jnp.logical_and(last_col > first_row, ki * tk <= first_row + tq - 1))
    def _():
        step(True)

    @pl.when(ki == nk - 1)
    def _():
        lam = _lambda(lp_ref, lam_init)
        o = acc_ref[0] / l_ref[0] - lam * (acc_ref[1] / l_ref[1])
        o_ref[...] = _sub_ln(o, g_ref[...], lam_init).astype(BF16)


def attn_b_prompt(q, kv, lp, g_sub, lam_init, batch, seq, *, tq=1024, tk=1024, sub=128):
    tq = min(tq, seq)
    tk = min(tk, seq)
    nq, nk = seq // tq, seq // tk
    hd = 2 * B_QK_DIM

    def kmap(b, h, i, j):
        return (b * nk + jnp.minimum(j, (i * tq + tq - 1) // tk), h)

    def vmap_(b, h, i, j):
        return (b * nk + jnp.minimum(j, (i * tq + tq - 1) // tk), B_HEADS + h)

    kern = functools.partial(_attn_b_kernel, lam_init=lam_init, tq=tq, tk=tk, sub=sub)
    return pl.pallas_call(
        kern, grid=(batch, B_HEADS, nq, nk),
        in_specs=[pl.BlockSpec((tq, hd), lambda b, h, i, j: (b * nq + i, h)),
                  pl.BlockSpec((tk, hd), kmap),
                  pl.BlockSpec((tk, B_V_DIM), vmap_),
                  pl.BlockSpec((4, B_QK_DIM), lambda b, h, i, j: (0, 0)),
                  pl.BlockSpec((1, B_V_DIM), lambda b, h, i, j: (0, 0))],
        out_specs=pl.BlockSpec((tq, B_V_DIM), lambda b, h, i, j: (b * nq + i, h)),
        out_shape=jax.ShapeDtypeStruct((batch * seq, B_HEADS * B_V_DIM), BF16),
        scratch_shapes=[pltpu.VMEM((2, tq, 1), F32), pltpu.VMEM((2, tq, 1), F32),
                        pltpu.VMEM((2, tq, B_V_DIM), F32)],
        compiler_params=_params(("parallel", "parallel", "parallel", "arbitrary")),
        name="attn_b_prompt")(q, kv, kv, lp, g_sub.reshape(1, B_V_DIM))


def _attn_b_sample_kernel(*refs, lam_init, t_new, n_par):
    pt_ref, q_ref = refs[0], refs[1]
    page_refs = refs[2:2 + 2 * n_par]
    kvn_ref, lp_ref, g_ref, o_ref, m_ref, l_ref, acc_ref, xs_ref = refs[2 + 2 * n_par:]
    p = pl.program_id(1)
    n_steps = pl.num_programs(1)
    kw = B_HEADS * 2 * B_QK_DIM
    nt = (((1,), (1,)), ((), ()))
    rph = 2 * t_new

    @pl.when(p == 0)
    def _():
        m_ref[...] = jnp.full(m_ref.shape, -jnp.inf, F32)
        l_ref[...] = jnp.zeros(l_ref.shape, F32)
        acc_ref[...] = jnp.zeros(acc_ref.shape, F32)

    def update(state, s, weigh):
        m_old, l_old, acc_old = state
        m_new = jnp.maximum(m_old, jnp.max(s, axis=-1, keepdims=True))
        alpha = jnp.exp2(m_old - m_new)
        e = jnp.exp2(s - m_new)
        return m_new, alpha * l_old + jnp.sum(e, axis=-1, keepdims=True), alpha * acc_old + weigh(e)

    half_heads = B_HEADS // 2
    rows_per_page = 4 * PAGE_SIZE
    for g in range(half_heads):
        for half in range(2):
            for k in range(n_par):
                xs_ref[g, half, k * rows_per_page:(k + 1) * rows_per_page, :] = (
                    page_refs[2 * k + half][0, pl.ds(g, rows_per_page, stride=half_heads), :].astype(BF16))

    def pair_rows(x, g):
        return jnp.concatenate([x[g * rph:(g + 1) * rph], x[(g + half_heads) * rph:(g + half_heads + 1) * rph]],
                               axis=0)

    q_all = q_ref[0]
    lo, hi = [None] * half_heads, [None] * half_heads
    for g in range(half_heads):
        qg = pair_rows(q_all, g)
        sg = (lax.dot_general(qg[:, 0:B_QK_DIM], xs_ref[g, 0], nt, preferred_element_type=F32)
              + lax.dot_general(qg[:, B_QK_DIM:], xs_ref[g, 1], nt, preferred_element_type=F32))
        lo[g], hi[g] = sg[0:rph], sg[rph:2 * rph]
    s = jnp.concatenate(lo + hi, axis=0)
    kind = jnp.bitwise_and(lax.broadcasted_iota(jnp.int32, s.shape, 1), 3)
    own = (lax.broadcasted_iota(jnp.int32, s.shape, 0) >= half_heads * rph).astype(jnp.int32)
    s = jnp.where(kind == own, s, -jnp.inf)

    def weigh(e):
        ev = pltpu.roll(e, 2, 1)
        lo_o, hi_o = [None] * half_heads, [None] * half_heads
        for g in range(half_heads):
            eg = pair_rows(ev, g).astype(BF16)
            og = jnp.concatenate([jnp.dot(eg, xs_ref[g, 0], preferred_element_type=F32),
                                  jnp.dot(eg, xs_ref[g, 1], preferred_element_type=F32)], axis=1)
            lo_o[g], hi_o[g] = og[0:rph], og[rph:2 * rph]
        return jnp.concatenate(lo_o + hi_o, axis=0)

    m_new, l_new, acc_new = update((m_ref[...], l_ref[...], acc_ref[...]), s, weigh)
    m_ref[...] = m_new
    l_ref[...] = l_new
    acc_ref[...] = acc_new

    @pl.when(p == n_steps - 1)
    def _():
        lam = _lambda(lp_ref, lam_init)
        for h in range(B_HEADS):
            rs = slice(h * rph, (h + 1) * rph)
            qh = q_ref[0, rs, :]
            kn = kvn_ref[0, :, h * B_V_DIM:(h + 1) * B_V_DIM].astype(BF16)
            vn = kvn_ref[0, :, kw + h * B_V_DIM:kw + (h + 1) * B_V_DIM].astype(BF16)
            sn = lax.dot_general(qh, kn, nt, preferred_element_type=F32)
            rq = jnp.bitwise_and(lax.broadcasted_iota(jnp.int32, sn.shape, 0), t_new - 1)
            cj = lax.broadcasted_iota(jnp.int32, sn.shape, 1)
            _, l_fin, acc_fin = update(
                (m_ref[rs, :], l_ref[rs, :], acc_ref[rs, :]), jnp.where(cj <= rq, sn, -jnp.inf),
                lambda e, vn=vn: jnp.dot(e.astype(BF16), vn, preferred_element_type=F32))
            on = acc_fin / l_fin
            o = on[0:t_new, :] - lam * on[t_new:2 * t_new, :]
            o_ref[0, :, h * B_V_DIM:(h + 1) * B_V_DIM] = _sub_ln(o, g_ref[...], lam_init)


def attn_b_sample(q_s, kv_s, cache_b_kv, page_table, lp, g_sub, lam_init):
    db, t_new, qw = q_s.shape
    n_pages = page_table.shape[1]
    n_phys = cache_b_kv.shape[0]
    assert t_new & (t_new - 1) == 0
    n_par = next(k for k in (8, 4, 2, 1) if n_pages % k == 0)
    kvw = 2 * B_HEADS * B_V_DIM
    page_rows = PAGE_SIZE * 2 * B_HEADS
    pages = cache_b_kv.reshape(n_phys, page_rows, B_V_DIM)
    q5 = q_s.reshape(db, t_new, B_HEADS, 2, B_QK_DIM).transpose(0, 2, 3, 1, 4)
    eye = jnp.eye(2, dtype=q_s.dtype)
    qbd = q5[:, :, :, :, None, :] * eye[None, None, :, None, :, None]
    rows = B_HEADS * 2 * t_new
    qbd = qbd.reshape(db, rows, 2 * B_QK_DIM).astype(BF16)
    kern = functools.partial(_attn_b_sample_kernel, lam_init=lam_init, t_new=t_new, n_par=n_par)
    page_specs = [pl.BlockSpec((1, page_rows, B_QK_DIM),
                               lambda b, p, pt, k=k, half=half: (pt[b, p * n_par + k], 0, half))
                  for k in range(n_par) for half in range(2)]
    grid_spec = pltpu.PrefetchScalarGridSpec(
        num_scalar_prefetch=1, grid=(db, n_pages // n_par),
        in_specs=[pl.BlockSpec((1, rows, 2 * B_QK_DIM), lambda b, p, pt: (b, 0, 0))] + page_specs + [
                  pl.BlockSpec((1, t_new, kvw), lambda b, p, pt: (b, 0, 0)),
                  pl.BlockSpec((4, B_QK_DIM), lambda b, p, pt: (0, 0)),
                  pl.BlockSpec((1, B_V_DIM), lambda b, p, pt: (0, 0))],
        out_specs=pl.BlockSpec((1, t_new, qw), lambda b, p, pt: (b, 0, 0)),
        scratch_shapes=[pltpu.VMEM((rows, 1), F32), pltpu.VMEM((rows, 1), F32),
                        pltpu.VMEM((rows, B_V_DIM), F32),
                        pltpu.VMEM((B_HEADS // 2, 2, n_par * 4 * PAGE_SIZE, B_QK_DIM), BF16)])
    return pl.pallas_call(
        kern, grid_spec=grid_spec,
        out_shape=jax.ShapeDtypeStruct((db, t_new, qw), F32),
        compiler_params=_params(("parallel", "arbitrary")),
        name="attn_b_sample")(page_table, qbd, *([pages] * (2 * n_par)), kv_s, lp, g_sub.reshape(1, B_V_DIM))


def _top_ranks(s, exact):
    iota = lax.broadcasted_iota(jnp.int32, s.shape, 0).astype(F32)
    rank = jnp.full(s.shape, NOT_SELECTED, F32)
    vals = []
    work = s
    for a in range(PEER_TOPK):
        m = jnp.max(work, axis=0, keepdims=True)
        if exact:
            idx = jnp.min(jnp.where(work == m, iota, float(N_KEYS)), axis=0, keepdims=True)
            hit = iota == idx
        else:
            hit = work == m
        rank = jnp.where(hit, float(a), rank)
        work = jnp.where(hit, -jnp.inf, work)
        vals.append(m)
    taken = jnp.sum(jnp.where(rank < NOT_SELECTED, 1.0, 0.0), axis=0, keepdims=True)
    return vals, rank, taken


def _candidate_rows():
    pieces = [(0, 0, 16, 16)]
    for a in range(1, 8):
        pieces.append((a, 0, 8, PEER_TOPK // (a + 1)))
    return pieces


def _route_kernel(ht_ref, wq_ref, sk_ref, r1_ref, e1_ref, bq_out_ref, c0_out_ref, qt_ref, bq_ref, c0_ref):
    n_chunks = ht_ref.shape[1] // LANES
    qt_ref[...] = jnp.dot(wq_ref[...], ht_ref[...], preferred_element_type=F32)
    neg = -jnp.inf
    t = LANES

    def route_chunk(h, cs, scores, exact):
        vals, ranks = [], []
        off = jnp.zeros((1, t), F32)
        for c in range(2):
            v, r, taken = _top_ranks(scores[c], exact)
            vals.append(v); ranks.append(r)
            off = off + jnp.abs(taken - float(PEER_TOPK))
        v0, v1 = vals
        v1_16 = jnp.concatenate(v1, axis=0)
        v0_hi = jnp.concatenate(v0[8:16], axis=0)
        cands, flats = [], []
        for a, _, rows, nvalid in _candidate_rows():
            b_iota = lax.broadcasted_iota(jnp.int32, (rows, t), 0)
            cs_ab = v0[a] + v1_16[0:rows, :]
            cands.append(jnp.where(b_iota < nvalid, cs_ab, neg))
            flats.append((b_iota + a * PEER_TOPK).astype(F32))
        cands.append(v0_hi + v1[0])
        flats.append(((lax.broadcasted_iota(jnp.int32, (8, t), 0) + 8) * PEER_TOPK).astype(F32))
        cand = jnp.concatenate(cands, axis=0)
        flat = jnp.concatenate(flats, axis=0)
        big = float(PEER_TOPK * PEER_TOPK)
        work = cand
        sel = jnp.zeros(cand.shape, F32)
        for _ in range(PEER_TOPK):
            m = jnp.max(work, axis=0, keepdims=True)
            if exact:
                idx = jnp.min(jnp.where(work == m, flat, big), axis=0, keepdims=True)
                hit = flat == idx
            else:
                hit = work == m
            sel = jnp.where(hit, 1.0, sel)
            work = jnp.where(hit, neg, work)
        off = off + jnp.abs(jnp.sum(sel, axis=0, keepdims=True) - float(PEER_TOPK))
        top = v0[0] + v1[0]
        z = jnp.sum(jnp.where(sel > 0.0, jnp.exp(cand - top), 0.0), axis=0, keepdims=True)
        counts = [jnp.sum(sel[0:16, :], axis=0, keepdims=True)]
        for k in range(1, 8):
            counts.append(jnp.sum(sel[8 + 8 * k:16 + 8 * k, :], axis=0, keepdims=True))
        hi = sel[72:80, :]
        bq = jnp.zeros(ranks[0].shape, F32)
        for a in range(PEER_TOPK):
            cnt = counts[a] if a < 8 else hi[a - 8:a - 7, :]
            bq = jnp.where(ranks[0] == float(a), cnt, bq)
        rs = pl.ds(pl.multiple_of(h * N_KEYS, N_KEYS), N_KEYS)
        bq_ref[rs, cs] = bq
        c0_ref[rs, cs] = jnp.exp(scores[0] - v0[0]) * (0.5 / z)
        r1_ref[rs, cs] = ranks[1].astype(BF16)
        e1_ref[rs, cs] = jnp.exp(scores[1] - v1[0]).astype(BF16)
        return off

    def body(h, carry):
        work = []
        for k in range(n_chunks):
            cs = slice(k * LANES, (k + 1) * LANES)
            scores = []
            for c in range(2):
                hc = h * 2 + c
                qhc = qt_ref[pl.ds(pl.multiple_of(hc * N_KEYS, N_KEYS), N_KEYS), cs].astype(BF16)
                scores.append(jnp.dot(sk_ref[hc].astype(BF16), qhc, preferred_element_type=F32))
            work.append((cs, scores, route_chunk(h, cs, scores, exact=False)))
        for cs, scores, off in work:
            @pl.when(jnp.max(off) > 0.0)
            def _(cs=cs, scores=scores):
                route_chunk(h, cs, scores, exact=True)

        return carry

    lax.fori_loop(0, PEER_HEADS, body, 0)
    for h in range(PEER_HEADS):
        bq_out_ref[:, h, :] = bq_ref[h * N_KEYS:(h + 1) * N_KEYS, :]
        c0_out_ref[:, h, :] = c0_ref[h * N_KEYS:(h + 1) * N_KEYS, :]


def peer_route(ht, wq_t, subkeys, *, tm=512):
    d, n = ht.shape
    rows = PEER_HEADS * N_KEYS
    slab = pl.BlockSpec((rows, tm), lambda i: (0, i))
    sds = jax.ShapeDtypeStruct((rows, n), BF16)
    slab3 = pl.BlockSpec((N_KEYS, PEER_HEADS, tm), lambda i: (0, 0, i))
    sds3 = jax.ShapeDtypeStruct((N_KEYS, PEER_HEADS, n), F32)
    return pl.pallas_call(
        _route_kernel, grid=(n // tm,),
        in_specs=[pl.BlockSpec((d, tm), lambda i: (0, i)),
                  pl.BlockSpec(wq_t.shape, lambda i: (0, 0)),
                  pl.BlockSpec(subkeys.shape, lambda i: (0, 0, 0))],
        out_specs=[slab, slab, slab3, slab3], out_shape=[sds, sds, sds3, sds3],
        scratch_shapes=[pltpu.VMEM((wq_t.shape[0], tm), F32), pltpu.VMEM((rows, tm), F32),
                        pltpu.VMEM((rows, tm), F32)],
        compiler_params=_params(("parallel",)), name="peer_route")(ht, wq_t, subkeys)


def _gelu_x2(a):
    return a * (1.0 + lax.erf(a * math.sqrt(0.5)))


def _expert_kernel(x_ref, ht_ref, u_ref, v_ref, r1_ref, e1_ref, bq_ref, c0_ref, o_ref, *a_refs, te, tm, ge):
    e = pl.program_id(1)
    rows_per_group = ge // N_KEYS
    n_chunks = tm // LANES
    packed = 2 * SUBLANES
    tiles = N_KEYS // packed
    nn = (((1,), (0,)), ((), ()))

    @pl.when(e == 0)
    def _():
        o_ref[...] = x_ref[...]

    n_groups = te // ge
    for k in range(n_groups):
        a_refs[k][...] = lax.dot_general(u_ref[k * ge:(k + 1) * ge, :], ht_ref[...], nn,
                                         preferred_element_type=F32)
    groups = []
    for k in range(n_groups):
        rows = []
        for r in range(rows_per_group):
            i = e * (te // N_KEYS) + k * rows_per_group + r
            cols = []
            for c in range(n_chunks):
                cs = slice(c * LANES, (c + 1) * LANES)
                bq_all = bq_ref[i, :, cs]
                c0_all = c0_ref[i, :, cs]
                w = [jnp.zeros((packed, LANES), BF16)] * tiles
                for h in range(PEER_HEADS):
                    bq = jnp.broadcast_to(bq_all[h:h + 1, :], (packed, LANES)).astype(BF16)
                    c0 = jnp.broadcast_to(c0_all[h:h + 1, :], (packed, LANES)).astype(BF16)
                    for t in range(tiles):
                        js = slice(h * N_KEYS + t * packed, h * N_KEYS + (t + 1) * packed)
                        e1 = e1_ref[js, cs]
                        w[t] = w[t] + jnp.where(r1_ref[js, cs] < bq, e1 * c0, jnp.zeros_like(e1))
                gate = jnp.concatenate(w, axis=0).astype(F32)
                cols.append(_gelu_x2(a_refs[k][r * N_KEYS:(r + 1) * N_KEYS, cs]) * gate)
            rows.append(jnp.concatenate(cols, axis=1))
        groups.append(jnp.concatenate(rows, axis=0).T)
    g = jnp.concatenate(groups, axis=1)
    o_ref[...] += lax.dot_general(g, v_ref[...], nn, preferred_element_type=F32)


def peer_experts(x, ht, u, v_bf, layer, slabs, *, tm=512, te=1024, ge=256):
    d, n = ht.shape
    n_exp = u.shape[1]
    while n % tm:
        tm //= 2
    te = min(te, n_exp)
    rows = PEER_HEADS * N_KEYS
    once = pl.Buffered(1)
    slab = pl.BlockSpec((rows, tm), lambda i, e: (0, i), pipeline_mode=once)
    slab3 = pl.BlockSpec((N_KEYS, PEER_HEADS, tm), lambda i, e: (0, 0, i), pipeline_mode=once)
    kern = functools.partial(_expert_kernel, te=te, tm=tm, ge=ge)
    return pl.pallas_call(
        kern, grid=(n // tm, n_exp // te),
        in_specs=[pl.BlockSpec((tm, d), lambda i, e: (i, 0)),
                  pl.BlockSpec((d, tm), lambda i, e: (0, i)),
                  pl.BlockSpec((None, te, d), lambda i, e: (layer, e, 0)),
                  pl.BlockSpec((None, te, d), lambda i, e: (layer, e, 0)),
                  slab, slab, slab3, slab3],
        out_specs=pl.BlockSpec((tm, d), lambda i, e: (i, 0)),
        out_shape=jax.ShapeDtypeStruct((n, d), F32),
        scratch_shapes=[pltpu.VMEM((ge, tm), F32) for _ in range(te // ge)],
        compiler_params=_params(("parallel", "arbitrary")),
        name="peer_experts")(x, ht, u, v_bf, *slabs)


def _rope_table(pos):
    rot = A_HEAD_DIM // 4
    half = rot // 2
    inv_freq = ROPE_THETA ** (-jnp.arange(half, dtype=F32) / half)
    ang = pos.astype(F32)[:, None] * inv_freq[None, :]
    cos, sin = jnp.cos(ang), jnp.sin(ang)
    n = pos.shape[0]
    ones = jnp.ones((n, A_HEAD_DIM - rot), F32)
    zeros = jnp.zeros((n, A_HEAD_DIM - rot), F32)
    zh = jnp.zeros((n, half), F32)
    return jnp.concatenate([cos, cos, ones, -sin, zh, zeros, zh, sin, zeros], axis=1)


def kernel(x_prompt, x_sample, cache_a_w128, cache_a_w512, cache_a_w2048, cache_b_kv, page_table,
           p_prompt, p_sample, norm_mix, norm_ffn, norm_ple, norm_kv, norm_final,
           w_qkv_a, w_o_a, w_kv_b, w_q_b, diff_lambda, norm_sub_b, w_o_b,
           peer_wq, peer_subkeys, peer_u, peer_v, w_ple, w_ple_gate):
    batch, seq, d = x_prompt.shape
    db, t_new, _ = x_sample.shape
    depth = norm_mix.shape[0]
    n_a = w_qkv_a.shape[0]
    past_len = page_table.shape[1] * PAGE_SIZE
    n_p, n_s = batch * seq, db * t_new
    n_tot = -(-(n_p + n_s) // TOKEN_PAD) * TOKEN_PAD
    pad = n_tot - n_p - n_s
    a_caches = (cache_a_w128, cache_a_w512, cache_a_w2048)
    hw = A_HEADS * A_HEAD_DIM

    def tokens(prompt_part, sample_part):
        w = prompt_part.shape[-1]
        return jnp.concatenate([prompt_part.reshape(n_p, w), sample_part.reshape(n_s, w),
                                jnp.zeros((pad, w), prompt_part.dtype)], axis=0)

    x = tokens(x_prompt, x_sample)
    pos = jnp.concatenate([jnp.tile(jnp.arange(seq, dtype=jnp.int32), batch),
                           jnp.tile(past_len + jnp.arange(t_new, dtype=jnp.int32), db),
                           jnp.zeros((pad,), jnp.int32)])
    tab = _rope_table(pos)
    v_bf = peer_v.astype(BF16)

    a_rows_p = [[] for _ in A_GROUPS]
    a_rows_s = [[] for _ in A_GROUPS]
    new_b_kv_prompt = new_b_kv_sample = kv = None
    for i in range(depth):
        h = rms_norm(x, norm_mix[i], out_h=True)[0]
        if i < n_a:
            qkv = matmul(h, w_qkv_a, layer=i, mode="rope", tab=tab, tn=hw,
                         rope_fn=lambda col: (col // hw) % 3 != 2)
            o_p = attn_a_prompt(qkv, batch, seq)
            qkv_s = qkv[n_p:n_p + n_s].reshape(db, t_new, qkv.shape[1])
            o_s = attn_a_sample(qkv_s, a_caches, i).reshape(n_s, hw).astype(BF16)
            o_all = jnp.concatenate([o_p, o_s, jnp.zeros((pad, hw), BF16)], axis=0)
            x = matmul(o_all, w_o_a, layer=i, mode="res", res=x, tn=1024)
            for g, (win, dil) in enumerate(A_GROUPS):
                wb = min(win, seq)
                c0, c1 = (g * 3 + 1) * hw, (g * 3 + 3) * hw
                rows = jnp.stack([qkv[(b + 1) * seq - wb:(b + 1) * seq, c0:c1] for b in range(batch)], axis=0)
                a_rows_p[g].append(rows.reshape(batch, wb, 2, A_HEADS, A_HEAD_DIM))
                a_rows_s[g].append(qkv[n_p:n_p + n_s, c0:c1].reshape(db, t_new, 2, A_HEADS, A_HEAD_DIM))
        else:
            j = i - n_a
            if j == 0:
                hkv = rms_norm(x, norm_kv, out_h=True)[0]
                kw = B_HEADS * 2 * B_QK_DIM
                kv = matmul(hkv, w_kv_b, mode="rope", tab=tab, tn=1024, rope_fn=lambda col: col < kw)
                new_b_kv_prompt = kv[:n_p].reshape(batch, seq, 2, B_HEADS, B_V_DIM)
                new_b_kv_sample = kv[n_p:n_p + n_s].reshape(db, t_new, 2, B_HEADS, B_V_DIM)
            lam_init = 0.8 - 0.6 * math.exp(-0.3 * i)
            q = matmul(h, w_q_b, layer=j, mode="rope", tab=tab, tn=1024, rope_fn=lambda col: col >= 0,
                       out_scale=math.log2(math.e) / math.sqrt(B_QK_DIM), out_dtype=BF16)
            o_p = attn_b_prompt(q, kv, diff_lambda[j], norm_sub_b[j], lam_init, batch, seq)
            q_s = q[n_p:n_p + n_s].reshape(db, t_new, q.shape[1])
            kv_s = kv[n_p:n_p + n_s].reshape(db, t_new, kv.shape[1])
            o_s = attn_b_sample(q_s, kv_s, cache_b_kv, page_table, diff_lambda[j], norm_sub_b[j], lam_init)
            o_all = jnp.concatenate([o_p, o_s.reshape(n_s, -1).astype(BF16),
                                     jnp.zeros((pad, o_p.shape[1]), BF16)], axis=0)
            x = matmul(o_all, w_o_b, layer=j, mode="res", res=x, tn=1024)
        ht = rms_norm(x, norm_ffn[i], out_ht=True)[0]
        wq_t = peer_wq[i].T.astype(BF16)
        sk = peer_subkeys[i].reshape(PEER_HEADS * 2, N_KEYS, -1)
        slabs = peer_route(ht, wq_t, sk)
        x = peer_experts(x, ht, peer_u, v_bf, i, slabs)
        hn = rms_norm(x, norm_ple[i], out_h=True)[0]
        x = ple(x, tokens(p_prompt[i], p_sample[i]), hn, w_ple, w_ple_gate, i)

    y = rms_norm(x, norm_final, out_y=True)[0]
    y_prompt = y[:n_p].reshape(batch, seq, d)
    y_sample = y[n_p:n_p + n_s].reshape(db, t_new, d)
    outs_p = [jnp.stack(r, axis=0) for r in a_rows_p]
    outs_s = [jnp.stack(r, axis=0) for r in a_rows_s]
    return (y_prompt, y_sample, *outs_p, *outs_s, new_b_kv_prompt, new_b_kv_sample)
```
